```python
import jax
import jax.numpy as jnp
from jax import lax
import numpy as np

D_MODEL = 1024
BATCH = 2
SEQ = 8192
DEPTH = 4

HEAD_DIM = 64
PLE_DIM = 256
ROPE_THETA = 10000.0
BLOCK = 128
N_MIXERS = 3
LN_EPS = 1e-5
DEEPNORM_ALPHA = (2 * DEPTH) ** 0.25
DEEPNORM_BETA = (8 * DEPTH) ** -0.25

A_GROUPS = ((128, 1), (512, 4), (2048, 16))
A_HEADS = D_MODEL // HEAD_DIM
A_WIDTH = A_HEADS * HEAD_DIM
A_IN = 3 * len(A_GROUPS) * A_WIDTH + A_WIDTH

B_HEADS = D_MODEL // HEAD_DIM
B_KV_HEADS = 2
B_WINDOW = 128
B_WIDTH = B_HEADS * HEAD_DIM
B_KV = B_KV_HEADS * HEAD_DIM
B_IN = 2 * B_WIDTH + 2 * B_KV

C_HEADS = D_MODEL // HEAD_DIM
C_KV_HEADS = 4
C_WIDTH = C_HEADS * HEAD_DIM
C_KV = C_KV_HEADS * HEAD_DIM
C_CMP_LEN = 32
C_CMP_STRIDE = 16
C_SEL_LEN = 64
C_N_SEL = 16
C_WINDOW = 512
C_SEL_OVERLAP = (1.0, 2.0, 2.0, 2.0, 1.0)
C_IN = 2 * C_WIDTH + 6 * C_KV + 3 * C_HEADS

N_A_LAYERS = len(range(0, DEPTH, N_MIXERS))
N_B_LAYERS = len(range(1, DEPTH, N_MIXERS))
N_C_LAYERS = len(range(2, DEPTH, N_MIXERS))

kernel_name = 'hybrid_dilated_swa_nsa_deepnorm'


def split_cols(h, sizes):
    out, start = [], 0
    for s in sizes:
        out.append(h[..., start:start + s])
        start += s
    return out


def rope_tables(seq_len):
    inv = 1.0 / (ROPE_THETA ** (jnp.arange(0, HEAD_DIM, 2, dtype=jnp.float32) / HEAD_DIM))
    ang = jnp.arange(seq_len, dtype=jnp.float32)[:, None] * inv[None, :]
    return jnp.cos(ang), jnp.sin(ang)


def apply_rope(t, cos, sin):
    tf = t.astype(jnp.float32)
    half = HEAD_DIM // 2
    t1, t2 = tf[..., :half], tf[..., half:]
    c, s = cos[None, :, None, :], sin[None, :, None, :]
    return jnp.concatenate([t1 * c - t2 * s, t2 * c + t1 * s], axis=-1).astype(t.dtype)


def layer_norm(x, g, b):
    xf = x.astype(jnp.float32)
    mu = jnp.mean(xf, axis=-1, keepdims=True)
    var = jnp.mean(jnp.square(xf - mu), axis=-1, keepdims=True)
    y = (xf - mu) * lax.rsqrt(var + LN_EPS) * g.astype(jnp.float32) + b.astype(jnp.float32)
    return y.astype(x.dtype)


def banded_attention(q, k, v, max_dist, sinks=None):
    B_, L, Hk, G, dh = q.shape
    nb = -(-L // BLOCK)
    pad = nb * BLOCK - L
    npv = -(-max_dist // BLOCK)
    W = (npv + 1) * BLOCK
    qb = jnp.pad(q, ((0, 0), (0, pad), (0, 0), (0, 0), (0, 0))).reshape(B_, nb, BLOCK, Hk, G, dh)

    def windows(t):
        tp = jnp.pad(t, ((0, 0), (npv * BLOCK, pad), (0, 0), (0, 0))).reshape(B_, nb + npv, BLOCK, Hk, dh)
        return jnp.concatenate([tp[:, j:j + nb] for j in range(npv + 1)], axis=2)

    kw, vw = windows(k), windows(v)
    s = jnp.einsum('bnqhgd,bnkhd->bnhgqk', qb, kw).astype(jnp.float32) * (dh ** -0.5)
    dist = np.arange(BLOCK)[:, None] + npv * BLOCK - np.arange(W)[None, :]
    band = (dist >= 0) & (dist <= max_dist)
    kpos = np.arange(nb)[:, None] * BLOCK - npv * BLOCK + np.arange(W)[None, :]
    mask = band[None, :, :] & (kpos >= 0)[:, None, :]
    s = jnp.where(mask[None, :, None, None, :, :], s, -jnp.inf)
    m = jnp.max(s, axis=-1, keepdims=True)
    if sinks is not None:
        sk = sinks.astype(jnp.float32).reshape(1, 1, Hk, G, 1, 1)
        m = jnp.maximum(m, sk)
    e = jnp.exp(s - m)
    l = jnp.sum(e, axis=-1, keepdims=True)
    if sinks is not None:
        l = l + jnp.exp(sk - m)
    o = jnp.einsum('bnhgqk,bnkhd->bnqhgd', e, vw.astype(jnp.float32))
    o = o / jnp.transpose(l, (0, 1, 4, 2, 3, 5))
    lse = jnp.transpose((m + jnp.log(l))[..., 0], (0, 1, 4, 2, 3))
    o = o.reshape(B_, nb * BLOCK, Hk, G, dh)[:, :L]
    lse = lse.reshape(B_, nb * BLOCK, Hk, G)[:, :L]
    return o, lse


def mixer_a(x, w_in, w_out, cos, sin):
    B_, S, _ = x.shape
    h = x @ w_in
    outs, lses = [], []
    for gi, (window, dil) in enumerate(A_GROUPS):
        base = 3 * gi * A_WIDTH
        q, k, v = split_cols(h[..., base:base + 3 * A_WIDTH], [A_WIDTH] * 3)
        q = apply_rope(q.reshape(B_, S, A_HEADS, HEAD_DIM), cos, sin)
        k = apply_rope(k.reshape(B_, S, A_HEADS, HEAD_DIM), cos, sin)
        v = v.reshape(B_, S, A_HEADS, HEAD_DIM)
        Ls = S // dil

        def strided(t):
            return jnp.transpose(t.reshape(B_, Ls, dil, A_HEADS, HEAD_DIM), (0, 2, 1, 3, 4)).reshape(B_ * dil, Ls, A_HEADS, HEAD_DIM)

        o, lse = banded_attention(strided(q)[:, :, :, None, :], strided(k), strided(v), window // dil)
        o = jnp.transpose(o.reshape(B_, dil, Ls, A_HEADS, HEAD_DIM), (0, 2, 1, 3, 4)).reshape(B_, S, A_HEADS, HEAD_DIM)
        lse = jnp.transpose(lse.reshape(B_, dil, Ls, A_HEADS), (0, 2, 1, 3)).reshape(B_, S, A_HEADS)
        outs.append(o)
        lses.append(lse)
    wts = jax.nn.softmax(jnp.stack(lses, axis=0), axis=0)
    o = jnp.einsum('gbsh,gbshd->bshd', wts, jnp.stack(outs, axis=0))
    z = h[..., -A_WIDTH:]
    o = o.reshape(B_, S, A_WIDTH).astype(x.dtype)
    return (o * jax.nn.silu(z)) @ w_out


def mixer_b(x, w_in, sinks, w_out, cos, sin):
    B_, S, _ = x.shape
    G = B_HEADS // B_KV_HEADS
    q, k, v, z = split_cols(x @ w_in, [B_WIDTH, B_KV, B_KV, B_WIDTH])
    q = apply_rope(q.reshape(B_, S, B_HEADS, HEAD_DIM), cos, sin).reshape(B_, S, B_KV_HEADS, G, HEAD_DIM)
    k = apply_rope(k.reshape(B_, S, B_KV_HEADS, HEAD_DIM), cos, sin)
    v = v.reshape(B_, S, B_KV_HEADS, HEAD_DIM)
    o, _ = banded_attention(q, k, v, B_WINDOW - 1, sinks)
    o = o.reshape(B_, S, B_WIDTH).astype(x.dtype)
    return (o * jax.nn.silu(z)) @ w_out


def mixer_c(x, w_in, w_ck, w_cv, pos_cmp, w_out, cos, sin):
    B_, S, _ = x.shape
    Hk, G, dh = C_KV_HEADS, C_HEADS // C_KV_HEADS, HEAD_DIM
    scale = dh ** -0.5
    q, kc, vc, ks, vs, kw, vw, gl, z = split_cols(x @ w_in, [C_WIDTH] + [C_KV] * 6 + [3 * C_HEADS, C_WIDTH])
    q = apply_rope(q.reshape(B_, S, C_HEADS, dh), cos, sin).reshape(B_, S, Hk, G, dh)
    kv_shape = (B_, S, Hk, dh)
    kc = apply_rope(kc.reshape(kv_shape), cos, sin)
    ks = apply_rope(ks.reshape(kv_shape), cos, sin)
    kw = apply_rope(kw.reshape(kv_shape), cos, sin)
    vc, vs, vw = vc.reshape(kv_shape), vs.reshape(kv_shape), vw.reshape(kv_shape)

    n_c = S // C_CMP_STRIDE - 1

    def compress(t, w):
        tr = t.reshape(B_, S // C_CMP_STRIDE, C_CMP_STRIDE, Hk, dh)
        blk = jnp.concatenate([tr[:, :-1], tr[:, 1:]], axis=2)
        return jnp.einsum('bnjhd,jde->bnhe', blk + pos_cmp[None, None, :, None, :], w)

    k_cmp, v_cmp = compress(kc, w_ck), compress(vc, w_cv)
    cmp_end = jnp.arange(n_c) * C_CMP_STRIDE + C_CMP_LEN - 1

    n_s = S // C_SEL_LEN
    n_sel = min(C_N_SEL, n_s)
    ks_blocks = jnp.transpose(ks.reshape(B_, n_s, C_SEL_LEN, Hk, dh), (0, 3, 1, 2, 4))
    vs_blocks = jnp.transpose(vs.reshape(B_, n_s, C_SEL_LEN, Hk, dh), (0, 3, 1, 2, 4))
    bi = jnp.arange(B_)[:, None, None, None]
    hi = jnp.arange(Hk)[None, None, :, None]
    blk_ids = jnp.arange(n_s)

    def query_block(args):
        qb, t0 = args
        tq = t0 + jnp.arange(BLOCK)
        s = jnp.einsum('bqhgd,bnhd->bqhgn', qb, k_cmp).astype(jnp.float32) * scale
        cvalid = cmp_end[None, :] <= tq[:, None]
        s = jnp.where(cvalid[None, :, None, None, :], s, -jnp.inf)
        m = jnp.max(s, axis=-1, keepdims=True)
        m = jnp.where(jnp.isfinite(m), m, 0.0)
        e = jnp.exp(s - m)
        pc = e / jnp.maximum(jnp.sum(e, axis=-1, keepdims=True), 1e-30)
        o_cmp = jnp.einsum('bqhgn,bnhd->bqhgd', pc, v_cmp.astype(jnp.float32))
        imp = jnp.pad(jnp.sum(pc, axis=3), ((0, 0), (0, 0), (0, 0), (1, 1)))
        imp_s = C_SEL_OVERLAP[0] * imp[..., 0::4][..., :n_s]
        for o_off in range(1, len(C_SEL_OVERLAP)):
            imp_s = imp_s + C_SEL_OVERLAP[o_off] * imp[..., o_off::4][..., :n_s]
        cur = tq // C_SEL_LEN
        forced = (blk_ids[None, :] == 0) | (blk_ids[None, :] == cur[:, None]) | (blk_ids[None, :] == cur[:, None] - 1)
        bvalid = blk_ids[None, :] * C_SEL_LEN <= tq[:, None]
        score = jnp.where(forced[None, :, None, :], 1e4, jnp.where(bvalid[None, :, None, :], imp_s, -1.0))
        _, idx = lax.top_k(score, n_sel)
        ksel = ks_blocks[bi, hi, idx]
        vsel = vs_blocks[bi, hi, idx]
        kpos = idx[..., None] * C_SEL_LEN + jnp.arange(C_SEL_LEN)
        smask = kpos <= tq[None, :, None, None, None]
        ss = jnp.einsum('bqhgd,bqhnkd->bqhgnk', qb, ksel).astype(jnp.float32) * scale
        ss = jnp.where(smask[:, :, :, None], ss, -jnp.inf)
        ps = jax.nn.softmax(ss.reshape(ss.shape[:4] + (n_sel * C_SEL_LEN,)), axis=-1).reshape(ss.shape)
        o_slc = jnp.einsum('bqhgnk,bqhnkd->bqhgd', ps, vsel.astype(jnp.float32))
        return o_cmp, o_slc

    nqb = S // BLOCK
    q_blocks = jnp.transpose(q.reshape(B_, nqb, BLOCK, Hk, G, dh), (1, 0, 2, 3, 4, 5))
    t0s = jnp.arange(nqb, dtype=jnp.int32) * BLOCK
    o_cmp, o_slc = lax.map(query_block, (q_blocks, t0s))
    o_cmp = jnp.transpose(o_cmp, (1, 0, 2, 3, 4, 5)).reshape(B_, S, Hk, G, dh)
    o_slc = jnp.transpose(o_slc, (1, 0, 2, 3, 4, 5)).reshape(B_, S, Hk, G, dh)
    o_win, _ = banded_attention(q, kw, vw, C_WINDOW - 1)
    g = jax.nn.sigmoid(gl.astype(jnp.float32)).reshape(B_, S, 3, Hk, G, 1)
    o = g[:, :, 0] * o_cmp + g[:, :, 1] * o_slc + g[:, :, 2] * o_win
    o = o.reshape(B_, S, C_WIDTH).astype(x.dtype)
    return (o * jax.nn.silu(z)) @ w_out


def setup_inputs(seed: int = 0) -> dict:
    key = jax.random.key(seed)
    ks = jax.random.split(key, 16)

    def nrm(k, shape, scale):
        return jax.random.normal(k, shape, jnp.float32) * scale

    return {
        'x': nrm(ks[0], (BATCH, SEQ, D_MODEL), 1.0),
        'p': nrm(ks[1], (DEPTH, BATCH, SEQ, PLE_DIM), 1.0),
        'a_w_in': nrm(ks[2], (N_A_LAYERS, D_MODEL, A_IN), D_MODEL ** -0.5),
        'a_w_out': nrm(ks[3], (N_A_LAYERS, A_WIDTH, D_MODEL), A_WIDTH ** -0.5 * DEEPNORM_BETA),
        'b_w_in': nrm(ks[4], (N_B_LAYERS, D_MODEL, B_IN), D_MODEL ** -0.5),
        'b_sinks': nrm(ks[5], (N_B_LAYERS, B_HEADS), 1.0),
        'b_w_out': nrm(ks[6], (N_B_LAYERS, B_WIDTH, D_MODEL), B_WIDTH ** -0.5 * DEEPNORM_BETA),
        'c_w_in': nrm(ks[7], (N_C_LAYERS, D_MODEL, C_IN), D_MODEL ** -0.5),
        'c_w_ck': nrm(ks[8], (N_C_LAYERS, C_CMP_LEN, HEAD_DIM, HEAD_DIM), (C_CMP_LEN * HEAD_DIM) ** -0.5),
        'c_w_cv': nrm(ks[9], (N_C_LAYERS, C_CMP_LEN, HEAD_DIM, HEAD_DIM), (C_CMP_LEN * HEAD_DIM) ** -0.5),
        'c_pos': nrm(ks[10], (N_C_LAYERS, C_CMP_LEN, HEAD_DIM), 0.5),
        'c_w_out': nrm(ks[11], (N_C_LAYERS, C_WIDTH, D_MODEL), C_WIDTH ** -0.5 * DEEPNORM_BETA),
        'ln_g': 1.0 + nrm(ks[12], (DEPTH, D_MODEL), 0.05),
        'ln_b': nrm(ks[13], (DEPTH, D_MODEL), 0.05),
        'ple_w_proj': nrm(ks[14], (DEPTH, PLE_DIM, D_MODEL), PLE_DIM ** -0.5),
        'ple_w_gate': nrm(ks[15], (DEPTH, D_MODEL, D_MODEL), D_MODEL ** -0.5),
    }


def reference(x, p, a_w_in, a_w_out, b_w_in, b_sinks, b_w_out, c_w_in, c_w_ck, c_w_cv, c_pos, c_w_out, ln_g, ln_b, ple_w_proj, ple_w_gate):
    cos, sin = rope_tables(x.shape[1])
    for i in range(DEPTH):
        j = i // N_MIXERS
        kind = i % N_MIXERS
        if kind == 0:
            h = mixer_a(x, a_w_in[j], a_w_out[j], cos, sin)
        elif kind == 1:
            h = mixer_b(x, b_w_in[j], b_sinks[j], b_w_out[j], cos, sin)
        else:
            h = mixer_c(x, c_w_in[j], c_w_ck[j], c_w_cv[j], c_pos[j], c_w_out[j], cos, sin)
        x = layer_norm(DEEPNORM_ALPHA * x + h, ln_g[i], ln_b[i])
        gate = jax.nn.sigmoid((x @ ple_w_gate[i]).astype(jnp.float32))
        x = (x.astype(jnp.float32) + gate * (p[i] @ ple_w_proj[i]).astype(jnp.float32)).astype(x.dtype)
    return x
```

```python
import functools

import numpy as np
import jax
import jax.numpy as jnp
from jax import lax
from jax.experimental import pallas as pl
from jax.experimental.pallas import tpu as pltpu

F32 = jnp.float32
BF16 = jnp.bfloat16

LANES = 128
VMEM_LIMIT_BYTES = 56 * 1024 * 1024

HEAD_DIM = 64
HALF = HEAD_DIM // 2
PAIR = 2 * HEAD_DIM
assert PAIR == LANES
N_HEADS = 16
N_PAIRS = N_HEADS // 2
ROPE_THETA = 10000.0
QBLK = 128
LN_EPS = 1e-5
DEPTH = 4
N_MIXERS = 3
DEEPNORM_ALPHA = (2 * DEPTH) ** 0.25
A_GROUPS = ((128, 1), (512, 4), (2048, 16))
B_KV_HEADS = 2
B_WINDOW = 128
C_KV_HEADS = 4
C_GROUP = N_HEADS // C_KV_HEADS
C_KV = C_KV_HEADS * HEAD_DIM
C_CMP_STRIDE = 16
C_CMP_LEN = 32
C_SEL_LEN = 64
C_N_SEL = 16
C_WINDOW = 512
C_SEL_OVERLAP = (1.0, 2.0, 2.0, 2.0, 1.0)
SEL_PER_CMP = C_SEL_LEN // C_CMP_STRIDE
SEL_TILE = 512
NEG_INF = float("-inf")

_NT = (((1,), (1,)), ((), ()))


def _params(n_grid):
    return pltpu.CompilerParams(
        dimension_semantics=("arbitrary",) * n_grid, vmem_limit_bytes=VMEM_LIMIT_BYTES)


def _proj_kernel(*refs, rope):
    if rope:
        x_ref, w_ref, cos_ref, sa_ref, sb_ref, o_ref, xb_ref = refs
    else:
        x_ref, w_ref, o_ref, xb_ref = refs

    @pl.when(pl.program_id(1) == 0)
    def _():
        xb_ref[...] = x_ref[...].astype(BF16)

    acc = jnp.dot(xb_ref[...], w_ref[...], preferred_element_type=F32)
    if rope:
        c, sa, sb = cos_ref[...], sa_ref[...], sb_ref[...]
        for j in range(acc.shape[1] // LANES):
            t = acc[:, j * LANES:(j + 1) * LANES]
            r = t * c + pltpu.roll(t, LANES - HALF, 1) * sa + pltpu.roll(t, HALF, 1) * sb
            o_ref[:, j * LANES:(j + 1) * LANES] = r.astype(o_ref.dtype)
    else:
        o_ref[...] = acc.astype(o_ref.dtype)


def _pick_tile(n, candidates):
    for c in candidates:
        if n % c == 0:
            return c
    raise ValueError(f"no tile for {n}")


def _proj(x, w, seq_len, out_dtype, rope_tabs=None):
    t, k = x.shape
    n = w.shape[1]
    tm = _pick_tile(seq_len, (1024, 512, 256, 128))
    tn = _pick_tile(n, (512, 384, 256, 128))
    rope = rope_tabs is not None
    in_specs = [pl.BlockSpec((tm, k), lambda i, j: (i, 0)),
                pl.BlockSpec((k, tn), lambda i, j: (0, j))]
    args = [x, w]
    if rope:
        n_seq_tiles = seq_len // tm
        tab_spec = pl.BlockSpec((tm, LANES), lambda i, j: (i % n_seq_tiles, 0))
        in_specs += [tab_spec] * 3
        args += list(rope_tabs)
    return pl.pallas_call(
        functools.partial(_proj_kernel, rope=rope),
        grid=(t // tm, n // tn),
        in_specs=in_specs,
        out_specs=pl.BlockSpec((tm, tn), lambda i, j: (i, j)),
        out_shape=jax.ShapeDtypeStruct((t, n), out_dtype),
        scratch_shapes=[pltpu.VMEM((tm, k), BF16)],
        compiler_params=_params(2),
        name="proj_rope" if rope else "proj",
    )(*args)


def _band_kernel(*refs, tq, npv, max_dist, kv_pair_of, has_sinks, want_lse):
    it = iter(refs)
    q_ref, kc_ref, kp_ref, vc_ref, vp_ref = (next(it) for _ in range(5))
    sink_ref = next(it) if has_sinks else None
    o_ref = next(it)
    lse_ref = next(it) if want_lse else None
    qb = pl.program_id(2)
    w = (npv + 1) * QBLK
    lane = lax.broadcasted_iota(jnp.int32, (QBLK, LANES), 1)
    first_half = lane < HEAD_DIM
    qi = lax.broadcasted_iota(jnp.int32, (QBLK, w), 0)
    kj = lax.broadcasted_iota(jnp.int32, (QBLK, w), 1)
    dist = qi + npv * QBLK - kj
    band = (dist >= 0) & (dist <= max_dist)
    for sub in range(tq // QBLK):
        r0 = sub * QBLK
        ws = tq + r0 - npv * QBLK
        assert ws >= 0
        kstart = qb * tq + r0 - npv * QBLK
        mask = band & (kj + kstart >= 0)
        lse_tile = jnp.zeros((QBLK, LANES), F32)
        for pi in range(N_PAIRS):
            cl = slice(kv_pair_of(pi) * LANES, (kv_pair_of(pi) + 1) * LANES)
            k_parts, v_parts = [], []
            if ws < tq:
                k_parts.append(kp_ref[ws:tq, cl])
                v_parts.append(vp_ref[ws:tq, cl])
            cs = max(ws - tq, 0)
            k_parts.append(kc_ref[cs:r0 + QBLK, cl])
            v_parts.append(vc_ref[cs:r0 + QBLK, cl])
            kwin = k_parts[0] if len(k_parts) == 1 else jnp.concatenate(k_parts, axis=0)
            vwin = v_parts[0] if len(v_parts) == 1 else jnp.concatenate(v_parts, axis=0)
            qp = q_ref[r0:r0 + QBLK, pi * LANES:(pi + 1) * LANES]
            outs = []
            for e in range(2):
                qe = jnp.where(first_half if e == 0 else jnp.logical_not(first_half), qp, 0)
                s = lax.dot_general(qe, kwin, _NT, preferred_element_type=F32)
                s = jnp.where(mask, s, NEG_INF)
                m = jnp.max(s, axis=-1, keepdims=True)
                if has_sinks:
                    sk = sink_ref[2 * pi + e]
                    m = jnp.maximum(m, sk)
                p = jnp.exp(s - m)
                l = jnp.sum(p, axis=-1, keepdims=True)
                if has_sinks:
                    l = l + jnp.exp(sk - m)
                pv = jnp.dot(p.astype(BF16), vwin, preferred_element_type=F32)
                outs.append(pv / l)
                if want_lse:
                    lse_tile = jnp.where(lane == 2 * pi + e, m + jnp.log(l), lse_tile)
            o_ref[r0:r0 + QBLK, pi * LANES:(pi + 1) * LANES] = jnp.where(first_half, outs[0], outs[1])
        if want_lse:
            lse_ref[r0:r0 + QBLK, :] = lse_tile


def _banded_attention(q_arr, k_arr, v_arr, *, dil, tq, npv, max_dist, q_off, k_off, v_off,
                      kv_pairs, kv_pair_of, sinks=None, want_lse=False):
    b, l, _ = q_arr.shape
    qw, kw = N_PAIRS * LANES, kv_pairs * LANES
    nq = q_arr.shape[2] // dil // qw if dil > 1 else 0
    nk = k_arr.shape[2] // dil // kw if dil > 1 else 0
    nv = v_arr.shape[2] // dil // kw if dil > 1 else 0
    in_specs = [
        pl.BlockSpec((None, tq, qw), lambda bi, r, i: (bi, i, r * nq + q_off)),
        pl.BlockSpec((None, tq, kw), lambda bi, r, i: (bi, i, r * nk + k_off)),
        pl.BlockSpec((None, tq, kw), lambda bi, r, i: (bi, jnp.maximum(i - 1, 0), r * nk + k_off)),
        pl.BlockSpec((None, tq, kw), lambda bi, r, i: (bi, i, r * nv + v_off)),
        pl.BlockSpec((None, tq, kw), lambda bi, r, i: (bi, jnp.maximum(i - 1, 0), r * nv + v_off)),
    ]
    args = [q_arr, k_arr, k_arr, v_arr, v_arr]
    if sinks is not None:
        in_specs.append(pl.BlockSpec(memory_space=pltpu.SMEM))
        args.append(sinks)
    out_specs = [pl.BlockSpec((None, tq, qw), lambda bi, r, i: (bi, i, r))]
    out_shape = [jax.ShapeDtypeStruct((b, l, dil * qw), F32)]
    if want_lse:
        out_specs.append(pl.BlockSpec((None, tq, LANES), lambda bi, r, i: (bi, i, r)))
        out_shape.append(jax.ShapeDtypeStruct((b, l, dil * LANES), F32))
    res = pl.pallas_call(
        functools.partial(_band_kernel, tq=tq, npv=npv, max_dist=max_dist, kv_pair_of=kv_pair_of,
                          has_sinks=sinks is not None, want_lse=want_lse),
        grid=(b, dil, l // tq),
        in_specs=in_specs,
        out_specs=out_specs,
        out_shape=out_shape,
        compiler_params=_params(3),
        name=f"band_d{dil}_w{max_dist}",
    )(*args)
    return res if want_lse else res[0]


def _compress_kernel(c_ref, pos_ref, w_ref, o_ref):
    c = c_ref[...]
    top = jnp.dot((c + pos_ref[0:1, :]).astype(BF16), w_ref[0], preferred_element_type=F32)
    bot = jnp.dot((c + pos_ref[1:2, :]).astype(BF16), w_ref[1], preferred_element_type=F32)
    nc = c.shape[0]
    o_ref[...] = (top + pltpu.roll(bot, nc - 1, 0)).astype(o_ref.dtype)


def _compress(chunks, pos, w):
    b, hk, nc, cw = chunks.shape
    return pl.pallas_call(
        _compress_kernel,
        grid=(b, hk),
        in_specs=[pl.BlockSpec((None, None, nc, cw), lambda bi, h: (bi, h, 0, 0)),
                  pl.BlockSpec((2, cw), lambda bi, h: (0, 0)),
                  pl.BlockSpec((2, cw, HEAD_DIM), lambda bi, h: (0, 0, 0))],
        out_specs=pl.BlockSpec((None, None, nc, HEAD_DIM), lambda bi, h: (bi, h, 0, 0)),
        out_shape=jax.ShapeDtypeStruct((b, hk, nc, HEAD_DIM), BF16),
        compiler_params=_params(2),
        name="nsa_compress",
    )(chunks, pos, w)


def _stack_group_queries(q_ref, mp, half):
    pairs = [C_GROUP * mp + i for i in range(C_GROUP)]
    return jnp.concatenate(
        [jnp.where(half, q_ref[:, pr * LANES:(pr + 1) * LANES], 0) for pr in pairs], axis=0)


def _store_group_heads(o_ref, val, mp, e, first_half):
    for i in range(C_GROUP):
        ol = slice((C_GROUP * mp + i) * LANES, (C_GROUP * mp + i + 1) * LANES)
        rows = slice(i * QBLK, (i + 1) * QBLK)
        if e == 0:
            o_ref[:, ol] = val[rows]
        else:
            o_ref[:, ol] = jnp.where(first_half, o_ref[:, ol], val[rows])


def _nsa_cmp_kernel(q_ref, kc_ref, vc_ref, ocmp_ref, sel_ref, impt_ref, *, n_sel):
    nc = kc_ref.shape[0]
    ns = nc // SEL_PER_CMP
    t0 = pl.program_id(1) * QBLK
    lane = lax.broadcasted_iota(jnp.int32, (QBLK, LANES), 1)
    first_half = lane < HEAD_DIM

    qi_c = lax.broadcasted_iota(jnp.int32, (QBLK, nc), 0)
    nn_c = lax.broadcasted_iota(jnp.int32, (QBLK, nc), 1)
    cvalid = nn_c * C_CMP_STRIDE + (C_CMP_LEN - 1) <= t0 + qi_c
    cvalid = jnp.concatenate([cvalid] * C_GROUP, axis=0)

    jj = lax.broadcasted_iota(jnp.int32, (ns, QBLK), 0)
    cur = (t0 + lax.broadcasted_iota(jnp.int32, (ns, QBLK), 1)) // C_SEL_LEN
    forced = (jj == 0) | (jj == cur) | (jj == cur - 1)
    bvalid = jj <= cur

    impt_ref[0:8, :] = jnp.zeros((8, QBLK), F32)

    for kh in range(C_KV_HEADS):
        mp, e = divmod(kh, 2)
        cl = slice(mp * LANES, (mp + 1) * LANES)
        half = first_half if e == 0 else jnp.logical_not(first_half)
        qst = _stack_group_queries(q_ref, mp, half)

        sc = lax.dot_general(qst, kc_ref[:, cl], _NT, preferred_element_type=F32)
        sc = jnp.where(cvalid, sc, NEG_INF)
        mx = jnp.max(sc, axis=-1, keepdims=True)
        mx = jnp.where(mx > NEG_INF, mx, 0.0)
        ee = jnp.exp(sc - mx)
        pc = ee / jnp.maximum(jnp.sum(ee, axis=-1, keepdims=True), 1e-30)
        ocmp = jnp.dot(pc.astype(BF16), vc_ref[:, cl], preferred_element_type=F32)
        _store_group_heads(ocmp_ref, ocmp, mp, e, first_half)

        imp = pc[0:QBLK]
        for g in range(1, C_GROUP):
            imp = imp + pc[g * QBLK:(g + 1) * QBLK]
        for c in range(nc // QBLK):
            impt_ref[8 + c * QBLK:8 + (c + 1) * QBLK, :] = imp[:, c * QBLK:(c + 1) * QBLK].T
        imp_s = C_SEL_OVERLAP[0] * impt_ref[pl.ds(7, ns, stride=SEL_PER_CMP), :]
        for o_off in range(1, len(C_SEL_OVERLAP)):
            imp_s = imp_s + C_SEL_OVERLAP[o_off] * impt_ref[pl.ds(7 + o_off, ns, stride=SEL_PER_CMP), :]
        score = jnp.where(forced, 1e4, jnp.where(bvalid, imp_s, -1.0))
        selt = jnp.zeros((ns, QBLK), F32)
        for _ in range(n_sel):
            best = jnp.max(score, axis=0, keepdims=True)
            first = jnp.min(jnp.where(score == best, jj, ns), axis=0, keepdims=True)
            hit = jj == first
            selt = jnp.where(hit, 1.0, selt)
            score = jnp.where(hit, NEG_INF, score)
        if ns < LANES:
            selt = jnp.concatenate([selt, jnp.zeros((LANES - ns, QBLK), F32)], axis=0)
        sel_ref[:, kh * LANES:(kh + 1) * LANES] = selt.T.astype(BF16)


def _nsa_slc_kernel(q_ref, sel_ref, ks_ref, vs_ref, oslc_ref, m_ref, l_ref, acc_ref):
    gq = C_GROUP * QBLK
    t0 = pl.program_id(1) * QBLK
    lane = lax.broadcasted_iota(jnp.int32, (QBLK, LANES), 1)
    first_half = lane < HEAD_DIM
    je = lax.broadcasted_iota(jnp.int32, (LANES, SEL_TILE), 0)
    ce = lax.broadcasted_iota(jnp.int32, (LANES, SEL_TILE), 1) // C_SEL_LEN
    qi_s = lax.broadcasted_iota(jnp.int32, (QBLK, SEL_TILE), 0)
    kk_s = lax.broadcasted_iota(jnp.int32, (QBLK, SEL_TILE), 1)
    n_tiles = (t0 + QBLK - 1) // SEL_TILE + 1

    for kh in range(C_KV_HEADS):
        mp, e = divmod(kh, 2)
        cl = slice(mp * LANES, (mp + 1) * LANES)
        half = first_half if e == 0 else jnp.logical_not(first_half)
        qst = _stack_group_queries(q_ref, mp, half)
        selq = sel_ref[:, kh * LANES:(kh + 1) * LANES]

        m_ref[...] = jnp.full((gq, 1), NEG_INF, F32)
        l_ref[...] = jnp.zeros((gq, 1), F32)
        acc_ref[...] = jnp.zeros((gq, LANES), F32)

        def tile_step(kt, carry, qst=qst, selq=selq, cl=cl):
            k0 = pl.multiple_of(kt * SEL_TILE, SEL_TILE)
            expand = jnp.where(je == kt * (SEL_TILE // C_SEL_LEN) + ce, 1.0, 0.0).astype(BF16)
            picked = jnp.dot(selq, expand, preferred_element_type=F32)
            ok = (picked > 0.5) & (k0 + kk_s <= t0 + qi_s)
            ok = jnp.concatenate([ok] * C_GROUP, axis=0)
            ss = lax.dot_general(qst, ks_ref[pl.ds(k0, SEL_TILE), cl], _NT, preferred_element_type=F32)
            ss = jnp.where(ok, ss, NEG_INF)
            m_old = m_ref[...]
            m_new = jnp.maximum(m_old, jnp.max(ss, axis=-1, keepdims=True))
            alpha = jnp.exp(m_old - m_new)
            p = jnp.exp(ss - m_new)
            l_ref[...] = alpha * l_ref[...] + jnp.sum(p, axis=-1, keepdims=True)
            acc_ref[...] = alpha * acc_ref[...] + jnp.dot(
                p.astype(BF16), vs_ref[pl.ds(k0, SEL_TILE), cl], preferred_element_type=F32)
            m_ref[...] = m_new
            return carry

        lax.fori_loop(0, n_tiles, tile_step, 0)
        _store_group_heads(oslc_ref, acc_ref[...] / l_ref[...], mp, e, first_half)


def _nsa_select(q_arr, kcmp, vcmp, ks_arr, vs_arr, *, ks_off, vs_off):
    b, s, _ = q_arr.shape
    nc = kcmp.shape[1]
    ns = nc // SEL_PER_CMP
    assert ns <= LANES
    qw = N_PAIRS * LANES
    sw = C_KV_HEADS * LANES
    gq = C_GROUP * QBLK
    q_spec = pl.BlockSpec((None, QBLK, qw), lambda bi, i: (bi, i, 0))
    o_spec = pl.BlockSpec((None, QBLK, qw), lambda bi, i: (bi, i, 0))
    sel_spec = pl.BlockSpec((None, QBLK, sw), lambda bi, i: (bi, i, 0))
    o_shape = jax.ShapeDtypeStruct((b, s, qw), F32)
    o_cmp, sel = pl.pallas_call(
        functools.partial(_nsa_cmp_kernel, n_sel=min(C_N_SEL, ns)),
        grid=(b, s // QBLK),
        in_specs=[q_spec,
                  pl.BlockSpec((None, nc, C_KV), lambda bi, i: (bi, 0, 0)),
                  pl.BlockSpec((None, nc, C_KV), lambda bi, i: (bi, 0, 0))],
        out_specs=[o_spec, sel_spec],
        out_shape=[o_shape, jax.ShapeDtypeStruct((b, s, sw), BF16)],
        scratch_shapes=[pltpu.VMEM((8 + nc, QBLK), F32)],
        compiler_params=_params(2),
        name="nsa_compressed",
    )(q_arr, kcmp, vcmp)
    o_slc = pl.pallas_call(
        _nsa_slc_kernel,
        grid=(b, s // QBLK),
        in_specs=[q_spec, sel_spec,
                  pl.BlockSpec((None, s, C_KV), lambda bi, i: (bi, 0, ks_off)),
                  pl.BlockSpec((None, s, C_KV), lambda bi, i: (bi, 0, vs_off))],
        out_specs=o_spec,
        out_shape=o_shape,
        scratch_shapes=[pltpu.VMEM((gq, 1), F32),
                        pltpu.VMEM((gq, 1), F32),
                        pltpu.VMEM((gq, LANES), F32)],
        compiler_params=_params(2),
        name="nsa_selected",
    )(q_arr, sel, ks_arr, vs_arr)
    return o_cmp, o_slc


def _head_cols(tile, col, first_half):
    tm = tile.shape[0]
    a = jnp.broadcast_to(tile[:, col:col + 1], (tm, LANES))
    b = jnp.broadcast_to(tile[:, col + 1:col + 2], (tm, LANES))
    return jnp.where(first_half, a, b)


def _post_kernel(*refs, kind):
    it = iter(refs)
    n_branch = {"A": 3, "B": 1, "C": 3}[kind]
    o_refs = [next(it) for _ in range(n_branch)]
    if kind == "A":
        aux_refs = [next(it) for _ in range(3)]
    elif kind == "C":
        aux_refs = [next(it)]
    z_ref, x_ref, p_ref, wo_ref, g_ref, b_ref, wg_ref, wp_ref, out_ref, u_ref = (next(it) for _ in range(10))
    tm = x_ref.shape[0]
    lane = lax.broadcasted_iota(jnp.int32, (tm, LANES), 1)
    first_half = lane < HEAD_DIM

    if kind == "A":
        lses = [r[...] for r in aux_refs]
        mx = jnp.maximum(jnp.maximum(lses[0], lses[1]), lses[2])
        ws = [jnp.exp(v - mx) for v in lses]
        den = ws[0] + ws[1] + ws[2]
        ws = [v / den for v in ws]
    elif kind == "C":
        gates = jax.nn.sigmoid(aux_refs[0][...])

    for pi in range(N_PAIRS):
        cl = slice(pi * LANES, (pi + 1) * LANES)
        if kind == "A":
            o = _head_cols(ws[0], 2 * pi, first_half) * o_refs[0][:, cl]
            for g in range(1, 3):
                o = o + _head_cols(ws[g], 2 * pi, first_half) * o_refs[g][:, cl]
        elif kind == "B":
            o = o_refs[0][:, cl]
        else:
            o = _head_cols(gates, 2 * pi, first_half) * o_refs[0][:, cl]
            for c in range(1, 3):
                o = o + _head_cols(gates, c * N_HEADS + 2 * pi, first_half) * o_refs[c][:, cl]
        z = z_ref[:, cl]
        u_ref[:, cl] = (o * (z * jax.nn.sigmoid(z))).astype(BF16)

    h = jnp.dot(u_ref[...], wo_ref[...], preferred_element_type=F32)
    y = DEEPNORM_ALPHA * x_ref[...] + h
    mu = jnp.mean(y, axis=-1, keepdims=True)
    yc = y - mu
    var = jnp.mean(yc * yc, axis=-1, keepdims=True)
    yn = yc * lax.rsqrt(var + LN_EPS) * g_ref[...] + b_ref[...]
    gate = jax.nn.sigmoid(jnp.dot(yn.astype(BF16), wg_ref[...], preferred_element_type=F32))
    pp = jnp.dot(p_ref[...].astype(BF16), wp_ref[...], preferred_element_type=F32)
    out_ref[...] = yn + gate * pp


def _post(kind, o_list, aux_list, z, x, p, w_out, ln_g, ln_b, w_gate, w_proj):
    t, d = x.shape
    tm = 256
    row = lambda w: pl.BlockSpec((tm, w), lambda i: (i, 0))
    full = lambda a: pl.BlockSpec(a.shape, lambda i: (0,) * a.ndim)
    args = list(o_list) + list(aux_list) + [z, x, p, w_out, ln_g, ln_b, w_gate, w_proj]
    in_specs = ([row(d)] * len(o_list) + [row(LANES)] * len(aux_list)
                + [row(d), row(d), row(p.shape[1]), full(w_out), full(ln_g), full(ln_b), full(w_gate), full(w_proj)])
    return pl.pallas_call(
        functools.partial(_post_kernel, kind=kind),
        grid=(t // tm,),
        in_specs=in_specs,
        out_specs=row(d),
        out_shape=jax.ShapeDtypeStruct((t, d), F32),
        scratch_shapes=[pltpu.VMEM((tm, d), BF16)],
        compiler_params=_params(1),
        name=f"post_{kind}",
    )(*args)


def _rope_tables(seq_len):
    inv = 1.0 / (ROPE_THETA ** (jnp.arange(0, HEAD_DIM, 2, dtype=F32) / HEAD_DIM))
    ang = jnp.arange(seq_len, dtype=F32)[:, None] * inv[None, :]
    cos, sin = jnp.cos(ang), jnp.sin(ang)
    zero = jnp.zeros_like(sin)
    cos_t = jnp.concatenate([cos] * 4, axis=1)
    sa_t = jnp.concatenate([-sin, zero, -sin, zero], axis=1)
    sb_t = jnp.concatenate([zero, sin, zero, sin], axis=1)
    return cos_t, sa_t, sb_t


def _head_cols_index(head_order):
    return np.concatenate([np.arange(h * HEAD_DIM, (h + 1) * HEAD_DIM) for h in head_order])


_B_HEAD_ORDER = [e * (N_HEADS // B_KV_HEADS) + i for i in range(N_PAIRS) for e in range(2)]
_C_HEAD_ORDER = [C_GROUP * (2 * m + e) + i for m in range(C_KV_HEADS // 2) for i in range(C_GROUP) for e in range(2)]


def _mixer_a(xt, b, s, w_in, tabs):
    wd = N_HEADS * HEAD_DIM
    scale = HEAD_DIM ** -0.5
    qk_cols, v_cols = [], []
    for gi in range(len(A_GROUPS)):
        base = 3 * gi * wd
        qk_cols += [w_in[:, base:base + wd] * scale, w_in[:, base + wd:base + 2 * wd]]
        v_cols.append(w_in[:, base + 2 * wd:base + 3 * wd])
    qk = _proj(xt, jnp.concatenate(qk_cols, axis=1).astype(BF16), s, BF16, tabs)
    v = _proj(xt, jnp.concatenate(v_cols, axis=1).astype(BF16), s, BF16)
    z = _proj(xt, w_in[:, -wd:].astype(BF16), s, F32)
    outs, lses = [], []
    for gi, (window, dil) in enumerate(A_GROUPS):
        ls = s // dil
        o, lse = _banded_attention(
            qk.reshape(b, ls, -1), qk.reshape(b, ls, -1), v.reshape(b, ls, -1),
            dil=dil, tq=min(256, ls), npv=1, max_dist=window // dil,
            q_off=2 * gi, k_off=2 * gi + 1, v_off=gi,
            kv_pairs=N_PAIRS, kv_pair_of=lambda pi: pi, want_lse=True)
        outs.append(o.reshape(b * s, wd))
        lses.append(lse.reshape(b * s, LANES))
    return outs, lses, z


def _mixer_b(xt, b, s, w_in, sinks, tabs):
    wd = N_HEADS * HEAD_DIM
    kvw = B_KV_HEADS * HEAD_DIM
    perm = _head_cols_index(_B_HEAD_ORDER)
    wq = w_in[:, :wd][:, perm] * HEAD_DIM ** -0.5
    wk = w_in[:, wd:wd + kvw]
    wv = w_in[:, wd + kvw:wd + 2 * kvw]
    wz = w_in[:, wd + 2 * kvw:][:, perm]
    qk = _proj(xt, jnp.concatenate([wq, wk], axis=1).astype(BF16), s, BF16, tabs)
    v = _proj(xt, wv.astype(BF16), s, BF16)
    z = _proj(xt, wz.astype(BF16), s, F32)
    o = _banded_attention(
        qk.reshape(b, s, -1), qk.reshape(b, s, -1), v.reshape(b, s, -1),
        dil=1, tq=min(256, s), npv=1, max_dist=B_WINDOW - 1,
        q_off=0, k_off=wd // kvw, v_off=0, kv_pairs=1, kv_pair_of=lambda pi: 0,
        sinks=sinks[np.asarray(_B_HEAD_ORDER)].astype(F32))
    return [o.reshape(b * s, wd)], [], z, perm


def _mixer_c(xt, b, s, w_in, w_ck, w_cv, pos, tabs):
    wd = N_HEADS * HEAD_DIM
    perm = _head_cols_index(_C_HEAD_ORDER)
    cols = np.cumsum([0, wd] + [C_KV] * 6 + [3 * N_HEADS, wd])
    part = lambda i: w_in[:, cols[i]:cols[i + 1]]
    wq = part(0)[:, perm] * HEAD_DIM ** -0.5
    w_gl = part(7).reshape(-1, 3, N_HEADS)[:, :, np.asarray(_C_HEAD_ORDER)].reshape(-1, 3 * N_HEADS)
    w_gl = jnp.pad(w_gl, ((0, 0), (0, LANES - 3 * N_HEADS)))
    qkk = _proj(xt, jnp.concatenate([wq, part(3), part(5)], axis=1).astype(BF16), s, BF16, tabs)
    kc = _proj(xt, part(1).astype(BF16), s, F32, tabs)
    vv = _proj(xt, jnp.concatenate([part(4), part(6)], axis=1).astype(BF16), s, BF16)
    vc = _proj(xt, part(2).astype(BF16), s, F32)
    gl = _proj(xt, w_gl.astype(BF16), s, F32)
    z = _proj(xt, part(8)[:, perm].astype(BF16), s, F32)

    nc = s // C_CMP_STRIDE
    cw = C_CMP_STRIDE * HEAD_DIM

    def chunks(t):
        return jnp.transpose(t.reshape(b, s, C_KV_HEADS, HEAD_DIM), (0, 2, 1, 3)).reshape(b, C_KV_HEADS, nc, cw)

    pos2 = pos.reshape(2, cw)

    def compressed(t, w):
        c = _compress(chunks(t), pos2, w.reshape(2, cw, HEAD_DIM).astype(BF16))
        return jnp.transpose(c, (0, 2, 1, 3)).reshape(b, nc, C_KV)

    kcmp, vcmp = compressed(kc, w_ck), compressed(vc, w_cv)
    qkk3, vv3 = qkk.reshape(b, s, -1), vv.reshape(b, s, -1)
    o_cmp, o_slc = _nsa_select(qkk3, kcmp, vcmp, qkk3, vv3, ks_off=wd // C_KV, vs_off=0)
    o_win = _banded_attention(
        qkk3, qkk3, vv3, dil=1, tq=C_WINDOW, npv=C_WINDOW // QBLK, max_dist=C_WINDOW - 1,
        q_off=0, k_off=wd // C_KV + 1, v_off=1, kv_pairs=C_KV_HEADS // 2,
        kv_pair_of=lambda pi: pi // C_GROUP)
    outs = [o.reshape(b * s, wd) for o in (o_cmp, o_slc, o_win)]
    return outs, [gl], z, perm


def kernel(x, p, a_w_in, a_w_out, b_w_in, b_sinks, b_w_out, c_w_in, c_w_ck, c_w_cv, c_pos, c_w_out,
           ln_g, ln_b, ple_w_proj, ple_w_gate):
    b, s, d = x.shape
    assert d == N_HEADS * HEAD_DIM and s % (QBLK * A_GROUPS[-1][1]) == 0 and s % C_WINDOW == 0
    tabs = _rope_tables(s)
    xt = x.reshape(b * s, d)
    for i in range(DEPTH):
        j, kind = divmod(i, N_MIXERS)
        if kind == 0:
            outs, aux, z = _mixer_a(xt, b, s, a_w_in[j], tabs)
            w_out, name = a_w_out[j], "A"
        elif kind == 1:
            outs, aux, z, perm = _mixer_b(xt, b, s, b_w_in[j], b_sinks[j], tabs)
            w_out, name = b_w_out[j][perm, :], "B"
        else:
            outs, aux, z, perm = _mixer_c(xt, b, s, c_w_in[j], c_w_ck[j], c_w_cv[j], c_pos[j], tabs)
            w_out, name = c_w_out[j][perm, :], "C"
        xt = _post(name, outs, aux, z, xt, p[i].reshape(b * s, -1), w_out.astype(BF16),
                   ln_g[i].reshape(1, d), ln_b[i].reshape(1, d),
                   ple_w_gate[i].astype(BF16), ple_w_proj[i].astype(BF16))
    return xt.reshape(b, s, d)
```

```python
import functools

import numpy as np
import jax
import jax.numpy as jnp
from jax import lax
from jax.experimental import pallas as pl
from jax.experimental.pallas import tpu as pltpu

F32 = jnp.float32
BF16 = jnp.bfloat16

LANES = 128
VMEM_LIMIT_BYTES = 56 * 1024 * 1024

HEAD_DIM = 64
HALF = HEAD_DIM // 2
PAIR = 2 * HEAD_DIM
assert PAIR == LANES
N_HEADS = 16
N_PAIRS = N_HEADS // 2
ROPE_THETA = 10000.0
QBLK = 128
LN_EPS = 1e-5
DEPTH = 4
N_MIXERS = 3
DEEPNORM_ALPHA = (2 * DEPTH) ** 0.25
A_GROUPS = ((128, 1), (512, 4), (2048, 16))
B_KV_HEADS = 2
B_WINDOW = 128
C_KV_HEADS = 4
C_GROUP = N_HEADS // C_KV_HEADS
C_KV = C_KV_HEADS * HEAD_DIM
C_CMP_STRIDE = 16
C_CMP_LEN = 32
C_SEL_LEN = 64
C_N_SEL = 16
C_WINDOW = 512
C_SEL_OVERLAP = (1.0, 2.0, 2.0, 2.0, 1.0)
SEL_PER_CMP = C_SEL_LEN // C_CMP_STRIDE
SEL_TILE = 512
NEG_INF = float("-inf")

_NT = (((1,), (1,)), ((), ()))


def _params(n_grid):
    return pltpu.CompilerParams(
        dimension_semantics=("arbitrary",) * n_grid, vmem_limit_bytes=VMEM_LIMIT_BYTES)


def _proj_kernel(*refs, n_rope_tiles, n_tiles, dil):
    it = iter(refs)
    x_ref, w_ref = next(it), next(it)
    if n_rope_tiles:
        cos_ref, sa_ref, sb_ref = next(it), next(it), next(it)
    o_ref, xb_ref = next(it), next(it)
    stage_ref = next(it) if dil > 1 else None
    tm, tn = xb_ref.shape[0], w_ref.shape[1]

    @pl.when(pl.program_id(1) == 0)
    def _():
        xb_ref[...] = x_ref[...].astype(BF16)

    def emit(rope):
        acc = jnp.dot(xb_ref[...], w_ref[...], preferred_element_type=F32)
        if rope:
            c, sa, sb = cos_ref[...], sa_ref[...], sb_ref[...]
        for j in range(tn // LANES):
            cl = slice(j * LANES, (j + 1) * LANES)
            t = acc[:, cl]
            if rope:
                t = t * c + pltpu.roll(t, LANES - HALF, 1) * sa + pltpu.roll(t, HALF, 1) * sb
            if dil == 1:
                o_ref[0, :, cl] = t.astype(o_ref.dtype)
            else:
                stage_ref[j] = t
                for r in range(dil):
                    o_ref[r, :, cl] = stage_ref[j, pl.ds(r, tm // dil, stride=dil), :].astype(o_ref.dtype)

    if n_rope_tiles == 0 or n_rope_tiles == n_tiles:
        emit(n_rope_tiles > 0)
    else:
        pl.when(pl.program_id(1) < n_rope_tiles)(lambda: emit(True))
        pl.when(pl.program_id(1) >= n_rope_tiles)(lambda: emit(False))


def _pick_tile(n, candidates):
    for c in candidates:
        if n % c == 0:
            return c
    raise ValueError(f"no tile for {n}")


def _proj(x, w, batch, out_dtype, rope_tabs=None, n_rope_cols=0, dil=1, tn=None):
    t, k = x.shape
    n = w.shape[1]
    seq_len = t // batch
    tm = _pick_tile(seq_len, (1024, 512, 256, 128))
    tn = tn or _pick_tile(n, (512, 384, 256, 128))
    assert n % tn == 0 and n_rope_cols % tn == 0 and tm % (8 * dil) == 0
    n_seq_tiles = seq_len // tm
    in_specs = [pl.BlockSpec((tm, k), lambda i, j: (i, 0)),
                pl.BlockSpec((k, tn), lambda i, j: (0, j))]
    args = [x, w]
    if n_rope_cols:
        tab_spec = pl.BlockSpec((tm, LANES), lambda i, j: (i % n_seq_tiles, 0))
        in_specs += [tab_spec] * 3
        args += list(rope_tabs)
    scratch = [pltpu.VMEM((tm, k), BF16)]
    if dil > 1:
        scratch.append(pltpu.VMEM((tn // LANES, tm, LANES), F32))
    return pl.pallas_call(
        functools.partial(_proj_kernel, n_rope_tiles=n_rope_cols // tn, n_tiles=n // tn, dil=dil),
        grid=(t // tm, n // tn),
        in_specs=in_specs,
        out_specs=pl.BlockSpec((None, dil, tm // dil, tn),
                               lambda i, j: (i // n_seq_tiles, 0, i % n_seq_tiles, j)),
        out_shape=jax.ShapeDtypeStruct((batch, dil, seq_len // dil, n), out_dtype),
        scratch_shapes=scratch,
        compiler_params=_params(2),
        name=f"proj_d{dil}_r{n_rope_cols}",
    )(*args)


def _band_kernel(*refs, tq, npv, max_dist, kv_pair_of, has_sinks, want_lse):
    it = iter(refs)
    q_ref, kc_ref, kp_ref, vc_ref, vp_ref = (next(it) for _ in range(5))
    sink_ref = next(it) if has_sinks else None
    o_ref = next(it)
    lse_ref = next(it) if want_lse else None
    qb = pl.program_id(2)
    w = (npv + 1) * QBLK
    lane = lax.broadcasted_iota(jnp.int32, (QBLK, LANES), 1)
    first_half = lane < HEAD_DIM
    qi = lax.broadcasted_iota(jnp.int32, (QBLK, w), 0)
    kj = lax.broadcasted_iota(jnp.int32, (QBLK, w), 1)
    dist = qi + npv * QBLK - kj
    band = (dist >= 0) & (dist <= max_dist)
    for sub in range(tq // QBLK):
        r0 = sub * QBLK
        ws = tq + r0 - npv * QBLK
        assert ws >= 0
        kstart = qb * tq + r0 - npv * QBLK
        mask = band & (kj + kstart >= 0)
        lse_tile = jnp.zeros((QBLK, LANES), F32)
        for pi in range(N_PAIRS):
            cl = slice(kv_pair_of(pi) * LANES, (kv_pair_of(pi) + 1) * LANES)
            k_parts, v_parts = [], []
            if ws < tq:
                k_parts.append(kp_ref[ws:tq, cl])
                v_parts.append(vp_ref[ws:tq, cl])
            cs = max(ws - tq, 0)
            k_parts.append(kc_ref[cs:r0 + QBLK, cl])
            v_parts.append(vc_ref[cs:r0 + QBLK, cl])
            kwin = k_parts[0] if len(k_parts) == 1 else jnp.concatenate(k_parts, axis=0)
            vwin = v_parts[0] if len(v_parts) == 1 else jnp.concatenate(v_parts, axis=0)
            qp = q_ref[r0:r0 + QBLK, pi * LANES:(pi + 1) * LANES]
            outs = []
            for e in range(2):
                qe = jnp.where(first_half if e == 0 else jnp.logical_not(first_half), qp, 0)
                s = lax.dot_general(qe, kwin, _NT, preferred_element_type=F32)
                s = jnp.where(mask, s, NEG_INF)
                m = jnp.max(s, axis=-1, keepdims=True)
                if has_sinks:
                    sk = sink_ref[2 * pi + e]
                    m = jnp.maximum(m, sk)
                p = jnp.exp(s - m)
                l = jnp.sum(p, axis=-1, keepdims=True)
                if has_sinks:
                    l = l + jnp.exp(sk - m)
                pv = jnp.dot(p.astype(BF16), vwin, preferred_element_type=F32)
                outs.append(pv / l)
                if want_lse:
                    lse_tile = jnp.where(lane == 2 * pi + e, m + jnp.log(l), lse_tile)
            o_ref[r0:r0 + QBLK, pi * LANES:(pi + 1) * LANES] = jnp.where(first_half, outs[0], outs[1])
        if want_lse:
            lse_ref[r0:r0 + QBLK, :] = lse_tile


def _banded_attention(q_arr, k_arr, v_arr, *, tq, npv, max_dist, q_off, k_off, v_off,
                      kv_pairs, kv_pair_of, sinks=None, want_lse=False):
    b, dil, l, _ = q_arr.shape
    qw, kw = N_PAIRS * LANES, kv_pairs * LANES
    in_specs = [
        pl.BlockSpec((None, None, tq, qw), lambda bi, r, i: (bi, r, i, q_off)),
        pl.BlockSpec((None, None, tq, kw), lambda bi, r, i: (bi, r, i, k_off)),
        pl.BlockSpec((None, None, tq, kw), lambda bi, r, i: (bi, r, jnp.maximum(i - 1, 0), k_off)),
        pl.BlockSpec((None, None, tq, kw), lambda bi, r, i: (bi, r, i, v_off)),
        pl.BlockSpec((None, None, tq, kw), lambda bi, r, i: (bi, r, jnp.maximum(i - 1, 0), v_off)),
    ]
    args = [q_arr, k_arr, k_arr, v_arr, v_arr]
    if sinks is not None:
        in_specs.append(pl.BlockSpec(memory_space=pltpu.SMEM))
        args.append(sinks)
    out_specs = [pl.BlockSpec((None, None, tq, qw), lambda bi, r, i: (bi, r, i, 0))]
    out_shape = [jax.ShapeDtypeStruct((b, dil, l, qw), F32)]
    if want_lse:
        out_specs.append(pl.BlockSpec((None, None, tq, LANES), lambda bi, r, i: (bi, r, i, 0)))
        out_shape.append(jax.ShapeDtypeStruct((b, dil, l, LANES), F32))
    res = pl.pallas_call(
        functools.partial(_band_kernel, tq=tq, npv=npv, max_dist=max_dist, kv_pair_of=kv_pair_of,
                          has_sinks=sinks is not None, want_lse=want_lse),
        grid=(b, dil, l // tq),
        in_specs=in_specs,
        out_specs=out_specs,
        out_shape=out_shape,
        compiler_params=_params(3),
        name=f"band_d{dil}_w{max_dist}",
    )(*args)
    return res if want_lse else res[0]


def _compress_kernel(c_ref, pos_ref, w_ref, o_ref):
    c = c_ref[...]
    top = jnp.dot((c + pos_ref[0:1, :]).astype(BF16), w_ref[0], preferred_element_type=F32)
    bot = jnp.dot((c + pos_ref[1:2, :]).astype(BF16), w_ref[1], preferred_element_type=F32)
    nc = c.shape[0]
    o_ref[...] = (top + pltpu.roll(bot, nc - 1, 0)).astype(o_ref.dtype)


def _compress(chunks, pos, w):
    b, hk, nc, cw = chunks.shape
    return pl.pallas_call(
        _compress_kernel,
        grid=(b, hk),
        in_specs=[pl.BlockSpec((None, None, nc, cw), lambda bi, h: (bi, h, 0, 0)),
                  pl.BlockSpec((2, cw), lambda bi, h: (0, 0)),
                  pl.BlockSpec((2, cw, HEAD_DIM), lambda bi, h: (0, 0, 0))],
        out_specs=pl.BlockSpec((None, None, nc, HEAD_DIM), lambda bi, h: (bi, h, 0, 0)),
        out_shape=jax.ShapeDtypeStruct((b, hk, nc, HEAD_DIM), BF16),
        compiler_params=_params(2),
        name="nsa_compress",
    )(chunks, pos, w)


def _stack_group_queries(q_ref, mp, half):
    pairs = [C_GROUP * mp + i for i in range(C_GROUP)]
    return jnp.concatenate(
        [jnp.where(half, q_ref[:, pr * LANES:(pr + 1) * LANES], 0) for pr in pairs], axis=0)


def _store_group_heads(o_ref, val, mp, e, first_half):
    for i in range(C_GROUP):
        ol = slice((C_GROUP * mp + i) * LANES, (C_GROUP * mp + i + 1) * LANES)
        rows = slice(i * QBLK, (i + 1) * QBLK)
        if e == 0:
            o_ref[:, ol] = val[rows]
        else:
            o_ref[:, ol] = jnp.where(first_half, o_ref[:, ol], val[rows])


def _nsa_cmp_kernel(q_ref, kc_ref, vc_ref, ocmp_ref, sel_ref, impt_ref, *, n_sel):
    nc = kc_ref.shape[0]
    ns = nc // SEL_PER_CMP
    t0 = pl.program_id(1) * QBLK
    lane = lax.broadcasted_iota(jnp.int32, (QBLK, LANES), 1)
    first_half = lane < HEAD_DIM

    qi_c = lax.broadcasted_iota(jnp.int32, (QBLK, nc), 0)
    nn_c = lax.broadcasted_iota(jnp.int32, (QBLK, nc), 1)
    cvalid = nn_c * C_CMP_STRIDE + (C_CMP_LEN - 1) <= t0 + qi_c
    cvalid = jnp.concatenate([cvalid] * C_GROUP, axis=0)

    jj = lax.broadcasted_iota(jnp.int32, (ns, QBLK), 0)
    cur = (t0 + lax.broadcasted_iota(jnp.int32, (ns, QBLK), 1)) // C_SEL_LEN
    forced = (jj == 0) | (jj == cur) | (jj == cur - 1)
    bvalid = jj <= cur

    impt_ref[0:8, :] = jnp.zeros((8, QBLK), F32)

    for kh in range(C_KV_HEADS):
        mp, e = divmod(kh, 2)
        cl = slice(mp * LANES, (mp + 1) * LANES)
        half = first_half if e == 0 else jnp.logical_not(first_half)
        qst = _stack_group_queries(q_ref, mp, half)

        sc = lax.dot_general(qst, kc_ref[:, cl], _NT, preferred_element_type=F32)
        sc = jnp.where(cvalid, sc, NEG_INF)
        mx = jnp.max(sc, axis=-1, keepdims=True)
        mx = jnp.where(mx > NEG_INF, mx, 0.0)
        ee = jnp.exp(sc - mx)
        pc = ee / jnp.maximum(jnp.sum(ee, axis=-1, keepdims=True), 1e-30)
        ocmp = jnp.dot(pc.astype(BF16), vc_ref[:, cl], preferred_element_type=F32)
        _store_group_heads(ocmp_ref, ocmp, mp, e, first_half)

        imp = pc[0:QBLK]
        for g in range(1, C_GROUP):
            imp = imp + pc[g * QBLK:(g + 1) * QBLK]
        for c in range(nc // QBLK):
            impt_ref[8 + c * QBLK:8 + (c + 1) * QBLK, :] = imp[:, c * QBLK:(c + 1) * QBLK].T
        imp_s = C_SEL_OVERLAP[0] * impt_ref[pl.ds(7, ns, stride=SEL_PER_CMP), :]
        for o_off in range(1, len(C_SEL_OVERLAP)):
            imp_s = imp_s + C_SEL_OVERLAP[o_off] * impt_ref[pl.ds(7 + o_off, ns, stride=SEL_PER_CMP), :]
        score = jnp.where(forced, 1e4, jnp.where(bvalid, imp_s, -1.0))
        selt = jnp.zeros((ns, QBLK), F32)
        for _ in range(n_sel):
            best = jnp.max(score, axis=0, keepdims=True)
            first = jnp.min(jnp.where(score == best, jj, ns), axis=0, keepdims=True)
            hit = jj == first
            selt = jnp.where(hit, 1.0, selt)
            score = jnp.where(hit, NEG_INF, score)
        sel_ref[kh, 0:ns, :] = selt
        if ns < LANES:
            sel_ref[kh, ns:LANES, :] = jnp.zeros((LANES - ns, QBLK), F32)


def _nsa_slc_kernel(qt_ref, sel_ref, ks_ref, vst_ref, oslc_ref, acc_ref):
    gq = C_GROUP * QBLK
    blocks_per_tile = SEL_TILE // C_SEL_LEN
    t0 = pl.program_id(1) * QBLK
    lane = lax.broadcasted_iota(jnp.int32, (QBLK, LANES), 1)
    first_half = lane < HEAD_DIM
    row = lax.broadcasted_iota(jnp.int32, (LANES, QBLK), 0)
    top_rows = row < HEAD_DIM
    n_full = t0 // SEL_TILE
    key_in_tile = lax.broadcasted_iota(jnp.int32, (SEL_TILE, gq), 0)
    query_pos = t0 + lax.broadcasted_iota(jnp.int32, (SEL_TILE, gq), 1) % QBLK

    for kh in range(C_KV_HEADS):
        mp, e = divmod(kh, 2)
        cl = slice(mp * LANES, (mp + 1) * LANES)
        keep_rows = top_rows if e == 0 else jnp.logical_not(top_rows)
        qt = jnp.concatenate(
            [jnp.where(keep_rows, qt_ref[(C_GROUP * mp + i) * LANES:(C_GROUP * mp + i + 1) * LANES, :], 0)
             for i in range(C_GROUP)], axis=1)
        acc_ref[...] = jnp.zeros((LANES, gq), F32)

        def tile_step(kt, carry, diagonal, qt=qt, cl=cl, kh=kh):
            m_old, l_old = carry
            k0 = pl.multiple_of(kt * SEL_TILE, SEL_TILE)
            st = jnp.dot(ks_ref[pl.ds(k0, SEL_TILE), cl], qt, preferred_element_type=F32)
            picked = sel_ref[kh, pl.ds(pl.multiple_of(kt * blocks_per_tile, blocks_per_tile), blocks_per_tile), :]
            picked = jnp.concatenate([picked] * C_GROUP, axis=1) > 0.5
            ok = jnp.concatenate(
                [jnp.broadcast_to(picked[j:j + 1, :], (C_SEL_LEN, gq)) for j in range(blocks_per_tile)], axis=0)
            if diagonal:
                ok = ok & (k0 + key_in_tile <= query_pos)
            st = jnp.where(ok, st, NEG_INF)
            m_new = jnp.maximum(m_old, jnp.max(st, axis=0, keepdims=True))
            alpha = jnp.exp(m_old - m_new)
            p = jnp.exp(st - m_new)
            l_new = alpha * l_old + jnp.sum(p, axis=0, keepdims=True)
            acc_ref[...] = alpha * acc_ref[...] + jnp.dot(
                vst_ref[kt, cl, :], p.astype(BF16), preferred_element_type=F32)
            return m_new, l_new

        carry = (jnp.full((1, gq), NEG_INF, F32), jnp.zeros((1, gq), F32))
        carry = lax.fori_loop(0, n_full, functools.partial(tile_step, diagonal=False), carry)
        _, l_fin = lax.fori_loop(n_full, n_full + 1, functools.partial(tile_step, diagonal=True), carry)
        ot = acc_ref[...] / l_fin
        out = jnp.concatenate([ot[:, i * QBLK:(i + 1) * QBLK].T for i in range(C_GROUP)], axis=0)
        _store_group_heads(oslc_ref, out, mp, e, first_half)


def _nsa_select(q_arr, qt_arr, kcmp, vcmp, ks_arr, vst_arr, *, ks_off):
    b, s, _ = q_arr.shape
    nc = kcmp.shape[1]
    ns = nc // SEL_PER_CMP
    assert ns <= LANES and s % SEL_TILE == 0
    qw = N_PAIRS * LANES
    gq = C_GROUP * QBLK
    q_spec = pl.BlockSpec((None, QBLK, qw), lambda bi, i: (bi, i, 0))
    o_spec = pl.BlockSpec((None, QBLK, qw), lambda bi, i: (bi, i, 0))
    sel_spec = pl.BlockSpec((None, C_KV_HEADS, LANES, QBLK), lambda bi, i: (bi, 0, 0, i))
    o_shape = jax.ShapeDtypeStruct((b, s, qw), F32)
    o_cmp, sel = pl.pallas_call(
        functools.partial(_nsa_cmp_kernel, n_sel=min(C_N_SEL, ns)),
        grid=(b, s // QBLK),
        in_specs=[q_spec,
                  pl.BlockSpec((None, nc, C_KV), lambda bi, i: (bi, 0, 0)),
                  pl.BlockSpec((None, nc, C_KV), lambda bi, i: (bi, 0, 0))],
        out_specs=[o_spec, sel_spec],
        out_shape=[o_shape, jax.ShapeDtypeStruct((b, C_KV_HEADS, LANES, s), F32)],
        scratch_shapes=[pltpu.VMEM((8 + nc, QBLK), F32)],
        compiler_params=_params(2),
        name="nsa_compressed",
    )(q_arr, kcmp, vcmp)
    o_slc = pl.pallas_call(
        _nsa_slc_kernel,
        grid=(b, s // QBLK),
        in_specs=[pl.BlockSpec((None, qw, QBLK), lambda bi, i: (bi, 0, i)),
                  sel_spec,
                  pl.BlockSpec((None, s, C_KV), lambda bi, i: (bi, 0, ks_off)),
                  pl.BlockSpec((None, s // SEL_TILE, C_KV, SEL_TILE), lambda bi, i: (bi, 0, 0, 0))],
        out_specs=o_spec,
        out_shape=o_shape,
        scratch_shapes=[pltpu.VMEM((LANES, gq), F32)],
        compiler_params=_params(2),
        name="nsa_selected",
    )(qt_arr, sel, ks_arr, vst_arr)
    return o_cmp, o_slc


def _head_cols(tile, col, first_half):
    tm = tile.shape[0]
    a = jnp.broadcast_to(tile[:, col:col + 1], (tm, LANES))
    b = jnp.broadcast_to(tile[:, col + 1:col + 2], (tm, LANES))
    return jnp.where(first_half, a, b)


def _post_kernel(*refs, kind, n_staged):
    it = iter(refs)
    n_branch = {"A": 3, "B": 1, "C": 3}[kind]
    o_refs = [next(it) for _ in range(n_branch)]
    aux_refs = []
    if kind == "A":
        aux_refs = [next(it) for _ in range(3)]
    elif kind == "C":
        aux_refs = [next(it)]
    z_ref, x_ref, p_ref, wo_ref, g_ref, b_ref, wg_ref, wp_ref, out_ref, u_ref = (next(it) for _ in range(10))
    stage_refs = [next(it) for _ in range(n_staged)]
    tm = x_ref.shape[0]
    lane = lax.broadcasted_iota(jnp.int32, (tm, LANES), 1)
    first_half = lane < HEAD_DIM

    def token_order(ref):
        if len(ref.shape) == 2:
            return lambda j: ref[:, j * LANES:(j + 1) * LANES]
        dil = ref.shape[0]
        stage = stage_refs.pop(0)
        for j in range(ref.shape[2] // LANES):
            for r in range(dil):
                stage[j, pl.ds(r, tm // dil, stride=dil), :] = ref[r, :, j * LANES:(j + 1) * LANES]
        return lambda j: stage[j]

    o_cols = [token_order(r) for r in o_refs]
    aux_cols = [token_order(r) for r in aux_refs]

    if kind == "A":
        lses = [c(0) for c in aux_cols]
        mx = jnp.maximum(jnp.maximum(lses[0], lses[1]), lses[2])
        ws = [jnp.exp(v - mx) for v in lses]
        den = ws[0] + ws[1] + ws[2]
        ws = [v / den for v in ws]
    elif kind == "C":
        gates = jax.nn.sigmoid(aux_cols[0](0))

    for pi in range(N_PAIRS):
        cl = slice(pi * LANES, (pi + 1) * LANES)
        if kind == "A":
            o = _head_cols(ws[0], 2 * pi, first_half) * o_cols[0](pi)
            for g in range(1, 3):
                o = o + _head_cols(ws[g], 2 * pi, first_half) * o_cols[g](pi)
        elif kind == "B":
            o = o_cols[0](pi)
        else:
            o = _head_cols(gates, 2 * pi, first_half) * o_cols[0](pi)
            for c in range(1, 3):
                o = o + _head_cols(gates, c * N_HEADS + 2 * pi, first_half) * o_cols[c](pi)
        z = z_ref[:, cl]
        u_ref[:, cl] = (o * (z * jax.nn.sigmoid(z))).astype(BF16)

    h = jnp.dot(u_ref[...], wo_ref[...], preferred_element_type=F32)
    y = DEEPNORM_ALPHA * x_ref[...] + h
    mu = jnp.mean(y, axis=-1, keepdims=True)
    yc = y - mu
    var = jnp.mean(yc * yc, axis=-1, keepdims=True)
    yn = yc * lax.rsqrt(var + LN_EPS) * g_ref[...] + b_ref[...]
    gate = jax.nn.sigmoid(jnp.dot(yn.astype(BF16), wg_ref[...], preferred_element_type=F32))
    pp = jnp.dot(p_ref[...].astype(BF16), wp_ref[...], preferred_element_type=F32)
    out_ref[...] = yn + gate * pp


def _post(kind, o_list, aux_list, z, x, p, w_out, ln_g, ln_b, w_gate, w_proj):
    t, d = x.shape
    tm = 256
    row = lambda w: pl.BlockSpec((tm, w), lambda i: (i, 0))
    full = lambda a: pl.BlockSpec(a.shape, lambda i: (0,) * a.ndim)

    branch_specs, stages = [], []
    for a in list(o_list) + list(aux_list):
        if a.ndim == 2:
            branch_specs.append(row(a.shape[1]))
        else:
            _, dil, sub_len, w = a.shape
            n_seq_tiles = sub_len * dil // tm
            branch_specs.append(pl.BlockSpec(
                (None, dil, tm // dil, w), lambda i, n=n_seq_tiles: (i // n, 0, i % n, 0)))
            stages.append(pltpu.VMEM((w // LANES, tm, LANES), F32))
    args = list(o_list) + list(aux_list) + [z, x, p, w_out, ln_g, ln_b, w_gate, w_proj]
    in_specs = branch_specs + [row(d), row(d), row(p.shape[1]), full(w_out), full(ln_g), full(ln_b),
                               full(w_gate), full(w_proj)]
    return pl.pallas_call(
        functools.partial(_post_kernel, kind=kind, n_staged=len(stages)),
        grid=(t // tm,),
        in_specs=in_specs,
        out_specs=row(d),
        out_shape=jax.ShapeDtypeStruct((t, d), F32),
        scratch_shapes=[pltpu.VMEM((tm, d), BF16)] + stages,
        compiler_params=_params(1),
        name=f"post_{kind}",
    )(*args)


def _rope_tables(seq_len):
    inv = 1.0 / (ROPE_THETA ** (jnp.arange(0, HEAD_DIM, 2, dtype=F32) / HEAD_DIM))
    ang = jnp.arange(seq_len, dtype=F32)[:, None] * inv[None, :]
    cos, sin = jnp.cos(ang), jnp.sin(ang)
    zero = jnp.zeros_like(sin)
    cos_t = jnp.concatenate([cos] * 4, axis=1)
    sa_t = jnp.concatenate([-sin, zero, -sin, zero], axis=1)
    sb_t = jnp.concatenate([zero, sin, zero, sin], axis=1)
    return cos_t, sa_t, sb_t


def _head_cols_index(head_order):
    return np.concatenate([np.arange(h * HEAD_DIM, (h + 1) * HEAD_DIM) for h in head_order])


_B_HEAD_ORDER = [e * (N_HEADS // B_KV_HEADS) + i for i in range(N_PAIRS) for e in range(2)]
_C_HEAD_ORDER = [C_GROUP * (2 * m + e) + i for m in range(C_KV_HEADS // 2) for i in range(C_GROUP) for e in range(2)]


def _mixer_a(xt, b, s, w_in, tabs):
    wd = N_HEADS * HEAD_DIM
    scale = HEAD_DIM ** -0.5
    z = _proj(xt, w_in[:, -wd:].astype(BF16), b, F32).reshape(b * s, wd)
    outs, lses = [], []
    for gi, (window, dil) in enumerate(A_GROUPS):
        base = 3 * gi * wd
        w_qkv = jnp.concatenate([w_in[:, base:base + wd] * scale, w_in[:, base + wd:base + 3 * wd]], axis=1)
        qkv = _proj(xt, w_qkv.astype(BF16), b, BF16, tabs, n_rope_cols=2 * wd, dil=dil)
        o, lse = _banded_attention(
            qkv, qkv, qkv, tq=min(256, s // dil), npv=1, max_dist=window // dil,
            q_off=0, k_off=1, v_off=2, kv_pairs=N_PAIRS, kv_pair_of=lambda pi: pi, want_lse=True)
        if dil == 1:
            o, lse = o.reshape(b * s, wd), lse.reshape(b * s, LANES)
        outs.append(o)
        lses.append(lse)
    return outs, lses, z


def _mixer_b(xt, b, s, w_in, sinks, tabs):
    wd = N_HEADS * HEAD_DIM
    kvw = B_KV_HEADS * HEAD_DIM
    perm = _head_cols_index(_B_HEAD_ORDER)
    wq = w_in[:, :wd][:, perm] * HEAD_DIM ** -0.5
    wk = w_in[:, wd:wd + kvw]
    wv = w_in[:, wd + kvw:wd + 2 * kvw]
    wz = w_in[:, wd + 2 * kvw:][:, perm]
    qk = _proj(xt, jnp.concatenate([wq, wk], axis=1).astype(BF16), b, BF16, tabs, n_rope_cols=wd + kvw)
    v = _proj(xt, wv.astype(BF16), b, BF16)
    z = _proj(xt, wz.astype(BF16), b, F32).reshape(b * s, wd)
    o = _banded_attention(
        qk, qk, v, tq=min(256, s), npv=1, max_dist=B_WINDOW - 1,
        q_off=0, k_off=wd // kvw, v_off=0, kv_pairs=1, kv_pair_of=lambda pi: 0,
        sinks=sinks[np.asarray(_B_HEAD_ORDER)].astype(F32))
    return [o.reshape(b * s, wd)], [], z, perm


def _mixer_c(xt, b, s, w_in, w_ck, w_cv, pos, tabs):
    wd = N_HEADS * HEAD_DIM
    perm = _head_cols_index(_C_HEAD_ORDER)
    cols = np.cumsum([0, wd] + [C_KV] * 6 + [3 * N_HEADS, wd])
    part = lambda i: w_in[:, cols[i]:cols[i + 1]]
    wq = part(0)[:, perm] * HEAD_DIM ** -0.5
    w_gl = part(7).reshape(-1, 3, N_HEADS)[:, :, np.asarray(_C_HEAD_ORDER)].reshape(-1, 3 * N_HEADS)
    w_gl = jnp.pad(w_gl, ((0, 0), (0, LANES - 3 * N_HEADS)))
    w_att = jnp.concatenate([wq, part(3), part(5), part(4), part(6)], axis=1)
    att = _proj(xt, w_att.astype(BF16), b, BF16, tabs, n_rope_cols=wd + 2 * C_KV)
    cmp_in = _proj(xt, jnp.concatenate([part(1), part(2)], axis=1).astype(BF16), b, F32, tabs,
                   n_rope_cols=C_KV, tn=C_KV).reshape(b, s, 2 * C_KV)
    gl = _proj(xt, w_gl.astype(BF16), b, F32).reshape(b * s, LANES)
    z = _proj(xt, part(8)[:, perm].astype(BF16), b, F32).reshape(b * s, wd)

    nc = s // C_CMP_STRIDE
    cw = C_CMP_STRIDE * HEAD_DIM

    def chunks(t):
        return jnp.transpose(t.reshape(b, s, C_KV_HEADS, HEAD_DIM), (0, 2, 1, 3)).reshape(b, C_KV_HEADS, nc, cw)

    pos2 = pos.reshape(2, cw)

    def compressed(t, w):
        c = _compress(chunks(t), pos2, w.reshape(2, cw, HEAD_DIM).astype(BF16))
        return jnp.transpose(c, (0, 2, 1, 3)).reshape(b, nc, C_KV)

    kcmp, vcmp = compressed(cmp_in[:, :, :C_KV], w_ck), compressed(cmp_in[:, :, C_KV:], w_cv)
    att3 = att.reshape(b, s, -1)
    qt = jnp.transpose(att3[:, :, :wd], (0, 2, 1))
    vs = att3[:, :, wd + 2 * C_KV:wd + 3 * C_KV]
    vst = jnp.transpose(vs.reshape(b, s // SEL_TILE, SEL_TILE, C_KV), (0, 1, 3, 2))
    o_cmp, o_slc = _nsa_select(att3, qt, kcmp, vcmp, att3, vst, ks_off=wd // C_KV)
    o_win = _banded_attention(
        att, att, att, tq=C_WINDOW, npv=C_WINDOW // QBLK, max_dist=C_WINDOW - 1,
        q_off=0, k_off=wd // C_KV + 1, v_off=wd // C_KV + 3, kv_pairs=C_KV_HEADS // 2,
        kv_pair_of=lambda pi: pi // C_GROUP)
    outs = [o.reshape(b * s, wd) for o in (o_cmp, o_slc, o_win)]
    return outs, [gl], z, perm


def kernel(x, p, a_w_in, a_w_out, b_w_in, b_sinks, b_w_out, c_w_in, c_w_ck, c_w_cv, c_pos, c_w_out,
           ln_g, ln_b, ple_w_proj, ple_w_gate):
    b, s, d = x.shape
    assert d == N_HEADS * HEAD_DIM and s % (QBLK * A_GROUPS[-1][1]) == 0 and s % C_WINDOW == 0
    tabs = _rope_tables(s)
    xt = x.reshape(b * s, d)
    for i in range(DEPTH):
        j, kind = divmod(i, N_MIXERS)
        if kind == 0:
            outs, aux, z = _mixer_a(xt, b, s, a_w_in[j], tabs)
            w_out, name = a_w_out[j], "A"
        elif kind == 1:
            outs, aux, z, perm = _mixer_b(xt, b, s, b_w_in[j], b_sinks[j], tabs)
            w_out, name = b_w_out[j][perm, :], "B"
        else:
            outs, aux, z, perm = _mixer_c(xt, b, s, c_w_in[j], c_w_ck[j], c_w_cv[j], c_pos[j], tabs)
            w_out, name = c_w_out[j][perm, :], "C"
        xt = _post(name, outs, aux, z, xt, p[i].reshape(b * s, -1), w_out.astype(BF16),
                   ln_g[i].reshape(1, d), ln_b[i].reshape(1, d),
                   ple_w_gate[i].astype(BF16), ple_w_proj[i].astype(BF16))
    return xt.reshape(b, s, d)
```

```python
import functools

import numpy as np
import jax
import jax.numpy as jnp
from jax import lax
from jax.experimental import pallas as pl
from jax.experimental.pallas import tpu as pltpu

F32 = jnp.float32
BF16 = jnp.bfloat16

LANES = 128
VMEM_LIMIT_BYTES = 56 * 1024 * 1024

HEAD_DIM = 64
HALF = HEAD_DIM // 2
PAIR = 2 * HEAD_DIM
assert PAIR == LANES
N_HEADS = 16
N_PAIRS = N_HEADS // 2
ROPE_THETA = 10000.0
QBLK = 128
LN_EPS = 1e-5
DEPTH = 4
N_MIXERS = 3
DEEPNORM_ALPHA = (2 * DEPTH) ** 0.25
A_GROUPS = ((128, 1), (512, 4), (2048, 16))
B_KV_HEADS = 2
B_WINDOW = 128
C_KV_HEADS = 4
C_GROUP = N_HEADS // C_KV_HEADS
C_KV = C_KV_HEADS * HEAD_DIM
C_CMP_STRIDE = 16
C_CMP_LEN = 32
C_SEL_LEN = 64
C_N_SEL = 16
C_WINDOW = 512
C_SEL_OVERLAP = (1.0, 2.0, 2.0, 2.0, 1.0)
SEL_PER_CMP = C_SEL_LEN // C_CMP_STRIDE
SEL_TILE = 512
NEG_INF = float("-inf")
MASKED = -1e30

_NT = (((1,), (1,)), ((), ()))


def _params(n_grid):
    return pltpu.CompilerParams(
        dimension_semantics=("arbitrary",) * n_grid, vmem_limit_bytes=VMEM_LIMIT_BYTES)


def _proj_kernel(*refs, n_rope_tiles, n_tiles, dil):
    it = iter(refs)
    x_ref, w_ref = next(it), next(it)
    if n_rope_tiles:
        cos_ref, sa_ref, sb_ref = next(it), next(it), next(it)
    o_ref = next(it)
    stage_ref = next(it) if dil > 1 else None
    tm, tn = x_ref.shape[0], w_ref.shape[1]

    def emit(rope):
        acc = jnp.dot(x_ref[...], w_ref[...], preferred_element_type=F32)
        if rope:
            c, sa, sb = cos_ref[...], sa_ref[...], sb_ref[...]
        for j in range(tn // LANES):
            cl = slice(j * LANES, (j + 1) * LANES)
            t = acc[:, cl]
            if rope:
                t = t * c + pltpu.roll(t, LANES - HALF, 1) * sa + pltpu.roll(t, HALF, 1) * sb
            if dil == 1:
                o_ref[0, :, cl] = t.astype(o_ref.dtype)
            else:
                stage_ref[j] = t
                for r in range(dil):
                    o_ref[r, :, cl] = stage_ref[j, pl.ds(r, tm // dil, stride=dil), :].astype(o_ref.dtype)

    if n_rope_tiles == 0 or n_rope_tiles == n_tiles:
        emit(n_rope_tiles > 0)
    else:
        pl.when(pl.program_id(1) < n_rope_tiles)(lambda: emit(True))
        pl.when(pl.program_id(1) >= n_rope_tiles)(lambda: emit(False))


def _pick_tile(n, candidates):
    for c in candidates:
        if n % c == 0:
            return c
    raise ValueError(f"no tile for {n}")


def _proj(x, w, batch, out_dtype, rope_tabs=None, n_rope_cols=0, dil=1, tn=None):
    t, k = x.shape
    n = w.shape[1]
    seq_len = t // batch
    tm = _pick_tile(seq_len, (1024, 512, 256, 128))
    tn = tn or _pick_tile(n, (512, 384, 256, 128))
    assert n % tn == 0 and n_rope_cols % tn == 0 and tm % (8 * dil) == 0
    n_seq_tiles = seq_len // tm
    in_specs = [pl.BlockSpec((tm, k), lambda i, j: (i, 0)),
                pl.BlockSpec((k, tn), lambda i, j: (0, j))]
    args = [x, w]
    if n_rope_cols:
        tab_spec = pl.BlockSpec((tm, LANES), lambda i, j: (i % n_seq_tiles, 0))
        in_specs += [tab_spec] * 3
        args += list(rope_tabs)
    scratch = []
    if dil > 1:
        scratch.append(pltpu.VMEM((tn // LANES, tm, LANES), F32))
    return pl.pallas_call(
        functools.partial(_proj_kernel, n_rope_tiles=n_rope_cols // tn, n_tiles=n // tn, dil=dil),
        grid=(t // tm, n // tn),
        in_specs=in_specs,
        out_specs=pl.BlockSpec((None, dil, tm // dil, tn),
                               lambda i, j: (i // n_seq_tiles, 0, i % n_seq_tiles, j)),
        out_shape=jax.ShapeDtypeStruct((batch, dil, seq_len // dil, n), out_dtype),
        scratch_shapes=scratch,
        compiler_params=_params(2),
        name=f"proj_d{dil}_r{n_rope_cols}",
    )(*args)


def _band_kernel(*refs, tq, npv, max_dist, kv_pair_of, has_sinks, want_lse):
    it = iter(refs)
    q_ref, kc_ref, kp_ref, vc_ref, vp_ref = (next(it) for _ in range(5))
    sink_ref = next(it) if has_sinks else None
    o_ref = next(it)
    lse_ref = next(it) if want_lse else None
    qb = pl.program_id(2)
    w = (npv + 1) * QBLK
    pv_rows = npv * QBLK
    lane = lax.broadcasted_iota(jnp.int32, (QBLK, LANES), 1)
    first_half = lane < HEAD_DIM
    qi = lax.broadcasted_iota(jnp.int32, (QBLK, w), 0)
    kj = lax.broadcasted_iota(jnp.int32, (QBLK, w), 1)
    dist = qi + npv * QBLK - kj
    band = (dist >= 0) & (dist <= max_dist)
    for sub in range(tq // QBLK):
        r0 = sub * QBLK
        kstart = qb * tq + r0 - npv * QBLK
        mask = band & (kj + kstart >= 0)
        lse_tile = jnp.zeros((QBLK, LANES), F32)
        for pi in range(N_PAIRS):
            cl = slice(kv_pair_of(pi) * LANES, (kv_pair_of(pi) + 1) * LANES)
            k_parts, v_parts = [], []
            if r0 < pv_rows:
                k_parts.append(kp_ref[r0:pv_rows, cl])
                v_parts.append(vp_ref[r0:pv_rows, cl])
            cs = max(r0 - pv_rows, 0)
            k_parts.append(kc_ref[cs:r0 + QBLK, cl])
            v_parts.append(vc_ref[cs:r0 + QBLK, cl])
            kwin = k_parts[0] if len(k_parts) == 1 else jnp.concatenate(k_parts, axis=0)
            vwin = v_parts[0] if len(v_parts) == 1 else jnp.concatenate(v_parts, axis=0)
            qp = q_ref[r0:r0 + QBLK, pi * LANES:(pi + 1) * LANES]
            outs = []
            for e in range(2):
                qe = jnp.where(first_half if e == 0 else jnp.logical_not(first_half), qp, 0)
                s = lax.dot_general(qe, kwin, _NT, preferred_element_type=F32)
                s = jnp.where(mask, s, NEG_INF)
                m = jnp.max(s, axis=-1, keepdims=True)
                if has_sinks:
                    sk = sink_ref[2 * pi + e]
                    m = jnp.maximum(m, sk)
                p = jnp.exp(s - m)
                l = jnp.sum(p, axis=-1, keepdims=True)
                if has_sinks:
                    l = l + jnp.exp(sk - m)
                pv = jnp.dot(p.astype(BF16), vwin, preferred_element_type=F32)
                outs.append(pv / l)
                if want_lse:
                    lse_tile = jnp.where(lane == 2 * pi + e, m + jnp.log(l), lse_tile)
            o_ref[r0:r0 + QBLK, pi * LANES:(pi + 1) * LANES] = jnp.where(first_half, outs[0], outs[1])
        if want_lse:
            lse_ref[r0:r0 + QBLK, :] = lse_tile


def _banded_attention(q_arr, k_arr, v_arr, *, tq, npv, max_dist, q_off, k_off, v_off,
                      kv_pairs, kv_pair_of, sinks=None, want_lse=False):
    b, dil, l, _ = q_arr.shape
    qw, kw = N_PAIRS * LANES, kv_pairs * LANES
    pv = npv * QBLK
    assert tq % pv == 0
    in_specs = [
        pl.BlockSpec((None, None, tq, qw), lambda bi, r, i: (bi, r, i, q_off)),
        pl.BlockSpec((None, None, tq, kw), lambda bi, r, i: (bi, r, i, k_off)),
        pl.BlockSpec((None, None, pv, kw), lambda bi, r, i: (bi, r, jnp.maximum(i * (tq // pv) - 1, 0), k_off)),
        pl.BlockSpec((None, None, tq, kw), lambda bi, r, i: (bi, r, i, v_off)),
        pl.BlockSpec((None, None, pv, kw), lambda bi, r, i: (bi, r, jnp.maximum(i * (tq // pv) - 1, 0), v_off)),
    ]
    args = [q_arr, k_arr, k_arr, v_arr, v_arr]
    if sinks is not None:
        in_specs.append(pl.BlockSpec(memory_space=pltpu.SMEM))
        args.append(sinks)
    out_specs = [pl.BlockSpec((None, None, tq, qw), lambda bi, r, i: (bi, r, i, 0))]
    out_shape = [jax.ShapeDtypeStruct((b, dil, l, qw), F32)]
    if want_lse:
        out_specs.append(pl.BlockSpec((None, None, tq, LANES), lambda bi, r, i: (bi, r, i, 0)))
        out_shape.append(jax.ShapeDtypeStruct((b, dil, l, LANES), F32))
    res = pl.pallas_call(
        functools.partial(_band_kernel, tq=tq, npv=npv, max_dist=max_dist, kv_pair_of=kv_pair_of,
                          has_sinks=sinks is not None, want_lse=want_lse),
        grid=(b, dil, l // tq),
        in_specs=in_specs,
        out_specs=out_specs,
        out_shape=out_shape,
        compiler_params=_params(3),
        name=f"band_d{dil}_w{max_dist}",
    )(*args)
    return res if want_lse else res[0]


def _compress_kernel(c_ref, pos_ref, w_ref, o_ref):
    c = c_ref[...]
    top = jnp.dot((c + pos_ref[0:1, :]).astype(BF16), w_ref[0], preferred_element_type=F32)
    bot = jnp.dot((c + pos_ref[1:2, :]).astype(BF16), w_ref[1], preferred_element_type=F32)
    nc = c.shape[0]
    o_ref[...] = (top + pltpu.roll(bot, nc - 1, 0)).astype(o_ref.dtype)


def _compress(chunks, pos, w):
    b, hk, nc, cw = chunks.shape
    return pl.pallas_call(
        _compress_kernel,
        grid=(b, hk),
        in_specs=[pl.BlockSpec((None, None, nc, cw), lambda bi, h: (bi, h, 0, 0)),
                  pl.BlockSpec((2, cw), lambda bi, h: (0, 0)),
                  pl.BlockSpec((2, cw, HEAD_DIM), lambda bi, h: (0, 0, 0))],
        out_specs=pl.BlockSpec((None, None, nc, HEAD_DIM), lambda bi, h: (bi, h, 0, 0)),
        out_shape=jax.ShapeDtypeStruct((b, hk, nc, HEAD_DIM), BF16),
        compiler_params=_params(2),
        name="nsa_compress",
    )(chunks, pos, w)


def _stack_group_queries(q_ref, mp, half):
    pairs = [C_GROUP * mp + i for i in range(C_GROUP)]
    return jnp.concatenate(
        [jnp.where(half, q_ref[:, pr * LANES:(pr + 1) * LANES], 0) for pr in pairs], axis=0)


def _store_group_heads(o_ref, val, mp, e, first_half):
    for i in range(C_GROUP):
        ol = slice((C_GROUP * mp + i) * LANES, (C_GROUP * mp + i + 1) * LANES)
        rows = slice(i * QBLK, (i + 1) * QBLK)
        if e == 0:
            o_ref[:, ol] = val[rows]
        else:
            o_ref[:, ol] = jnp.where(first_half, o_ref[:, ol], val[rows])


def _nsa_cmp_kernel(q_ref, kc_ref, vc_ref, ocmp_ref, sel_ref, impt_ref, *, n_sel):
    nc = kc_ref.shape[0]
    ns = nc // SEL_PER_CMP
    t0 = pl.program_id(1) * QBLK
    lane = lax.broadcasted_iota(jnp.int32, (QBLK, LANES), 1)
    first_half = lane < HEAD_DIM

    qi_c = lax.broadcasted_iota(jnp.int32, (QBLK, nc), 0)
    nn_c = lax.broadcasted_iota(jnp.int32, (QBLK, nc), 1)
    cvalid = nn_c * C_CMP_STRIDE + (C_CMP_LEN - 1) <= t0 + qi_c
    cvalid = jnp.concatenate([cvalid] * C_GROUP, axis=0)

    jj = lax.broadcasted_iota(jnp.int32, (ns, QBLK), 0)
    cur = (t0 + lax.broadcasted_iota(jnp.int32, (ns, QBLK), 1)) // C_SEL_LEN
    forced = (jj == 0) | (jj == cur) | (jj == cur - 1)
    bvalid = jj <= cur

    impt_ref[0:8, :] = jnp.zeros((8, QBLK), F32)

    for kh in range(C_KV_HEADS):
        mp, e = divmod(kh, 2)
        cl = slice(mp * LANES, (mp + 1) * LANES)
        half = first_half if e == 0 else jnp.logical_not(first_half)
        qst = _stack_group_queries(q_ref, mp, half)

        sc = lax.dot_general(qst, kc_ref[:, cl], _NT, preferred_element_type=F32)
        sc = jnp.where(cvalid, sc, NEG_INF)
        mx = jnp.max(sc, axis=-1, keepdims=True)
        mx = jnp.where(mx > NEG_INF, mx, 0.0)
        ee = jnp.exp(sc - mx)
        pc = ee / jnp.maximum(jnp.sum(ee, axis=-1, keepdims=True), 1e-30)
        ocmp = jnp.dot(pc.astype(BF16), vc_ref[:, cl], preferred_element_type=F32)
        _store_group_heads(ocmp_ref, ocmp, mp, e, first_half)

        imp = pc[0:QBLK]
        for g in range(1, C_GROUP):
            imp = imp + pc[g * QBLK:(g + 1) * QBLK]
        for c in range(nc // QBLK):
            impt_ref[8 + c * QBLK:8 + (c + 1) * QBLK, :] = imp[:, c * QBLK:(c + 1) * QBLK].T
        imp_s = C_SEL_OVERLAP[0] * impt_ref[pl.ds(7, ns, stride=SEL_PER_CMP), :]
        for o_off in range(1, len(C_SEL_OVERLAP)):
            imp_s = imp_s + C_SEL_OVERLAP[o_off] * impt_ref[pl.ds(7 + o_off, ns, stride=SEL_PER_CMP), :]
        score = jnp.where(forced, 1e4, jnp.where(bvalid, imp_s, -1.0))
        selt = jnp.zeros((ns, QBLK), F32)
        for _ in range(n_sel):
            best = jnp.max(score, axis=0, keepdims=True)
            first = jnp.min(jnp.where(score == best, jj, ns), axis=0, keepdims=True)
            hit = jj == first
            selt = jnp.where(hit, 1.0, selt)
            score = jnp.where(hit, NEG_INF, score)
        sel_ref[kh, 0:ns, :] = selt
        if ns < LANES:
            sel_ref[kh, ns:LANES, :] = jnp.zeros((LANES - ns, QBLK), F32)


def _nsa_slc_kernel(qt_ref, sel_ref, ks_ref, vst_ref, oslc_ref, acc_ref):
    gq = C_GROUP * QBLK
    blocks_per_tile = SEL_TILE // C_SEL_LEN
    bias_rows = 16
    t0 = pl.program_id(1) * QBLK
    lane = lax.broadcasted_iota(jnp.int32, (QBLK, LANES), 1)
    first_half = lane < HEAD_DIM
    row = lax.broadcasted_iota(jnp.int32, (LANES, QBLK), 0)
    top_rows = row < HEAD_DIM
    n_full = t0 // SEL_TILE
    key_in_tile = lax.broadcasted_iota(jnp.int32, (SEL_TILE, gq), 0)
    query_pos = t0 + lax.broadcasted_iota(jnp.int32, (SEL_TILE, gq), 1) % QBLK
    block_of_key = lax.broadcasted_iota(jnp.int32, (SEL_TILE, LANES), 0) // C_SEL_LEN
    block_onehot = jnp.where(
        block_of_key == lax.broadcasted_iota(jnp.int32, (SEL_TILE, LANES), 1), 1.0, 0.0).astype(BF16)
    bias_pad = jnp.zeros((LANES - bias_rows, gq), BF16)

    for mp in range(C_KV_HEADS // 2):
        cl = slice(mp * LANES, (mp + 1) * LANES)
        qts = []
        for e in range(2):
            keep_rows = top_rows if e == 0 else jnp.logical_not(top_rows)
            qts.append(jnp.concatenate(
                [jnp.where(keep_rows, qt_ref[(C_GROUP * mp + i) * LANES:(C_GROUP * mp + i + 1) * LANES, :], 0)
                 for i in range(C_GROUP)], axis=1))
        acc_ref[...] = jnp.zeros((2, LANES, gq), F32)

        def tile_step(kt, carry, diagonal, qts=qts, cl=cl, mp=mp):
            k0 = pl.multiple_of(kt * SEL_TILE, SEL_TILE)
            b0 = pl.multiple_of(kt * blocks_per_tile, blocks_per_tile)
            keys = jnp.concatenate([ks_ref[pl.ds(k0, SEL_TILE), cl], block_onehot], axis=1)
            vt = vst_ref[kt, cl, :]
            new_carry = []
            for e in range(2):
                m_old, l_old = carry[2 * e], carry[2 * e + 1]
                picked = sel_ref[2 * mp + e, pl.ds(b0, blocks_per_tile), :]
                bias = jnp.concatenate([(picked - 1.0) * -MASKED] * C_GROUP, axis=1)
                bias = jnp.concatenate([bias, jnp.zeros((bias_rows - blocks_per_tile, gq), F32)], axis=0)
                queries = jnp.concatenate([qts[e], bias.astype(BF16), bias_pad], axis=0)
                st = jnp.dot(keys, queries, preferred_element_type=F32)
                if diagonal:
                    st = jnp.where(k0 + key_in_tile <= query_pos, st, MASKED)
                m_new = jnp.maximum(m_old, jnp.max(st, axis=0, keepdims=True))
                alpha = jnp.exp(m_old - m_new)
                p = jnp.exp(st - m_new)
                l_new = alpha * l_old + jnp.sum(p, axis=0, keepdims=True)
                acc_ref[e] = alpha * acc_ref[e] + jnp.dot(vt, p.astype(BF16), preferred_element_type=F32)
                new_carry += [m_new, l_new]
            return tuple(new_carry)

        init = (jnp.full((1, gq), NEG_INF, F32), jnp.zeros((1, gq), F32)) * 2
        carry = lax.fori_loop(0, n_full, functools.partial(tile_step, diagonal=False), init)
        carry = lax.fori_loop(n_full, n_full + 1, functools.partial(tile_step, diagonal=True), carry)
        ots = [acc_ref[e] / carry[2 * e + 1] for e in range(2)]
        for i in range(C_GROUP):
            ol = slice((C_GROUP * mp + i) * LANES, (C_GROUP * mp + i + 1) * LANES)
            qs = slice(i * QBLK, (i + 1) * QBLK)
            oslc_ref[:, ol] = jnp.where(first_half, ots[0][:, qs].T, ots[1][:, qs].T)


def _nsa_select(q_arr, qt_arr, kcmp, vcmp, ks_arr, vst_arr, *, ks_off):
    b, s, _ = q_arr.shape
    nc = kcmp.shape[1]
    ns = nc // SEL_PER_CMP
    assert ns <= LANES and s % SEL_TILE == 0
    qw = N_PAIRS * LANES
    gq = C_GROUP * QBLK
    q_spec = pl.BlockSpec((None, QBLK, qw), lambda bi, i: (bi, i, 0))
    o_spec = pl.BlockSpec((None, QBLK, qw), lambda bi, i: (bi, i, 0))
    sel_spec = pl.BlockSpec((None, C_KV_HEADS, LANES, QBLK), lambda bi, i: (bi, 0, 0, i))
    o_shape = jax.ShapeDtypeStruct((b, s, qw), F32)
    o_cmp, sel = pl.pallas_call(
        functools.partial(_nsa_cmp_kernel, n_sel=min(C_N_SEL, ns)),
        grid=(b, s // QBLK),
        in_specs=[q_spec,
                  pl.BlockSpec((None, nc, C_KV), lambda bi, i: (bi, 0, 0)),
                  pl.BlockSpec((None, nc, C_KV), lambda bi, i: (bi, 0, 0))],
        out_specs=[o_spec, sel_spec],
        out_shape=[o_shape, jax.ShapeDtypeStruct((b, C_KV_HEADS, LANES, s), F32)],
        scratch_shapes=[pltpu.VMEM((8 + nc, QBLK), F32)],
        compiler_params=_params(2),
        name="nsa_compressed",
    )(q_arr, kcmp, vcmp)
    o_slc = pl.pallas_call(
        _nsa_slc_kernel,
        grid=(b, s // QBLK),
        in_specs=[pl.BlockSpec((None, qw, QBLK), lambda bi, i: (bi, 0, i)),
                  sel_spec,
                  pl.BlockSpec((None, s, C_KV), lambda bi, i: (bi, 0, ks_off)),
                  pl.BlockSpec((None, s // SEL_TILE, C_KV, SEL_TILE), lambda bi, i: (bi, 0, 0, 0))],
        out_specs=o_spec,
        out_shape=o_shape,
        scratch_shapes=[pltpu.VMEM((2, LANES, gq), F32)],
        compiler_params=_params(2),
        name="nsa_selected",
    )(qt_arr, sel, ks_arr, vst_arr)
    return o_cmp, o_slc


def _head_cols(tile, col, first_half):
    tm = tile.shape[0]
    a = jnp.broadcast_to(tile[:, col:col + 1], (tm, LANES))
    b = jnp.broadcast_to(tile[:, col + 1:col + 2], (tm, LANES))
    return jnp.where(first_half, a, b)


def _post_kernel(*refs, kind, n_staged):
    it = iter(refs)
    n_branch = {"A": 3, "B": 1, "C": 3}[kind]
    o_refs = [next(it) for _ in range(n_branch)]
    aux_refs = [next(it) for _ in range(3)] if kind == "A" else []
    x_ref, xb_ref, p_ref, wz_ref = (next(it) for _ in range(4))
    wgl_ref = next(it) if kind == "C" else None
    wo_ref, g_ref, b_ref, wg_ref, wp_ref, out_ref, outb_ref, u_ref, z_ref = (next(it) for _ in range(9))
    stage_refs = [next(it) for _ in range(n_staged)]
    tm = x_ref.shape[0]
    lane = lax.broadcasted_iota(jnp.int32, (tm, LANES), 1)
    first_half = lane < HEAD_DIM

    def token_order(ref):
        if len(ref.shape) == 2:
            return lambda j: ref[:, j * LANES:(j + 1) * LANES]
        dil = ref.shape[0]
        stage = stage_refs.pop(0)
        for j in range(ref.shape[2] // LANES):
            for r in range(dil):
                stage[j, pl.ds(r, tm // dil, stride=dil), :] = ref[r, :, j * LANES:(j + 1) * LANES]
        return lambda j: stage[j]

    o_cols = [token_order(r) for r in o_refs]
    aux_cols = [token_order(r) for r in aux_refs]

    xb = xb_ref[...]
    z_ref[...] = jnp.dot(xb, wz_ref[...], preferred_element_type=F32)
    if kind == "A":
        lses = [c(0) for c in aux_cols]
        mx = jnp.maximum(jnp.maximum(lses[0], lses[1]), lses[2])
        ws = [jnp.exp(v - mx) for v in lses]
        den = ws[0] + ws[1] + ws[2]
        ws = [v / den for v in ws]
    elif kind == "C":
        gates = jax.nn.sigmoid(jnp.dot(xb, wgl_ref[...], preferred_element_type=F32))

    for pi in range(N_PAIRS):
        cl = slice(pi * LANES, (pi + 1) * LANES)
        if kind == "A":
            o = _head_cols(ws[0], 2 * pi, first_half) * o_cols[0](pi)
            for g in range(1, 3):
                o = o + _head_cols(ws[g], 2 * pi, first_half) * o_cols[g](pi)
        elif kind == "B":
            o = o_cols[0](pi)
        else:
            o = _head_cols(gates, 2 * pi, first_half) * o_cols[0](pi)
            for c in range(1, 3):
                o = o + _head_cols(gates, c * N_HEADS + 2 * pi, first_half) * o_cols[c](pi)
        z = z_ref[:, cl]
        u_ref[:, cl] = (o * (z * jax.nn.sigmoid(z))).astype(BF16)

    h = jnp.dot(u_ref[...], wo_ref[...], preferred_element_type=F32)
    y = DEEPNORM_ALPHA * x_ref[...] + h
    mu = jnp.mean(y, axis=-1, keepdims=True)
    yc = y - mu
    var = jnp.mean(yc * yc, axis=-1, keepdims=True)
    yn = yc * lax.rsqrt(var + LN_EPS) * g_ref[...] + b_ref[...]
    gate = jax.nn.sigmoid(jnp.dot(yn.astype(BF16), wg_ref[...], preferred_element_type=F32))
    pp = jnp.dot(p_ref[...].astype(BF16), wp_ref[...], preferred_element_type=F32)
    x_new = yn + gate * pp
    out_ref[...] = x_new
    outb_ref[...] = x_new.astype(BF16)


def _post(kind, o_list, aux_list, x, xb, p, w_z, w_gl, w_out, ln_g, ln_b, w_gate, w_proj):
    t, d = x.shape
    tm = 256
    row = lambda w: pl.BlockSpec((tm, w), lambda i: (i, 0))
    full = lambda a: pl.BlockSpec(a.shape, lambda i: (0,) * a.ndim)

    branch_specs, stages = [], []
    for a in list(o_list) + list(aux_list):
        if a.ndim == 2:
            branch_specs.append(row(a.shape[1]))
        else:
            _, dil, sub_len, w = a.shape
            n_seq_tiles = sub_len * dil // tm
            branch_specs.append(pl.BlockSpec(
                (None, dil, tm // dil, w), lambda i, n=n_seq_tiles: (i // n, 0, i % n, 0)))
            stages.append(pltpu.VMEM((w // LANES, tm, LANES), F32))
    weights = [w_z] + ([w_gl] if kind == "C" else []) + [w_out, ln_g, ln_b, w_gate, w_proj]
    args = list(o_list) + list(aux_list) + [x, xb, p] + weights
    in_specs = branch_specs + [row(d), row(d), row(p.shape[1])] + [full(w) for w in weights]
    return pl.pallas_call(
        functools.partial(_post_kernel, kind=kind, n_staged=len(stages)),
        grid=(t // tm,),
        in_specs=in_specs,
        out_specs=[row(d), row(d)],
        out_shape=[jax.ShapeDtypeStruct((t, d), F32), jax.ShapeDtypeStruct((t, d), BF16)],
        scratch_shapes=[pltpu.VMEM((tm, d), BF16), pltpu.VMEM((tm, d), F32)] + stages,
        compiler_params=_params(1),
        name=f"post_{kind}",
    )(*args)


def _rope_tables(seq_len):
    inv = 1.0 / (ROPE_THETA ** (jnp.arange(0, HEAD_DIM, 2, dtype=F32) / HEAD_DIM))
    ang = jnp.arange(seq_len, dtype=F32)[:, None] * inv[None, :]
    cos, sin = jnp.cos(ang), jnp.sin(ang)
    zero = jnp.zeros_like(sin)
    cos_t = jnp.concatenate([cos] * 4, axis=1)
    sa_t = jnp.concatenate([-sin, zero, -sin, zero], axis=1)
    sb_t = jnp.concatenate([zero, sin, zero, sin], axis=1)
    return cos_t, sa_t, sb_t


def _head_cols_index(head_order):
    return np.concatenate([np.arange(h * HEAD_DIM, (h + 1) * HEAD_DIM) for h in head_order])


_B_HEAD_ORDER = [e * (N_HEADS // B_KV_HEADS) + i for i in range(N_PAIRS) for e in range(2)]
_C_HEAD_ORDER = [C_GROUP * (2 * m + e) + i for m in range(C_KV_HEADS // 2) for i in range(C_GROUP) for e in range(2)]


def _mixer_a(xt, b, s, w_in, tabs):
    wd = N_HEADS * HEAD_DIM
    scale = HEAD_DIM ** -0.5
    outs, lses = [], []
    for gi, (window, dil) in enumerate(A_GROUPS):
        base = 3 * gi * wd
        w_qkv = jnp.concatenate([w_in[:, base:base + wd] * scale, w_in[:, base + wd:base + 3 * wd]], axis=1)
        qkv = _proj(xt, w_qkv.astype(BF16), b, BF16, tabs, n_rope_cols=2 * wd, dil=dil)
        o, lse = _banded_attention(
            qkv, qkv, qkv, tq=min(256, s // dil), npv=1, max_dist=window // dil,
            q_off=0, k_off=1, v_off=2, kv_pairs=N_PAIRS, kv_pair_of=lambda pi: pi, want_lse=True)
        if dil == 1:
            o, lse = o.reshape(b * s, wd), lse.reshape(b * s, LANES)
        outs.append(o)
        lses.append(lse)
    return outs, lses, w_in[:, -wd:]


def _mixer_b(xt, b, s, w_in, sinks, tabs):
    wd = N_HEADS * HEAD_DIM
    kvw = B_KV_HEADS * HEAD_DIM
    perm = _head_cols_index(_B_HEAD_ORDER)
    wq = w_in[:, :wd][:, perm] * HEAD_DIM ** -0.5
    wk = w_in[:, wd:wd + kvw]
    wv = w_in[:, wd + kvw:wd + 2 * kvw]
    wz = w_in[:, wd + 2 * kvw:][:, perm]
    qk = _proj(xt, jnp.concatenate([wq, wk], axis=1).astype(BF16), b, BF16, tabs, n_rope_cols=wd + kvw)
    v = _proj(xt, wv.astype(BF16), b, BF16)
    o = _banded_attention(
        qk, qk, v, tq=min(256, s), npv=1, max_dist=B_WINDOW - 1,
        q_off=0, k_off=wd // kvw, v_off=0, kv_pairs=1, kv_pair_of=lambda pi: 0,
        sinks=sinks[np.asarray(_B_HEAD_ORDER)].astype(F32))
    return [o.reshape(b * s, wd)], [], wz, perm


def _mixer_c(xt, b, s, w_in, w_ck, w_cv, pos, tabs):
    wd = N_HEADS * HEAD_DIM
    perm = _head_cols_index(_C_HEAD_ORDER)
    cols = np.cumsum([0, wd] + [C_KV] * 6 + [3 * N_HEADS, wd])
    part = lambda i: w_in[:, cols[i]:cols[i + 1]]
    wq = part(0)[:, perm] * HEAD_DIM ** -0.5
    w_gl = part(7).reshape(-1, 3, N_HEADS)[:, :, np.asarray(_C_HEAD_ORDER)].reshape(-1, 3 * N_HEADS)
    w_gl = jnp.pad(w_gl, ((0, 0), (0, LANES - 3 * N_HEADS)))
    w_att = jnp.concatenate([wq, part(3), part(5), part(4), part(6)], axis=1)
    att = _proj(xt, w_att.astype(BF16), b, BF16, tabs, n_rope_cols=wd + 2 * C_KV)
    cmp_in = _proj(xt, jnp.concatenate([part(1), part(2)], axis=1).astype(BF16), b, F32, tabs,
                   n_rope_cols=C_KV, tn=C_KV).reshape(b, s, 2 * C_KV)

    nc = s // C_CMP_STRIDE
    cw = C_CMP_STRIDE * HEAD_DIM

    def chunks(t):
        return jnp.transpose(t.reshape(b, s, C_KV_HEADS, HEAD_DIM), (0, 2, 1, 3)).reshape(b, C_KV_HEADS, nc, cw)

    pos2 = pos.reshape(2, cw)

    def compressed(t, w):
        c = _compress(chunks(t), pos2, w.reshape(2, cw, HEAD_DIM).astype(BF16))
        return jnp.transpose(c, (0, 2, 1, 3)).reshape(b, nc, C_KV)

    kcmp, vcmp = compressed(cmp_in[:, :, :C_KV], w_ck), compressed(cmp_in[:, :, C_KV:], w_cv)
    att3 = att.reshape(b, s, -1)
    qt = jnp.transpose(att3[:, :, :wd], (0, 2, 1))
    vs = att3[:, :, wd + 2 * C_KV:wd + 3 * C_KV]
    vst = jnp.transpose(vs.reshape(b, s // SEL_TILE, SEL_TILE, C_KV), (0, 1, 3, 2))
    o_cmp, o_slc = _nsa_select(att3, qt, kcmp, vcmp, att3, vst, ks_off=wd // C_KV)
    o_win = _banded_attention(
        att, att, att, tq=C_WINDOW, npv=C_WINDOW // QBLK, max_dist=C_WINDOW - 1,
        q_off=0, k_off=wd // C_KV + 1, v_off=wd // C_KV + 3, kv_pairs=C_KV_HEADS // 2,
        kv_pair_of=lambda pi: pi // C_GROUP)
    outs = [o.reshape(b * s, wd) for o in (o_cmp, o_slc, o_win)]
    return outs, [], (part(8)[:, perm], w_gl), perm


def _layer(i, xt, xb, b, s, tabs, p, a_w_in, a_w_out, b_w_in, b_sinks, b_w_out, c_w_in, c_w_ck, c_w_cv, c_pos,
           c_w_out, ln_g, ln_b, ple_w_proj, ple_w_gate):
    d = xt.shape[1]
    j, kind = divmod(i, N_MIXERS)
    w_gl = None
    if kind == 0:
        outs, aux, w_z = _mixer_a(xb, b, s, a_w_in[j], tabs)
        w_out, name = a_w_out[j], "A"
    elif kind == 1:
        outs, aux, w_z, perm = _mixer_b(xb, b, s, b_w_in[j], b_sinks[j], tabs)
        w_out, name = b_w_out[j][perm, :], "B"
    else:
        outs, aux, (w_z, w_gl), perm = _mixer_c(xb, b, s, c_w_in[j], c_w_ck[j], c_w_cv[j], c_pos[j], tabs)
        w_out, name = c_w_out[j][perm, :], "C"
        w_gl = w_gl.astype(BF16)
    return _post(name, outs, aux, xt, xb, p[i].reshape(b * s, -1), w_z.astype(BF16), w_gl,
                 w_out.astype(BF16), ln_g[i].reshape(1, d), ln_b[i].reshape(1, d),
                 ple_w_gate[i].astype(BF16), ple_w_proj[i].astype(BF16))


def kernel(x, p, a_w_in, a_w_out, b_w_in, b_sinks, b_w_out, c_w_in, c_w_ck, c_w_cv, c_pos, c_w_out,
           ln_g, ln_b, ple_w_proj, ple_w_gate):
    b, s, d = x.shape
    assert d == N_HEADS * HEAD_DIM and s % (QBLK * A_GROUPS[-1][1]) == 0 and s % C_WINDOW == 0
    tabs = _rope_tables(s)
    xt = x.reshape(b * s, d)
    xb = xt.astype(BF16)
    for i in range(DEPTH):
        xt, xb = _layer(i, xt, xb, b, s, tabs, p, a_w_in, a_w_out, b_w_in, b_sinks, b_w_out, c_w_in, c_w_ck,
                        c_w_cv, c_pos, c_w_out, ln_g, ln_b, ple_w_proj, ple_w_gate)
    return xt.reshape(b, s, d)
```

```python
import functools

import numpy as np
import jax
import jax.numpy as jnp
from jax import lax
from jax.experimental import pallas as pl
from jax.experimental.pallas import tpu as pltpu

F32 = jnp.float32
BF16 = jnp.bfloat16

LANES = 128
VMEM_LIMIT_BYTES = 56 * 1024 * 1024

HEAD_DIM = 64
HALF = HEAD_DIM // 2
PAIR = 2 * HEAD_DIM
assert PAIR == LANES
N_HEADS = 16
N_PAIRS = N_HEADS // 2
ROPE_THETA = 10000.0
QBLK = 128
BAND_GROUP = 4
LN_EPS = 1e-5
DEPTH = 4
N_MIXERS = 3
DEEPNORM_ALPHA = (2 * DEPTH) ** 0.25
A_GROUPS = ((128, 1), (512, 4), (2048, 16))
B_KV_HEADS = 2
B_WINDOW = 128
C_KV_HEADS = 4
C_GROUP = N_HEADS // C_KV_HEADS
C_KV = C_KV_HEADS * HEAD_DIM
C_CMP_STRIDE = 16
C_CMP_LEN = 32
C_SEL_LEN = 64
C_N_SEL = 16
C_WINDOW = 512
C_SEL_OVERLAP = (1.0, 2.0, 2.0, 2.0, 1.0)
SEL_PER_CMP = C_SEL_LEN // C_CMP_STRIDE
SEL_TILE = 512
LOG2E = 1.4426950408889634
NEG_INF = float("-inf")
MASKED = -1e30

_NT = (((1,), (1,)), ((), ()))


def _params(n_grid):
    return pltpu.CompilerParams(
        dimension_semantics=("arbitrary",) * n_grid, vmem_limit_bytes=VMEM_LIMIT_BYTES)


def _proj_kernel(*refs, n_rope_tiles, n_tiles, dil):
    it = iter(refs)
    x_ref, w_ref = next(it), next(it)
    if n_rope_tiles:
        cos_ref, sa_ref, sb_ref = next(it), next(it), next(it)
    o_ref = next(it)
    stage_ref = next(it) if dil > 1 else None
    tm, tn = x_ref.shape[0], w_ref.shape[1]

    def emit(rope):
        acc = jnp.dot(x_ref[...], w_ref[...], preferred_element_type=F32)
        if rope:
            c, sa, sb = cos_ref[...], sa_ref[...], sb_ref[...]
        for j in range(tn // LANES):
            cl = slice(j * LANES, (j + 1) * LANES)
            t = acc[:, cl]
            if rope:
                t = t * c + pltpu.roll(t, LANES - HALF, 1) * sa + pltpu.roll(t, HALF, 1) * sb
            if dil == 1:
                o_ref[0, :, cl] = t.astype(o_ref.dtype)
            else:
                stage_ref[j] = t
                for r in range(dil):
                    o_ref[r, :, cl] = stage_ref[j, pl.ds(r, tm // dil, stride=dil), :].astype(o_ref.dtype)

    if n_rope_tiles == 0 or n_rope_tiles == n_tiles:
        emit(n_rope_tiles > 0)
    else:
        pl.when(pl.program_id(1) < n_rope_tiles)(lambda: emit(True))
        pl.when(pl.program_id(1) >= n_rope_tiles)(lambda: emit(False))


def _pick_tile(n, candidates):
    for c in candidates:
        if n % c == 0:
            return c
    raise ValueError(f"no tile for {n}")


def _proj(x, w, batch, out_dtype, rope_tabs=None, n_rope_cols=0, dil=1, tn=None):
    t, k = x.shape
    n = w.shape[1]
    seq_len = t // batch
    tm = _pick_tile(seq_len, (1024, 512, 256, 128))
    tn = tn or _pick_tile(n, (512, 384, 256, 128))
    assert n % tn == 0 and n_rope_cols % tn == 0 and tm % (8 * dil) == 0
    n_seq_tiles = seq_len // tm
    in_specs = [pl.BlockSpec((tm, k), lambda i, j: (i, 0)),
                pl.BlockSpec((k, tn), lambda i, j: (0, j))]
    args = [x, w]
    if n_rope_cols:
        tab_spec = pl.BlockSpec((tm, LANES), lambda i, j: (i % n_seq_tiles, 0))
        in_specs += [tab_spec] * 3
        args += list(rope_tabs)
    scratch = []
    if dil > 1:
        scratch.append(pltpu.VMEM((tn // LANES, tm, LANES), F32))
    return pl.pallas_call(
        functools.partial(_proj_kernel, n_rope_tiles=n_rope_cols // tn, n_tiles=n // tn, dil=dil),
        grid=(t // tm, n // tn),
        in_specs=in_specs,
        out_specs=pl.BlockSpec((None, dil, tm // dil, tn),
                               lambda i, j: (i // n_seq_tiles, 0, i % n_seq_tiles, j)),
        out_shape=jax.ShapeDtypeStruct((batch, dil, seq_len // dil, n), out_dtype),
        scratch_shapes=scratch,
        compiler_params=_params(2),
        name=f"proj_d{dil}_r{n_rope_cols}",
    )(*args)


def _band_kernel(*refs, tq, npv, max_dist, kv_pair_of, has_sinks, want_lse):
    it = iter(refs)
    q_ref, kc_ref, kp_ref, vc_ref, vp_ref = (next(it) for _ in range(5))
    sink_ref = next(it) if has_sinks else None
    o_ref = next(it)
    lse_ref = next(it) if want_lse else None
    qb = pl.program_id(2)
    w = (npv + 1) * QBLK
    pv_rows = npv * QBLK
    lane = lax.broadcasted_iota(jnp.int32, (QBLK, LANES), 1)
    first_half = lane < HEAD_DIM
    qi = lax.broadcasted_iota(jnp.int32, (QBLK, w), 0)
    kj = lax.broadcasted_iota(jnp.int32, (QBLK, w), 1)
    dist = qi + npv * QBLK - kj
    band = (dist >= 0) & (dist <= max_dist)
    for sub in range(tq // QBLK):
        r0 = sub * QBLK
        kstart = qb * tq + r0 - npv * QBLK
        mask = band & (kj + kstart >= 0)
        lse_tile = jnp.zeros((QBLK, LANES), F32)
        for g0 in range(0, N_PAIRS, BAND_GROUP):
            staged = []
            for pi in range(g0, g0 + BAND_GROUP):
                cl = slice(kv_pair_of(pi) * LANES, (kv_pair_of(pi) + 1) * LANES)
                k_parts, v_parts = [], []
                if r0 < pv_rows:
                    k_parts.append(kp_ref[r0:pv_rows, cl])
                    v_parts.append(vp_ref[r0:pv_rows, cl])
                cs = max(r0 - pv_rows, 0)
                k_parts.append(kc_ref[cs:r0 + QBLK, cl])
                v_parts.append(vc_ref[cs:r0 + QBLK, cl])
                kwin = k_parts[0] if len(k_parts) == 1 else jnp.concatenate(k_parts, axis=0)
                vwin = v_parts[0] if len(v_parts) == 1 else jnp.concatenate(v_parts, axis=0)
                qp = q_ref[r0:r0 + QBLK, pi * LANES:(pi + 1) * LANES]
                for e in range(2):
                    qe = jnp.where(first_half if e == 0 else jnp.logical_not(first_half), qp, 0)
                    s = lax.dot_general(qe, kwin, _NT, preferred_element_type=F32)
                    staged.append((pi, e, jnp.where(mask, s, NEG_INF), vwin))
            outs = {}
            for pi, e, s, vwin in staged:
                m = jnp.max(s, axis=-1, keepdims=True)
                if has_sinks:
                    sk = sink_ref[2 * pi + e]
                    m = jnp.maximum(m, sk)
                p = jnp.exp2(s - m)
                l = jnp.sum(p, axis=-1, keepdims=True)
                if has_sinks:
                    l = l + jnp.exp2(sk - m)
                pv = jnp.dot(p.astype(BF16), vwin, preferred_element_type=F32)
                outs[(pi, e)] = pv / l
                if want_lse:
                    lse_tile = jnp.where(lane == 2 * pi + e, m + jnp.log2(l), lse_tile)
            for pi in range(g0, g0 + BAND_GROUP):
                o_ref[r0:r0 + QBLK, pi * LANES:(pi + 1) * LANES] = jnp.where(
                    first_half, outs[(pi, 0)], outs[(pi, 1)])
        if want_lse:
            lse_ref[r0:r0 + QBLK, :] = lse_tile


def _banded_attention(q_arr, k_arr, v_arr, *, tq, npv, max_dist, q_off, k_off, v_off,
                      kv_pairs, kv_pair_of, sinks=None, want_lse=False):
    b, dil, l, _ = q_arr.shape
    qw, kw = N_PAIRS * LANES, kv_pairs * LANES
    pv = npv * QBLK
    assert tq % pv == 0
    in_specs = [
        pl.BlockSpec((None, None, tq, qw), lambda bi, r, i: (bi, r, i, q_off)),
        pl.BlockSpec((None, None, tq, kw), lambda bi, r, i: (bi, r, i, k_off)),
        pl.BlockSpec((None, None, pv, kw), lambda bi, r, i: (bi, r, jnp.maximum(i * (tq // pv) - 1, 0), k_off)),
        pl.BlockSpec((None, None, tq, kw), lambda bi, r, i: (bi, r, i, v_off)),
        pl.BlockSpec((None, None, pv, kw), lambda bi, r, i: (bi, r, jnp.maximum(i * (tq // pv) - 1, 0), v_off)),
    ]
    args = [q_arr, k_arr, k_arr, v_arr, v_arr]
    if sinks is not None:
        in_specs.append(pl.BlockSpec(memory_space=pltpu.SMEM))
        args.append(sinks)
    out_specs = [pl.BlockSpec((None, None, tq, qw), lambda bi, r, i: (bi, r, i, 0))]
    out_shape = [jax.ShapeDtypeStruct((b, dil, l, qw), F32)]
    if want_lse:
        out_specs.append(pl.BlockSpec((None, None, tq, LANES), lambda bi, r, i: (bi, r, i, 0)))
        out_shape.append(jax.ShapeDtypeStruct((b, dil, l, LANES), F32))
    res = pl.pallas_call(
        functools.partial(_band_kernel, tq=tq, npv=npv, max_dist=max_dist, kv_pair_of=kv_pair_of,
                          has_sinks=sinks is not None, want_lse=want_lse),
        grid=(b, dil, l // tq),
        in_specs=in_specs,
        out_specs=out_specs,
        out_shape=out_shape,
        compiler_params=_params(3),
        name=f"band_d{dil}_w{max_dist}",
    )(*args)
    return res if want_lse else res[0]


def _compress_kernel(c_ref, pos_ref, w_ref, o_ref):
    c = c_ref[...]
    top = jnp.dot((c + pos_ref[0:1, :]).astype(BF16), w_ref[0], preferred_element_type=F32)
    bot = jnp.dot((c + pos_ref[1:2, :]).astype(BF16), w_ref[1], preferred_element_type=F32)
    nc = c.shape[0]
    o_ref[...] = (top + pltpu.roll(bot, nc - 1, 0)).astype(o_ref.dtype)


def _compress(chunks, pos, w):
    b, hk, nc, cw = chunks.shape
    return pl.pallas_call(
        _compress_kernel,
        grid=(b, hk),
        in_specs=[pl.BlockSpec((None, None, nc, cw), lambda bi, h: (bi, h, 0, 0)),
                  pl.BlockSpec((2, cw), lambda bi, h: (0, 0)),
                  pl.BlockSpec((2, cw, HEAD_DIM), lambda bi, h: (0, 0, 0))],
        out_specs=pl.BlockSpec((None, None, nc, HEAD_DIM), lambda bi, h: (bi, h, 0, 0)),
        out_shape=jax.ShapeDtypeStruct((b, hk, nc, HEAD_DIM), BF16),
        compiler_params=_params(2),
        name="nsa_compress",
    )(chunks, pos, w)


def _stack_group_queries(q_ref, mp, half):
    pairs = [C_GROUP * mp + i for i in range(C_GROUP)]
    return jnp.concatenate(
        [jnp.where(half, q_ref[:, pr * LANES:(pr + 1) * LANES], 0) for pr in pairs], axis=0)


def _store_group_heads(o_ref, val, mp, e, first_half):
    for i in range(C_GROUP):
        ol = slice((C_GROUP * mp + i) * LANES, (C_GROUP * mp + i + 1) * LANES)
        rows = slice(i * QBLK, (i + 1) * QBLK)
        if e == 0:
            o_ref[:, ol] = val[rows]
        else:
            o_ref[:, ol] = jnp.where(first_half, o_ref[:, ol], val[rows])


def _nsa_cmp_kernel(q_ref, kc_ref, vc_ref, ocmp_ref, sel_ref, impt_ref, *, n_sel):
    nc = kc_ref.shape[0]
    ns = nc // SEL_PER_CMP
    t0 = pl.program_id(1) * QBLK
    lane = lax.broadcasted_iota(jnp.int32, (QBLK, LANES), 1)
    first_half = lane < HEAD_DIM

    qi_c = lax.broadcasted_iota(jnp.int32, (QBLK, nc), 0)
    nn_c = lax.broadcasted_iota(jnp.int32, (QBLK, nc), 1)
    cvalid = nn_c * C_CMP_STRIDE + (C_CMP_LEN - 1) <= t0 + qi_c
    cvalid = jnp.concatenate([cvalid] * C_GROUP, axis=0)

    jj = lax.broadcasted_iota(jnp.int32, (ns, QBLK), 0)
    cur = (t0 + lax.broadcasted_iota(jnp.int32, (ns, QBLK), 1)) // C_SEL_LEN
    forced = (jj == 0) | (jj == cur) | (jj == cur - 1)
    bvalid = jj <= cur

    impt_ref[0:8, :] = jnp.zeros((8, QBLK), F32)

    for kh in range(C_KV_HEADS):
        mp, e = divmod(kh, 2)
        cl = slice(mp * LANES, (mp + 1) * LANES)
        half = first_half if e == 0 else jnp.logical_not(first_half)
        qst = _stack_group_queries(q_ref, mp, half)

        sc = lax.dot_general(qst, kc_ref[:, cl], _NT, preferred_element_type=F32)
        sc = jnp.where(cvalid, sc, NEG_INF)
        mx = jnp.max(sc, axis=-1, keepdims=True)
        mx = jnp.where(mx > NEG_INF, mx, 0.0)
        ee = jnp.exp2(sc - mx)
        pc = ee / jnp.maximum(jnp.sum(ee, axis=-1, keepdims=True), 1e-30)
        ocmp = jnp.dot(pc.astype(BF16), vc_ref[:, cl], preferred_element_type=F32)
        _store_group_heads(ocmp_ref, ocmp, mp, e, first_half)

        imp = pc[0:QBLK]
        for g in range(1, C_GROUP):
            imp = imp + pc[g * QBLK:(g + 1) * QBLK]
        for c in range(nc // QBLK):
            impt_ref[8 + c * QBLK:8 + (c + 1) * QBLK, :] = imp[:, c * QBLK:(c + 1) * QBLK].T
        imp_s = C_SEL_OVERLAP[0] * impt_ref[pl.ds(7, ns, stride=SEL_PER_CMP), :]
        for o_off in range(1, len(C_SEL_OVERLAP)):
            imp_s = imp_s + C_SEL_OVERLAP[o_off] * impt_ref[pl.ds(7 + o_off, ns, stride=SEL_PER_CMP), :]
        score = jnp.where(forced, 1e4, jnp.where(bvalid, imp_s, -1.0))
        selt = jnp.zeros((ns, QBLK), F32)
        for _ in range(n_sel):
            best = jnp.max(score, axis=0, keepdims=True)
            first = jnp.min(jnp.where(score == best, jj, ns), axis=0, keepdims=True)
            hit = jj == first
            selt = jnp.where(hit, 1.0, selt)
            score = jnp.where(hit, NEG_INF, score)
        sel_ref[kh, 0:ns, :] = selt
        if ns < LANES:
            sel_ref[kh, ns:LANES, :] = jnp.zeros((LANES - ns, QBLK), F32)


def _nsa_slc_kernel(qt_ref, sel_ref, ks_ref, vst_ref, oslc_ref, acc_ref):
    gq = C_GROUP * QBLK
    blocks_per_tile = SEL_TILE // C_SEL_LEN
    bias_rows = 16
    t0 = pl.program_id(1) * QBLK
    lane = lax.broadcasted_iota(jnp.int32, (QBLK, LANES), 1)
    first_half = lane < HEAD_DIM
    row = lax.broadcasted_iota(jnp.int32, (LANES, QBLK), 0)
    top_rows = row < HEAD_DIM
    n_full = t0 // SEL_TILE
    key_in_tile = lax.broadcasted_iota(jnp.int32, (SEL_TILE, gq), 0)
    query_pos = t0 + lax.broadcasted_iota(jnp.int32, (SEL_TILE, gq), 1) % QBLK
    block_of_key = lax.broadcasted_iota(jnp.int32, (SEL_TILE, LANES), 0) // C_SEL_LEN
    block_onehot = jnp.where(
        block_of_key == lax.broadcasted_iota(jnp.int32, (SEL_TILE, LANES), 1), 1.0, 0.0).astype(BF16)
    bias_pad = jnp.zeros((LANES - bias_rows, gq), BF16)

    qts = []
    for kh in range(C_KV_HEADS):
        mp, e = divmod(kh, 2)
        keep_rows = top_rows if e == 0 else jnp.logical_not(top_rows)
        qts.append(jnp.concatenate(
            [jnp.where(keep_rows, qt_ref[(C_GROUP * mp + i) * LANES:(C_GROUP * mp + i + 1) * LANES, :], 0)
             for i in range(C_GROUP)], axis=1))
    acc_ref[...] = jnp.zeros((C_KV_HEADS, LANES, gq), F32)

    def tile_step(kt, carry, diagonal):
        k0 = pl.multiple_of(kt * SEL_TILE, SEL_TILE)
        b0 = pl.multiple_of(kt * blocks_per_tile, blocks_per_tile)
        sts = []
        for kh in range(C_KV_HEADS):
            mp = kh // 2
            cl = slice(mp * LANES, (mp + 1) * LANES)
            keys = jnp.concatenate([ks_ref[pl.ds(k0, SEL_TILE), cl], block_onehot], axis=1)
            picked = sel_ref[kh, pl.ds(b0, blocks_per_tile), :]
            bias = jnp.concatenate([(picked - 1.0) * -MASKED] * C_GROUP, axis=1)
            bias = jnp.concatenate([bias, jnp.zeros((bias_rows - blocks_per_tile, gq), F32)], axis=0)
            queries = jnp.concatenate([qts[kh], bias.astype(BF16), bias_pad], axis=0)
            st = jnp.dot(keys, queries, preferred_element_type=F32)
            if diagonal:
                st = jnp.where(k0 + key_in_tile <= query_pos, st, MASKED)
            sts.append(st)
        new_carry = []
        for kh in range(C_KV_HEADS):
            mp = kh // 2
            cl = slice(mp * LANES, (mp + 1) * LANES)
            st = sts[kh]
            m_old = carry[kh]
            m_new = jnp.maximum(m_old, jnp.max(st, axis=0, keepdims=True))
            alpha = jnp.exp2(m_old - m_new)
            p = jnp.exp2(st - m_new)
            acc_ref[kh] = alpha * acc_ref[kh] + jnp.dot(
                vst_ref[kh % 2, kt, cl, :], p.astype(BF16), preferred_element_type=F32)
            new_carry += [m_new]
        return tuple(new_carry)

    init = (jnp.full((1, gq), NEG_INF, F32),) * C_KV_HEADS
    carry = lax.fori_loop(0, n_full, functools.partial(tile_step, diagonal=False), init)
    carry = lax.fori_loop(n_full, n_full + 1, functools.partial(tile_step, diagonal=True), carry)
    for mp in range(C_KV_HEADS // 2):
        ots = [acc_ref[2 * mp + e] / acc_ref[2 * mp + e, HEAD_DIM * (1 - e):HEAD_DIM * (1 - e) + 1, :]
               for e in range(2)]
        for i in range(C_GROUP):
            ol = slice((C_GROUP * mp + i) * LANES, (C_GROUP * mp + i + 1) * LANES)
            qs = slice(i * QBLK, (i + 1) * QBLK)
            oslc_ref[:, ol] = jnp.where(first_half, ots[0][:, qs].T, ots[1][:, qs].T)


def _nsa_select(q_arr, qt_arr, kcmp, vcmp, ks_arr, vst_arr, *, ks_off):
    b, s, _ = q_arr.shape
    nc = kcmp.shape[1]
    ns = nc // SEL_PER_CMP
    assert ns <= LANES and s % SEL_TILE == 0
    qw = N_PAIRS * LANES
    gq = C_GROUP * QBLK
    q_spec = pl.BlockSpec((None, QBLK, qw), lambda bi, i: (bi, i, 0))
    o_spec = pl.BlockSpec((None, QBLK, qw), lambda bi, i: (bi, i, 0))
    sel_spec = pl.BlockSpec((None, C_KV_HEADS, LANES, QBLK), lambda bi, i: (bi, 0, 0, i))
    o_shape = jax.ShapeDtypeStruct((b, s, qw), F32)
    o_cmp, sel = pl.pallas_call(
        functools.partial(_nsa_cmp_kernel, n_sel=min(C_N_SEL, ns)),
        grid=(b, s // QBLK),
        in_specs=[q_spec,
                  pl.BlockSpec((None, nc, C_KV), lambda bi, i: (bi, 0, 0)),
                  pl.BlockSpec((None, nc, C_KV), lambda bi, i: (bi, 0, 0))],
        out_specs=[o_spec, sel_spec],
        out_shape=[o_shape, jax.ShapeDtypeStruct((b, C_KV_HEADS, LANES, s), F32)],
        scratch_shapes=[pltpu.VMEM((8 + nc, QBLK), F32)],
        compiler_params=_params(2),
        name="nsa_compressed",
    )(q_arr, kcmp, vcmp)
    o_slc = pl.pallas_call(
        _nsa_slc_kernel,
        grid=(b, s // QBLK),
        in_specs=[pl.BlockSpec((None, qw, QBLK), lambda bi, i: (bi, 0, i)),
                  sel_spec,
                  pl.BlockSpec((None, s, C_KV), lambda bi, i: (bi, 0, ks_off)),
                  pl.BlockSpec((None, 2, s // SEL_TILE, C_KV, SEL_TILE), lambda bi, i: (bi, 0, 0, 0, 0))],
        out_specs=o_spec,
        out_shape=o_shape,
        scratch_shapes=[pltpu.VMEM((C_KV_HEADS, LANES, gq), F32)],
        compiler_params=_params(2),
        name="nsa_selected",
    )(qt_arr, sel, ks_arr, vst_arr)
    return o_cmp, o_slc


def _head_cols(tile, col, first_half):
    tm = tile.shape[0]
    a = jnp.broadcast_to(tile[:, col:col + 1], (tm, LANES))
    b = jnp.broadcast_to(tile[:, col + 1:col + 2], (tm, LANES))
    return jnp.where(first_half, a, b)


def _post_kernel(*refs, kind, n_staged):
    it = iter(refs)
    n_branch = {"A": 3, "B": 1, "C": 3}[kind]
    o_refs = [next(it) for _ in range(n_branch)]
    aux_refs = [next(it) for _ in range(3)] if kind == "A" else []
    x_ref, xb_ref, p_ref, wz_ref = (next(it) for _ in range(4))
    wgl_ref = next(it) if kind == "C" else None
    wo_ref, g_ref, b_ref, wg_ref, wp_ref, out_ref, outb_ref, u_ref, z_ref = (next(it) for _ in range(9))
    stage_refs = [next(it) for _ in range(n_staged)]
    tm = x_ref.shape[0]
    lane = lax.broadcasted_iota(jnp.int32, (tm, LANES), 1)
    first_half = lane < HEAD_DIM

    def token_order(ref):
        if len(ref.shape) == 2:
            return lambda j: ref[:, j * LANES:(j + 1) * LANES]
        dil = ref.shape[0]
        stage = stage_refs.pop(0)
        for j in range(ref.shape[2] // LANES):
            for r in range(dil):
                stage[j, pl.ds(r, tm // dil, stride=dil), :] = ref[r, :, j * LANES:(j + 1) * LANES]
        return lambda j: stage[j]

    o_cols = [token_order(r) for r in o_refs]
    aux_cols = [token_order(r) for r in aux_refs]

    xb = xb_ref[...]
    z_ref[...] = jnp.dot(xb, wz_ref[...], preferred_element_type=F32)
    if kind == "A":
        lses = [c(0) for c in aux_cols]
        mx = jnp.maximum(jnp.maximum(lses[0], lses[1]), lses[2])
        ws = [jnp.exp2(v - mx) for v in lses]
        den = ws[0] + ws[1] + ws[2]
        ws = [v / den for v in ws]
    elif kind == "C":
        gates = jax.nn.sigmoid(jnp.dot(xb, wgl_ref[...], preferred_element_type=F32))

    for pi in range(N_PAIRS):
        cl = slice(pi * LANES, (pi + 1) * LANES)
        if kind == "A":
            o = _head_cols(ws[0], 2 * pi, first_half) * o_cols[0](pi)
            for g in range(1, 3):
                o = o + _head_cols(ws[g], 2 * pi, first_half) * o_cols[g](pi)
        elif kind == "B":
            o = o_cols[0](pi)
        else:
            o = _head_cols(gates, 2 * pi, first_half) * o_cols[0](pi)
            for c in range(1, 3):
                o = o + _head_cols(gates, c * N_HEADS + 2 * pi, first_half) * o_cols[c](pi)
        z = z_ref[:, cl]
        u_ref[:, cl] = (o * (z * jax.nn.sigmoid(z))).astype(BF16)

    h = jnp.dot(u_ref[...], wo_ref[...], preferred_element_type=F32)
    y = DEEPNORM_ALPHA * x_ref[...] + h
    mu = jnp.mean(y, axis=-1, keepdims=True)
    yc = y - mu
    var = jnp.mean(yc * yc, axis=-1, keepdims=True)
    yn = yc * lax.rsqrt(var + LN_EPS) * g_ref[...] + b_ref[...]
    gate = jax.nn.sigmoid(jnp.dot(yn.astype(BF16), wg_ref[...], preferred_element_type=F32))
    pp = jnp.dot(p_ref[...].astype(BF16), wp_ref[...], preferred_element_type=F32)
    x_new = yn + gate * pp
    out_ref[...] = x_new
    outb_ref[...] = x_new.astype(BF16)


def _post(kind, o_list, aux_list, x, xb, p, w_z, w_gl, w_out, ln_g, ln_b, w_gate, w_proj):
    t, d = x.shape
    tm = 256
    row = lambda w: pl.BlockSpec((tm, w), lambda i: (i, 0))
    full = lambda a: pl.BlockSpec(a.shape, lambda i: (0,) * a.ndim)

    branch_specs, stages = [], []
    for a in list(o_list) + list(aux_list):
        if a.ndim == 2:
            branch_specs.append(row(a.shape[1]))
        else:
            _, dil, sub_len, w = a.shape
            n_seq_tiles = sub_len * dil // tm
            branch_specs.append(pl.BlockSpec(
                (None, dil, tm // dil, w), lambda i, n=n_seq_tiles: (i // n, 0, i % n, 0)))
            stages.append(pltpu.VMEM((w // LANES, tm, LANES), F32))
    weights = [w_z] + ([w_gl] if kind == "C" else []) + [w_out, ln_g, ln_b, w_gate, w_proj]
    args = list(o_list) + list(aux_list) + [x, xb, p] + weights
    in_specs = branch_specs + [row(d), row(d), row(p.shape[1])] + [full(w) for w in weights]
    return pl.pallas_call(
        functools.partial(_post_kernel, kind=kind, n_staged=len(stages)),
        grid=(t // tm,),
        in_specs=in_specs,
        out_specs=[row(d), row(d)],
        out_shape=[jax.ShapeDtypeStruct((t, d), F32), jax.ShapeDtypeStruct((t, d), BF16)],
        scratch_shapes=[pltpu.VMEM((tm, d), BF16), pltpu.VMEM((tm, d), F32)] + stages,
        compiler_params=_params(1),
        name=f"post_{kind}",
    )(*args)


def _rope_tables(seq_len):
    inv = 1.0 / (ROPE_THETA ** (jnp.arange(0, HEAD_DIM, 2, dtype=F32) / HEAD_DIM))
    ang = jnp.arange(seq_len, dtype=F32)[:, None] * inv[None, :]
    cos, sin = jnp.cos(ang), jnp.sin(ang)
    zero = jnp.zeros_like(sin)
    cos_t = jnp.concatenate([cos] * 4, axis=1)
    sa_t = jnp.concatenate([-sin, zero, -sin, zero], axis=1)
    sb_t = jnp.concatenate([zero, sin, zero, sin], axis=1)
    return cos_t, sa_t, sb_t


def _head_cols_index(head_order):
    return np.concatenate([np.arange(h * HEAD_DIM, (h + 1) * HEAD_DIM) for h in head_order])


_B_HEAD_ORDER = [e * (N_HEADS // B_KV_HEADS) + i for i in range(N_PAIRS) for e in range(2)]
_C_HEAD_ORDER = [C_GROUP * (2 * m + e) + i for m in range(C_KV_HEADS // 2) for i in range(C_GROUP) for e in range(2)]


def _mixer_a(xt, b, s, w_in, tabs):
    wd = N_HEADS * HEAD_DIM
    scale = HEAD_DIM ** -0.5 * LOG2E
    outs, lses = [], []
    for gi, (window, dil) in enumerate(A_GROUPS):
        base = 3 * gi * wd
        w_qkv = jnp.concatenate([w_in[:, base:base + wd] * scale, w_in[:, base + wd:base + 3 * wd]], axis=1)
        qkv = _proj(xt, w_qkv.astype(BF16), b, BF16, tabs, n_rope_cols=2 * wd, dil=dil)
        o, lse = _banded_attention(
            qkv, qkv, qkv, tq=min(256, s // dil), npv=1, max_dist=window // dil,
            q_off=0, k_off=1, v_off=2, kv_pairs=N_PAIRS, kv_pair_of=lambda pi: pi, want_lse=True)
        if dil == 1:
            o, lse = o.reshape(b * s, wd), lse.reshape(b * s, LANES)
        outs.append(o)
        lses.append(lse)
    return outs, lses, w_in[:, -wd:]


def _mixer_b(xt, b, s, w_in, sinks, tabs):
    wd = N_HEADS * HEAD_DIM
    kvw = B_KV_HEADS * HEAD_DIM
    perm = _head_cols_index(_B_HEAD_ORDER)
    wq = w_in[:, :wd][:, perm] * (HEAD_DIM ** -0.5 * LOG2E)
    wk = w_in[:, wd:wd + kvw]
    wv = w_in[:, wd + kvw:wd + 2 * kvw]
    wz = w_in[:, wd + 2 * kvw:][:, perm]
    qk = _proj(xt, jnp.concatenate([wq, wk], axis=1).astype(BF16), b, BF16, tabs, n_rope_cols=wd + kvw)
    v = _proj(xt, wv.astype(BF16), b, BF16)
    o = _banded_attention(
        qk, qk, v, tq=min(256, s), npv=1, max_dist=B_WINDOW - 1,
        q_off=0, k_off=wd // kvw, v_off=0, kv_pairs=1, kv_pair_of=lambda pi: 0,
        sinks=sinks[np.asarray(_B_HEAD_ORDER)].astype(F32) * LOG2E)
    return [o.reshape(b * s, wd)], [], wz, perm


def _mixer_c(xt, b, s, w_in, w_ck, w_cv, pos, tabs):
    wd = N_HEADS * HEAD_DIM
    perm = _head_cols_index(_C_HEAD_ORDER)
    cols = np.cumsum([0, wd] + [C_KV] * 6 + [3 * N_HEADS, wd])
    part = lambda i: w_in[:, cols[i]:cols[i + 1]]
    wq = part(0)[:, perm] * (HEAD_DIM ** -0.5 * LOG2E)
    w_gl = part(7).reshape(-1, 3, N_HEADS)[:, :, np.asarray(_C_HEAD_ORDER)].reshape(-1, 3 * N_HEADS)
    w_gl = jnp.pad(w_gl, ((0, 0), (0, LANES - 3 * N_HEADS)))
    w_att = jnp.concatenate([wq, part(3), part(5), part(4), part(6)], axis=1)
    att = _proj(xt, w_att.astype(BF16), b, BF16, tabs, n_rope_cols=wd + 2 * C_KV)
    cmp_in = _proj(xt, jnp.concatenate([part(1), part(2)], axis=1).astype(BF16), b, F32, tabs,
                   n_rope_cols=C_KV, tn=C_KV).reshape(b, s, 2 * C_KV)

    nc = s // C_CMP_STRIDE
    cw = C_CMP_STRIDE * HEAD_DIM

    def chunks(t):
        return jnp.transpose(t.reshape(b, s, C_KV_HEADS, HEAD_DIM), (0, 2, 1, 3)).reshape(b, C_KV_HEADS, nc, cw)

    pos2 = pos.reshape(2, cw)

    def compressed(t, w):
        c = _compress(chunks(t), pos2, w.reshape(2, cw, HEAD_DIM).astype(BF16))
        return jnp.transpose(c, (0, 2, 1, 3)).reshape(b, nc, C_KV)

    kcmp, vcmp = compressed(cmp_in[:, :, :C_KV], w_ck), compressed(cmp_in[:, :, C_KV:], w_cv)
    att3 = att.reshape(b, s, -1)
    qt = jnp.transpose(att3[:, :, :wd], (0, 2, 1))
    vs = att3[:, :, wd + 2 * C_KV:wd + 3 * C_KV]
    vst = jnp.transpose(vs.reshape(b, s // SEL_TILE, SEL_TILE, C_KV), (0, 1, 3, 2))
    in_half0 = (jnp.arange(C_KV) % PAIR < HEAD_DIM)[None, None, :, None]
    vst = jnp.stack([jnp.where(in_half0, vst, 1), jnp.where(in_half0, 1, vst)], axis=1)
    o_cmp, o_slc = _nsa_select(att3, qt, kcmp, vcmp, att3, vst, ks_off=wd // C_KV)
    o_win = _banded_attention(
        att, att, att, tq=C_WINDOW, npv=C_WINDOW // QBLK, max_dist=C_WINDOW - 1,
        q_off=0, k_off=wd // C_KV + 1, v_off=wd // C_KV + 3, kv_pairs=C_KV_HEADS // 2,
        kv_pair_of=lambda pi: pi // C_GROUP)
    outs = [o.reshape(b * s, wd) for o in (o_cmp, o_slc, o_win)]
    return outs, [], (part(8)[:, perm], w_gl), perm


def _layer(i, xt, xb, b, s, tabs, p, a_w_in, a_w_out, b_w_in, b_sinks, b_w_out, c_w_in, c_w_ck, c_w_cv, c_pos,
           c_w_out, ln_g, ln_b, ple_w_proj, ple_w_gate):
    d = xt.shape[1]
    j, kind = divmod(i, N_MIXERS)
    w_gl = None
    if kind == 0:
        outs, aux, w_z = _mixer_a(xb, b, s, a_w_in[j], tabs)
        w_out, name = a_w_out[j], "A"
    elif kind == 1:
        outs, aux, w_z, perm = _mixer_b(xb, b, s, b_w_in[j], b_sinks[j], tabs)
        w_out, name = b_w_out[j][perm, :], "B"
    else:
        outs, aux, (w_z, w_gl), perm = _mixer_c(xb, b, s, c_w_in[j], c_w_ck[j], c_w_cv[j], c_pos[j], tabs)
        w_out, name = c_w_out[j][perm, :], "C"
        w_gl = w_gl.astype(BF16)
    return _post(name, outs, aux, xt, xb, p[i].reshape(b * s, -1), w_z.astype(BF16), w_gl,
                 w_out.astype(BF16), ln_g[i].reshape(1, d), ln_b[i].reshape(1, d),
                 ple_w_gate[i].astype(BF16), ple_w_proj[i].astype(BF16))


def kernel(x, p, a_w_in, a_w_out, b_w_in, b_sinks, b_w_out, c_w_in, c_w_ck, c_w_cv, c_pos, c_w_out,
           ln_g, ln_b, ple_w_proj, ple_w_gate):
    b, s, d = x.shape
    assert d == N_HEADS * HEAD_DIM and s % (QBLK * A_GROUPS[-1][1]) == 0 and s % C_WINDOW == 0
    tabs = _rope_tables(s)
    xt = x.reshape(b * s, d)
    xb = xt.astype(BF16)
    for i in range(DEPTH):
        xt, xb = _layer(i, xt, xb, b, s, tabs, p, a_w_in, a_w_out, b_w_in, b_sinks, b_w_out, c_w_in, c_w_ck,
                        c_w_cv, c_pos, c_w_out, ln_g, ln_b, ple_w_proj, ple_w_gate)
    return xt.reshape(b, s, d)
```

```python
import functools

import numpy as np
import jax
import jax.numpy as jnp
from jax import lax
from jax.experimental import pallas as pl
from jax.experimental.pallas import tpu as pltpu

F32 = jnp.float32
BF16 = jnp.bfloat16

LANES = 128
VMEM_LIMIT_BYTES = 56 * 1024 * 1024

HEAD_DIM = 64
HALF = HEAD_DIM // 2
PAIR = 2 * HEAD_DIM
assert PAIR == LANES
N_HEADS = 16
N_PAIRS = N_HEADS // 2
ROPE_THETA = 10000.0
QBLK = 128
BAND_GROUP = 4
PROJ_ROWS = 128
POST_TILE = 512
POST_ROWS = 256
LN_EPS = 1e-5
DEPTH = 4
N_MIXERS = 3
DEEPNORM_ALPHA = (2 * DEPTH) ** 0.25
A_GROUPS = ((128, 1), (512, 4), (2048, 16))
B_KV_HEADS = 2
B_WINDOW = 128
C_KV_HEADS = 4
C_GROUP = N_HEADS // C_KV_HEADS
C_KV = C_KV_HEADS * HEAD_DIM
C_CMP_STRIDE = 16
C_CMP_LEN = 32
C_SEL_LEN = 64
C_N_SEL = 16
C_WINDOW = 512
C_SEL_OVERLAP = (1.0, 2.0, 2.0, 2.0, 1.0)
SEL_PER_CMP = C_SEL_LEN // C_CMP_STRIDE
SEL_TILE = 512
LOG2E = 1.4426950408889634
NEG_INF = float("-inf")
MASKED = -1e30

_NT = (((1,), (1,)), ((), ()))


def _params(n_grid):
    return pltpu.CompilerParams(
        dimension_semantics=("arbitrary",) * n_grid, vmem_limit_bytes=VMEM_LIMIT_BYTES)


def _proj_kernel(*refs, n_rope_tiles, n_tiles):
    it = iter(refs)
    x_ref, w_ref = next(it), next(it)
    if n_rope_tiles:
        tab_refs = [next(it), next(it), next(it)]
    o_ref = next(it)
    dil, sub, _ = x_ref.shape
    tn = w_ref.shape[1]
    n_chunks = dil * sub // PROJ_ROWS

    def chunk(ref, m, cols=slice(None)):
        if sub >= PROJ_ROWS:
            r, l0 = divmod(m * PROJ_ROWS, sub)
            return ref.at[r, l0:l0 + PROJ_ROWS, cols]
        k = PROJ_ROWS // sub
        return ref.at[m * k:(m + 1) * k, :, cols]

    def emit(rope):
        def matmul(m):
            xm = chunk(x_ref, m)[...].reshape(PROJ_ROWS, x_ref.shape[2])
            return jnp.dot(xm, w_ref[...], preferred_element_type=F32)

        def finish(m, acc):
            if rope:
                c, sa, sb = (chunk(t, m)[...].reshape(PROJ_ROWS, LANES) for t in tab_refs)
            for j in range(tn // LANES):
                cl = slice(j * LANES, (j + 1) * LANES)
                t = acc[:, cl]
                if rope:
                    t = t * c + pltpu.roll(t, LANES - HALF, 1) * sa + pltpu.roll(t, HALF, 1) * sb
                dst = chunk(o_ref, m, cl)
                dst[...] = t.astype(o_ref.dtype).reshape(dst.shape)

        acc = matmul(0)
        for m in range(n_chunks):
            nxt = matmul(m + 1) if m + 1 < n_chunks else None
            finish(m, acc)
            acc = nxt

    if n_rope_tiles == 0 or n_rope_tiles == n_tiles:
        emit(n_rope_tiles > 0)
    else:
        pl.when(pl.program_id(1) < n_rope_tiles)(lambda: emit(True))
        pl.when(pl.program_id(1) >= n_rope_tiles)(lambda: emit(False))


def _pick_tile(n, candidates):
    for c in candidates:
        if n % c == 0:
            return c
    raise ValueError(f"no tile for {n}")


def _proj(x, w, out_dtype, rope_tabs=None, n_rope_cols=0, tn=None):
    batch, dil, sub_len, k = x.shape
    n = w.shape[1]
    seq_len = dil * sub_len
    tm = _pick_tile(seq_len, (1024, 512, 256, 128))
    tn = tn or _pick_tile(n, (512, 384, 256, 128))
    sub = tm // dil
    assert n % tn == 0 and n_rope_cols % tn == 0 and tm % PROJ_ROWS == 0
    assert sub % 16 == 0 and (sub % PROJ_ROWS == 0 or PROJ_ROWS % sub == 0)
    n_seq_tiles = seq_len // tm
    in_specs = [pl.BlockSpec((None, dil, sub, k), lambda i, j: (i // n_seq_tiles, 0, i % n_seq_tiles, 0)),
                pl.BlockSpec((k, tn), lambda i, j: (0, j))]
    args = [x, w]
    if n_rope_cols:
        tab_spec = pl.BlockSpec((dil, sub, LANES), lambda i, j: (0, i % n_seq_tiles, 0))
        in_specs += [tab_spec] * 3
        args += list(rope_tabs)
    return pl.pallas_call(
        functools.partial(_proj_kernel, n_rope_tiles=n_rope_cols // tn, n_tiles=n // tn),
        grid=(batch * n_seq_tiles, n // tn),
        in_specs=in_specs,
        out_specs=pl.BlockSpec((None, dil, sub, tn), lambda i, j: (i // n_seq_tiles, 0, i % n_seq_tiles, j)),
        out_shape=jax.ShapeDtypeStruct((batch, dil, sub_len, n), out_dtype),
        compiler_params=_params(2),
        name=f"proj_d{dil}_r{n_rope_cols}",
    )(*args)


def _band_kernel(*refs, tq, npv, max_dist, kv_pair_of, has_sinks, want_lse):
    it = iter(refs)
    q_ref, kc_ref, kp_ref, vc_ref, vp_ref = (next(it) for _ in range(5))
    sink_ref = next(it) if has_sinks else None
    o_ref = next(it)
    lse_ref = next(it) if want_lse else None
    qb = pl.program_id(2)
    w = (npv + 1) * QBLK
    pv_rows = npv * QBLK
    lane = lax.broadcasted_iota(jnp.int32, (QBLK, LANES), 1)
    first_half = lane < HEAD_DIM
    qi = lax.broadcasted_iota(jnp.int32, (QBLK, w), 0)
    kj = lax.broadcasted_iota(jnp.int32, (QBLK, w), 1)
    dist = qi + npv * QBLK - kj
    band = (dist >= 0) & (dist <= max_dist)
    for sub in range(tq // QBLK):
        r0 = sub * QBLK
        kstart = qb * tq + r0 - npv * QBLK
        mask = band & (kj + kstart >= 0)
        lse_tile = jnp.zeros((QBLK, LANES), F32)
        for g0 in range(0, N_PAIRS, BAND_GROUP):
            staged = []
            for pi in range(g0, g0 + BAND_GROUP):
                cl = slice(kv_pair_of(pi) * LANES, (kv_pair_of(pi) + 1) * LANES)
                k_parts, v_parts = [], []
                if r0 < pv_rows:
                    k_parts.append(kp_ref[r0:pv_rows, cl])
                    v_parts.append(vp_ref[r0:pv_rows, cl])
                cs = max(r0 - pv_rows, 0)
                k_parts.append(kc_ref[cs:r0 + QBLK, cl])
                v_parts.append(vc_ref[cs:r0 + QBLK, cl])
                kwin = k_parts[0] if len(k_parts) == 1 else jnp.concatenate(k_parts, axis=0)
                vwin = v_parts[0] if len(v_parts) == 1 else jnp.concatenate(v_parts, axis=0)
                qp = q_ref[r0:r0 + QBLK, pi * LANES:(pi + 1) * LANES]
                for e in range(2):
                    qe = jnp.where(first_half if e == 0 else jnp.logical_not(first_half), qp, 0)
                    s = lax.dot_general(qe, kwin, _NT, preferred_element_type=F32)
                    staged.append((pi, e, jnp.where(mask, s, NEG_INF), vwin))
            outs = {}
            for pi, e, s, vwin in staged:
                m = jnp.max(s, axis=-1, keepdims=True)
                if has_sinks:
                    sk = sink_ref[2 * pi + e]
                    m = jnp.maximum(m, sk)
                p = jnp.exp2(s - m)
                l = jnp.sum(p, axis=-1, keepdims=True)
                if has_sinks:
                    l = l + jnp.exp2(sk - m)
                pv = jnp.dot(p.astype(BF16), vwin, preferred_element_type=F32)
                outs[(pi, e)] = pv / l
                if want_lse:
                    lse_tile = jnp.where(lane == 2 * pi + e, m + jnp.log2(l), lse_tile)
            for pi in range(g0, g0 + BAND_GROUP):
                o_ref[r0:r0 + QBLK, pi * LANES:(pi + 1) * LANES] = jnp.where(
                    first_half, outs[(pi, 0)], outs[(pi, 1)])
        if want_lse:
            lse_ref[r0:r0 + QBLK, :] = lse_tile


def _banded_attention(q_arr, k_arr, v_arr, *, tq, npv, max_dist, q_off, k_off, v_off,
                      kv_pairs, kv_pair_of, sinks=None, want_lse=False):
    b, dil, l, _ = q_arr.shape
    qw, kw = N_PAIRS * LANES, kv_pairs * LANES
    pv = npv * QBLK
    assert tq % pv == 0
    in_specs = [
        pl.BlockSpec((None, None, tq, qw), lambda bi, r, i: (bi, r, i, q_off)),
        pl.BlockSpec((None, None, tq, kw), lambda bi, r, i: (bi, r, i, k_off)),
        pl.BlockSpec((None, None, pv, kw), lambda bi, r, i: (bi, r, jnp.maximum(i * (tq // pv) - 1, 0), k_off)),
        pl.BlockSpec((None, None, tq, kw), lambda bi, r, i: (bi, r, i, v_off)),
        pl.BlockSpec((None, None, pv, kw), lambda bi, r, i: (bi, r, jnp.maximum(i * (tq // pv) - 1, 0), v_off)),
    ]
    args = [q_arr, k_arr, k_arr, v_arr, v_arr]
    if sinks is not None:
        in_specs.append(pl.BlockSpec(memory_space=pltpu.SMEM))
        args.append(sinks)
    out_specs = [pl.BlockSpec((None, None, tq, qw), lambda bi, r, i: (bi, r, i, 0))]
    out_shape = [jax.ShapeDtypeStruct((b, dil, l, qw), F32)]
    if want_lse:
        out_specs.append(pl.BlockSpec((None, None, tq, LANES), lambda bi, r, i: (bi, r, i, 0)))
        out_shape.append(jax.ShapeDtypeStruct((b, dil, l, LANES), F32))
    res = pl.pallas_call(
        functools.partial(_band_kernel, tq=tq, npv=npv, max_dist=max_dist, kv_pair_of=kv_pair_of,
                          has_sinks=sinks is not None, want_lse=want_lse),
        grid=(b, dil, l // tq),
        in_specs=in_specs,
        out_specs=out_specs,
        out_shape=out_shape,
        compiler_params=_params(3),
        name=f"band_d{dil}_w{max_dist}",
    )(*args)
    return res if want_lse else res[0]


def _compress_kernel(c_ref, pos_ref, w_ref, o_ref):
    c = c_ref[...]
    top = jnp.dot((c + pos_ref[0:1, :]).astype(BF16), w_ref[0], preferred_element_type=F32)
    bot = jnp.dot((c + pos_ref[1:2, :]).astype(BF16), w_ref[1], preferred_element_type=F32)
    nc = c.shape[0]
    o_ref[...] = (top + pltpu.roll(bot, nc - 1, 0)).astype(o_ref.dtype)


def _compress(chunks, pos, w):
    b, hk, nc, cw = chunks.shape
    return pl.pallas_call(
        _compress_kernel,
        grid=(b, hk),
        in_specs=[pl.BlockSpec((None, None, nc, cw), lambda bi, h: (bi, h, 0, 0)),
                  pl.BlockSpec((2, cw), lambda bi, h: (0, 0)),
                  pl.BlockSpec((2, cw, HEAD_DIM), lambda bi, h: (0, 0, 0))],
        out_specs=pl.BlockSpec((None, None, nc, HEAD_DIM), lambda bi, h: (bi, h, 0, 0)),
        out_shape=jax.ShapeDtypeStruct((b, hk, nc, HEAD_DIM), BF16),
        compiler_params=_params(2),
        name="nsa_compress",
    )(chunks, pos, w)


def _stack_group_queries(q_ref, mp, half):
    pairs = [C_GROUP * mp + i for i in range(C_GROUP)]
    return jnp.concatenate(
        [jnp.where(half, q_ref[:, pr * LANES:(pr + 1) * LANES], 0) for pr in pairs], axis=0)


def _store_group_heads(o_ref, val, mp, e, first_half):
    for i in range(C_GROUP):
        ol = slice((C_GROUP * mp + i) * LANES, (C_GROUP * mp + i + 1) * LANES)
        rows = slice(i * QBLK, (i + 1) * QBLK)
        if e == 0:
            o_ref[:, ol] = val[rows]
        else:
            o_ref[:, ol] = jnp.where(first_half, o_ref[:, ol], val[rows])


def _nsa_cmp_kernel(q_ref, kc_ref, vc_ref, ocmp_ref, sel_ref, impt_ref, *, n_sel):
    nc = kc_ref.shape[0]
    ns = nc // SEL_PER_CMP
    t0 = pl.program_id(1) * QBLK
    lane = lax.broadcasted_iota(jnp.int32, (QBLK, LANES), 1)
    first_half = lane < HEAD_DIM

    qi_c = lax.broadcasted_iota(jnp.int32, (QBLK, nc), 0)
    nn_c = lax.broadcasted_iota(jnp.int32, (QBLK, nc), 1)
    cvalid = nn_c * C_CMP_STRIDE + (C_CMP_LEN - 1) <= t0 + qi_c
    cvalid = jnp.concatenate([cvalid] * C_GROUP, axis=0)

    jj = lax.broadcasted_iota(jnp.int32, (ns, QBLK), 0)
    cur = (t0 + lax.broadcasted_iota(jnp.int32, (ns, QBLK), 1)) // C_SEL_LEN
    forced = (jj == 0) | (jj == cur) | (jj == cur - 1)
    bvalid = jj <= cur

    impt_ref[0:8, :] = jnp.zeros((8, QBLK), F32)

    for kh in range(C_KV_HEADS):
        mp, e = divmod(kh, 2)
        cl = slice(mp * LANES, (mp + 1) * LANES)
        half = first_half if e == 0 else jnp.logical_not(first_half)
        qst = _stack_group_queries(q_ref, mp, half)

        sc = lax.dot_general(qst, kc_ref[:, cl], _NT, preferred_element_type=F32)
        sc = jnp.where(cvalid, sc, NEG_INF)
        mx = jnp.max(sc, axis=-1, keepdims=True)
        mx = jnp.where(mx > NEG_INF, mx, 0.0)
        ee = jnp.exp2(sc - mx)
        pc = ee / jnp.maximum(jnp.sum(ee, axis=-1, keepdims=True), 1e-30)
        ocmp = jnp.dot(pc.astype(BF16), vc_ref[:, cl], preferred_element_type=F32)
        _store_group_heads(ocmp_ref, ocmp, mp, e, first_half)

        imp = pc[0:QBLK]
        for g in range(1, C_GROUP):
            imp = imp + pc[g * QBLK:(g + 1) * QBLK]
        for c in range(nc // QBLK):
            impt_ref[8 + c * QBLK:8 + (c + 1) * QBLK, :] = imp[:, c * QBLK:(c + 1) * QBLK].T
        imp_s = C_SEL_OVERLAP[0] * impt_ref[pl.ds(7, ns, stride=SEL_PER_CMP), :]
        for o_off in range(1, len(C_SEL_OVERLAP)):
            imp_s = imp_s + C_SEL_OVERLAP[o_off] * impt_ref[pl.ds(7 + o_off, ns, stride=SEL_PER_CMP), :]
        score = jnp.where(forced, 1e4, jnp.where(bvalid, imp_s, -1.0))
        selt = jnp.zeros((ns, QBLK), F32)
        for _ in range(n_sel):
            best = jnp.max(score, axis=0, keepdims=True)
            first = jnp.min(jnp.where(score == best, jj, ns), axis=0, keepdims=True)
            hit = jj == first
            selt = jnp.where(hit, 1.0, selt)
            score = jnp.where(hit, NEG_INF, score)
        sel_ref[kh, 0:ns, :] = selt
        if ns < LANES:
            sel_ref[kh, ns:LANES, :] = jnp.zeros((LANES - ns, QBLK), F32)


def _nsa_slc_kernel(qt_ref, sel_ref, ks_ref, vst_ref, oslc_ref, acc_ref):
    gq = C_GROUP * QBLK
    blocks_per_tile = SEL_TILE // C_SEL_LEN
    bias_rows = 16
    t0 = pl.program_id(1) * QBLK
    lane = lax.broadcasted_iota(jnp.int32, (QBLK, LANES), 1)
    first_half = lane < HEAD_DIM
    row = lax.broadcasted_iota(jnp.int32, (LANES, QBLK), 0)
    top_rows = row < HEAD_DIM
    n_full = t0 // SEL_TILE
    key_in_tile = lax.broadcasted_iota(jnp.int32, (SEL_TILE, gq), 0)
    query_pos = t0 + lax.broadcasted_iota(jnp.int32, (SEL_TILE, gq), 1) % QBLK
    block_of_key = lax.broadcasted_iota(jnp.int32, (SEL_TILE, LANES), 0) // C_SEL_LEN
    block_onehot = jnp.where(
        block_of_key == lax.broadcasted_iota(jnp.int32, (SEL_TILE, LANES), 1), 1.0, 0.0).astype(BF16)
    bias_pad = jnp.zeros((LANES - bias_rows, gq), BF16)

    qts = []
    for kh in range(C_KV_HEADS):
        mp, e = divmod(kh, 2)
        keep_rows = top_rows if e == 0 else jnp.logical_not(top_rows)
        qts.append(jnp.concatenate(
            [jnp.where(keep_rows, qt_ref[(C_GROUP * mp + i) * LANES:(C_GROUP * mp + i + 1) * LANES, :], 0)
             for i in range(C_GROUP)], axis=1))
    acc_ref[...] = jnp.zeros((C_KV_HEADS, LANES, gq), F32)

    def tile_step(kt, carry, diagonal):
        k0 = pl.multiple_of(kt * SEL_TILE, SEL_TILE)
        b0 = pl.multiple_of(kt * blocks_per_tile, blocks_per_tile)
        sts = []
        for kh in range(C_KV_HEADS):
            mp = kh // 2
            cl = slice(mp * LANES, (mp + 1) * LANES)
            keys = jnp.concatenate([ks_ref[pl.ds(k0, SEL_TILE), cl], block_onehot], axis=1)
            picked = sel_ref[kh, pl.ds(b0, blocks_per_tile), :]
            bias = jnp.concatenate([(picked - 1.0) * -MASKED] * C_GROUP, axis=1)
            bias = jnp.concatenate([bias, jnp.zeros((bias_rows - blocks_per_tile, gq), F32)], axis=0)
            queries = jnp.concatenate([qts[kh], bias.astype(BF16), bias_pad], axis=0)
            st = jnp.dot(keys, queries, preferred_element_type=F32)
            if diagonal:
                st = jnp.where(k0 + key_in_tile <= query_pos, st, MASKED)
            sts.append(st)
        new_carry = []
        for kh in range(C_KV_HEADS):
            mp = kh // 2
            cl = slice(mp * LANES, (mp + 1) * LANES)
            st = sts[kh]
            m_old = carry[kh]
            m_new = jnp.maximum(m_old, jnp.max(st, axis=0, keepdims=True))
            alpha = jnp.exp2(m_old - m_new)
            p = jnp.exp2(st - m_new)
            acc_ref[kh] = alpha * acc_ref[kh] + jnp.dot(
                vst_ref[kh % 2, kt, cl, :], p.astype(BF16), preferred_element_type=F32)
            new_carry += [m_new]
        return tuple(new_carry)

    init = (jnp.full((1, gq), NEG_INF, F32),) * C_KV_HEADS
    carry = lax.fori_loop(0, n_full, functools.partial(tile_step, diagonal=False), init)
    carry = lax.fori_loop(n_full, n_full + 1, functools.partial(tile_step, diagonal=True), carry)
    for mp in range(C_KV_HEADS // 2):
        ots = [acc_ref[2 * mp + e] / acc_ref[2 * mp + e, HEAD_DIM * (1 - e):HEAD_DIM * (1 - e) + 1, :]
               for e in range(2)]
        for i in range(C_GROUP):
            ol = slice((C_GROUP * mp + i) * LANES, (C_GROUP * mp + i + 1) * LANES)
            qs = slice(i * QBLK, (i + 1) * QBLK)
            oslc_ref[:, ol] = jnp.where(first_half, ots[0][:, qs].T, ots[1][:, qs].T)


def _nsa_select(q_arr, qt_arr, kcmp, vcmp, ks_arr, vst_arr, *, ks_off):
    b, s, _ = q_arr.shape
    nc = kcmp.shape[1]
    ns = nc // SEL_PER_CMP
    assert ns <= LANES and s % SEL_TILE == 0
    qw = N_PAIRS * LANES
    gq = C_GROUP * QBLK
    q_spec = pl.BlockSpec((None, QBLK, qw), lambda bi, i: (bi, i, 0))
    o_spec = pl.BlockSpec((None, QBLK, qw), lambda bi, i: (bi, i, 0))
    sel_spec = pl.BlockSpec((None, C_KV_HEADS, LANES, QBLK), lambda bi, i: (bi, 0, 0, i))
    o_shape = jax.ShapeDtypeStruct((b, s, qw), F32)
    o_cmp, sel = pl.pallas_call(
        functools.partial(_nsa_cmp_kernel, n_sel=min(C_N_SEL, ns)),
        grid=(b, s // QBLK),
        in_specs=[q_spec,
                  pl.BlockSpec((None, nc, C_KV), lambda bi, i: (bi, 0, 0)),
                  pl.BlockSpec((None, nc, C_KV), lambda bi, i: (bi, 0, 0))],
        out_specs=[o_spec, sel_spec],
        out_shape=[o_shape, jax.ShapeDtypeStruct((b, C_KV_HEADS, LANES, s), F32)],
        scratch_shapes=[pltpu.VMEM((8 + nc, QBLK), F32)],
        compiler_params=_params(2),
        name="nsa_compressed",
    )(q_arr, kcmp, vcmp)
    o_slc = pl.pallas_call(
        _nsa_slc_kernel,
        grid=(b, s // QBLK),
        in_specs=[pl.BlockSpec((None, qw, QBLK), lambda bi, i: (bi, 0, i)),
                  sel_spec,
                  pl.BlockSpec((None, s, C_KV), lambda bi, i: (bi, 0, ks_off)),
                  pl.BlockSpec((None, 2, s // SEL_TILE, C_KV, SEL_TILE), lambda bi, i: (bi, 0, 0, 0, 0))],
        out_specs=o_spec,
        out_shape=o_shape,
        scratch_shapes=[pltpu.VMEM((C_KV_HEADS, LANES, gq), F32)],
        compiler_params=_params(2),
        name="nsa_selected",
    )(qt_arr, sel, ks_arr, vst_arr)
    return o_cmp, o_slc


def _head_cols(tile, col, first_half):
    tm = tile.shape[0]
    a = jnp.broadcast_to(tile[:, col:col + 1], (tm, LANES))
    b = jnp.broadcast_to(tile[:, col + 1:col + 2], (tm, LANES))
    return jnp.where(first_half, a, b)


def _post_kernel(*refs, kind, n_staged, n_dil_outs):
    it = iter(refs)
    n_branch = {"A": 3, "B": 1, "C": 3}[kind]
    o_refs = [next(it) for _ in range(n_branch)]
    aux_refs = [next(it) for _ in range(3)] if kind == "A" else []
    x_ref, xb_ref, p_ref, wz_ref = (next(it) for _ in range(4))
    wgl_ref = next(it) if kind == "C" else None
    wo_ref, g_ref, b_ref, wg_ref, wp_ref, out_ref, outb_ref = (next(it) for _ in range(7))
    dil_out_refs = [next(it) for _ in range(n_dil_outs)]
    u_ref, z_ref = next(it), next(it)
    stage_refs = [next(it) for _ in range(n_staged + (1 if n_dil_outs else 0))]
    tm = x_ref.shape[0]

    def token_order(ref):
        if len(ref.shape) == 2:
            return lambda j: ref[:, j * LANES:(j + 1) * LANES]
        dil = ref.shape[0]
        stage = stage_refs.pop(0)
        for j in range(ref.shape[2] // LANES):
            for r in range(dil):
                stage[j, pl.ds(r, tm // dil, stride=dil), :] = ref[r, :, j * LANES:(j + 1) * LANES]
        return lambda j: stage[j]

    o_cols = [token_order(r) for r in o_refs]
    aux_cols = [token_order(r) for r in aux_refs]

    n_chunks = tm // POST_ROWS
    rows = [slice(c * POST_ROWS, (c + 1) * POST_ROWS) for c in range(n_chunks)]
    first_half = lax.broadcasted_iota(jnp.int32, (POST_ROWS, LANES), 1) < HEAD_DIM
    gates = [None] * n_chunks
    for c, rs in enumerate(rows):
        xb = xb_ref[rs, :]
        z_ref[rs, :] = jnp.dot(xb, wz_ref[...], preferred_element_type=F32)
        if kind == "C":
            gates[c] = jax.nn.sigmoid(jnp.dot(xb, wgl_ref[...], preferred_element_type=F32))

    hs = []
    for c, rs in enumerate(rows):
        if kind == "A":
            lses = [col(0)[rs] for col in aux_cols]
            mx = jnp.maximum(jnp.maximum(lses[0], lses[1]), lses[2])
            ws = [jnp.exp2(v - mx) for v in lses]
            den = ws[0] + ws[1] + ws[2]
            ws = [v / den for v in ws]
        for pi in range(N_PAIRS):
            cl = slice(pi * LANES, (pi + 1) * LANES)
            if kind == "A":
                o = _head_cols(ws[0], 2 * pi, first_half) * o_cols[0](pi)[rs]
                for g in range(1, 3):
                    o = o + _head_cols(ws[g], 2 * pi, first_half) * o_cols[g](pi)[rs]
            elif kind == "B":
                o = o_cols[0](pi)[rs]
            else:
                o = _head_cols(gates[c], 2 * pi, first_half) * o_cols[0](pi)[rs]
                for br in range(1, 3):
                    o = o + _head_cols(gates[c], br * N_HEADS + 2 * pi, first_half) * o_cols[br](pi)[rs]
            z = z_ref[rs, cl]
            u_ref[rs, cl] = (o * (z * jax.nn.sigmoid(z))).astype(BF16)
        hs.append(jnp.dot(u_ref[rs, :], wo_ref[...], preferred_element_type=F32))

    pre = []
    for c, rs in enumerate(rows):
        y = DEEPNORM_ALPHA * x_ref[rs, :] + hs[c]
        mu = jnp.mean(y, axis=-1, keepdims=True)
        yc = y - mu
        var = jnp.mean(yc * yc, axis=-1, keepdims=True)
        yn = yc * lax.rsqrt(var + LN_EPS) * g_ref[...] + b_ref[...]
        gate_logits = jnp.dot(yn.astype(BF16), wg_ref[...], preferred_element_type=F32)
        pp = jnp.dot(p_ref[rs, :].astype(BF16), wp_ref[...], preferred_element_type=F32)
        pre.append((yn, gate_logits, pp))

    for c, rs in enumerate(rows):
        yn, gate_logits, pp = pre[c]
        x_new = yn + jax.nn.sigmoid(gate_logits) * pp
        out_ref[rs, :] = x_new
        outb_ref[rs, :] = x_new.astype(BF16)
        if dil_out_refs:
            for j in range(x_new.shape[1] // LANES):
                stage_refs[-1][j, rs, :] = x_new[:, j * LANES:(j + 1) * LANES]
    if dil_out_refs:
        stage = stage_refs[-1]
        for o_ref in dil_out_refs:
            dil = o_ref.shape[0]
            for j in range(x_new.shape[1] // LANES):
                for r in range(dil):
                    o_ref[r, :, j * LANES:(j + 1) * LANES] = stage[j, pl.ds(r, tm // dil, stride=dil), :].astype(BF16)


def _post(kind, o_list, aux_list, x, xb, p, w_z, w_gl, w_out, ln_g, ln_b, w_gate, w_proj, batch, out_dils=()):
    t, d = x.shape
    tm = POST_TILE
    row = lambda w: pl.BlockSpec((tm, w), lambda i: (i, 0))
    full = lambda a: pl.BlockSpec(a.shape, lambda i: (0,) * a.ndim)

    branch_specs, stages = [], []
    for a in list(o_list) + list(aux_list):
        if a.ndim == 2:
            branch_specs.append(row(a.shape[1]))
        else:
            _, dil, sub_len, w = a.shape
            n_seq_tiles = sub_len * dil // tm
            branch_specs.append(pl.BlockSpec(
                (None, dil, tm // dil, w), lambda i, n=n_seq_tiles: (i // n, 0, i % n, 0)))
            stages.append(pltpu.VMEM((w // LANES, tm, LANES), F32))
    n_staged = len(stages)
    if out_dils:
        stages.append(pltpu.VMEM((d // LANES, tm, LANES), F32))
    seq_tiles = t // batch // tm
    weights = [w_z] + ([w_gl] if kind == "C" else []) + [w_out, ln_g, ln_b, w_gate, w_proj]
    args = list(o_list) + list(aux_list) + [x, xb, p] + weights
    in_specs = branch_specs + [row(d), row(d), row(p.shape[1])] + [full(w) for w in weights]
    return pl.pallas_call(
        functools.partial(_post_kernel, kind=kind, n_staged=n_staged, n_dil_outs=len(out_dils)),
        grid=(t // tm,),
        in_specs=in_specs,
        out_specs=[row(d), row(d)] + [
            pl.BlockSpec((None, dil, tm // dil, d), lambda i, n=seq_tiles: (i // n, 0, i % n, 0)) for dil in out_dils],
        out_shape=[jax.ShapeDtypeStruct((t, d), F32), jax.ShapeDtypeStruct((t, d), BF16)] + [
            jax.ShapeDtypeStruct((batch, dil, t // batch // dil, d), BF16) for dil in out_dils],
        scratch_shapes=[pltpu.VMEM((tm, d), BF16), pltpu.VMEM((tm, d), F32)] + stages,
        compiler_params=_params(1),
        name=f"post_{kind}",
    )(*args)


def _rope_tables(seq_len):
    inv = 1.0 / (ROPE_THETA ** (jnp.arange(0, HEAD_DIM, 2, dtype=F32) / HEAD_DIM))
    ang = jnp.arange(seq_len, dtype=F32)[:, None] * inv[None, :]
    cos, sin = jnp.cos(ang), jnp.sin(ang)
    zero = jnp.zeros_like(sin)
    cos_t = jnp.concatenate([cos] * 4, axis=1)
    sa_t = jnp.concatenate([-sin, zero, -sin, zero], axis=1)
    sb_t = jnp.concatenate([zero, sin, zero, sin], axis=1)
    return cos_t, sa_t, sb_t


def _residue_major(a, dil):
    *lead, s, w = a.shape
    return jnp.swapaxes(a.reshape(*lead, s // dil, dil, w), -3, -2)


def _head_cols_index(head_order):
    return np.concatenate([np.arange(h * HEAD_DIM, (h + 1) * HEAD_DIM) for h in head_order])


_B_HEAD_ORDER = [e * (N_HEADS // B_KV_HEADS) + i for i in range(N_PAIRS) for e in range(2)]
_C_HEAD_ORDER = [C_GROUP * (2 * m + e) + i for m in range(C_KV_HEADS // 2) for i in range(C_GROUP) for e in range(2)]


def _mixer_a(xbs, b, s, w_in, tabs):
    wd = N_HEADS * HEAD_DIM
    scale = HEAD_DIM ** -0.5 * LOG2E
    outs, lses = [], []
    for gi, (window, dil) in enumerate(A_GROUPS):
        base = 3 * gi * wd
        w_qkv = jnp.concatenate([w_in[:, base:base + wd] * scale, w_in[:, base + wd:base + 3 * wd]], axis=1)
        qkv = _proj(xbs[dil], w_qkv.astype(BF16), BF16, [_residue_major(t, dil) for t in tabs],
                    n_rope_cols=2 * wd)
        o, lse = _banded_attention(
            qkv, qkv, qkv, tq=min(256, s // dil), npv=1, max_dist=window // dil,
            q_off=0, k_off=1, v_off=2, kv_pairs=N_PAIRS, kv_pair_of=lambda pi: pi, want_lse=True)
        if dil == 1:
            o, lse = o.reshape(b * s, wd), lse.reshape(b * s, LANES)
        outs.append(o)
        lses.append(lse)
    return outs, lses, w_in[:, -wd:]


def _mixer_b(xbs, b, s, w_in, sinks, tabs):
    wd = N_HEADS * HEAD_DIM
    kvw = B_KV_HEADS * HEAD_DIM
    perm = _head_cols_index(_B_HEAD_ORDER)
    wq = w_in[:, :wd][:, perm] * (HEAD_DIM ** -0.5 * LOG2E)
    wk = w_in[:, wd:wd + kvw]
    wv = w_in[:, wd + kvw:wd + 2 * kvw]
    wz = w_in[:, wd + 2 * kvw:][:, perm]
    tabs1 = [t[None] for t in tabs]
    qk = _proj(xbs[1], jnp.concatenate([wq, wk], axis=1).astype(BF16), BF16, tabs1, n_rope_cols=wd + kvw)
    v = _proj(xbs[1], wv.astype(BF16), BF16)
    o = _banded_attention(
        qk, qk, v, tq=min(256, s), npv=1, max_dist=B_WINDOW - 1,
        q_off=0, k_off=wd // kvw, v_off=0, kv_pairs=1, kv_pair_of=lambda pi: 0,
        sinks=sinks[np.asarray(_B_HEAD_ORDER)].astype(F32) * LOG2E)
    return [o.reshape(b * s, wd)], [], wz, perm


def _mixer_c(xbs, b, s, w_in, w_ck, w_cv, pos, tabs):
    wd = N_HEADS * HEAD_DIM
    perm = _head_cols_index(_C_HEAD_ORDER)
    cols = np.cumsum([0, wd] + [C_KV] * 6 + [3 * N_HEADS, wd])
    part = lambda i: w_in[:, cols[i]:cols[i + 1]]
    wq = part(0)[:, perm] * (HEAD_DIM ** -0.5 * LOG2E)
    w_gl = part(7).reshape(-1, 3, N_HEADS)[:, :, np.asarray(_C_HEAD_ORDER)].reshape(-1, 3 * N_HEADS)
    w_gl = jnp.pad(w_gl, ((0, 0), (0, LANES - 3 * N_HEADS)))
    w_att = jnp.concatenate([wq, part(3), part(5), part(4), part(6)], axis=1)
    tabs1 = [t[None] for t in tabs]
    att = _proj(xbs[1], w_att.astype(BF16), BF16, tabs1, n_rope_cols=wd + 2 * C_KV)
    cmp_in = _proj(xbs[1], jnp.concatenate([part(1), part(2)], axis=1).astype(BF16), F32, tabs1,
                   n_rope_cols=C_KV, tn=C_KV).reshape(b, s, 2 * C_KV)

    nc = s // C_CMP_STRIDE
    cw = C_CMP_STRIDE * HEAD_DIM

    def chunks(t):
        return jnp.transpose(t.reshape(b, s, C_KV_HEADS, HEAD_DIM), (0, 2, 1, 3)).reshape(b, C_KV_HEADS, nc, cw)

    pos2 = pos.reshape(2, cw)

    def compressed(t, w):
        c = _compress(chunks(t), pos2, w.reshape(2, cw, HEAD_DIM).astype(BF16))
        return jnp.transpose(c, (0, 2, 1, 3)).reshape(b, nc, C_KV)

    kcmp, vcmp = compressed(cmp_in[:, :, :C_KV], w_ck), compressed(cmp_in[:, :, C_KV:], w_cv)
    att3 = att.reshape(b, s, -1)
    qt = jnp.transpose(att3[:, :, :wd], (0, 2, 1))
    vs = att3[:, :, wd + 2 * C_KV:wd + 3 * C_KV]
    vst = jnp.transpose(vs.reshape(b, s // SEL_TILE, SEL_TILE, C_KV), (0, 1, 3, 2))
    in_half0 = (jnp.arange(C_KV) % PAIR < HEAD_DIM)[None, None, :, None]
    vst = jnp.stack([jnp.where(in_half0, vst, 1), jnp.where(in_half0, 1, vst)], axis=1)
    o_cmp, o_slc = _nsa_select(att3, qt, kcmp, vcmp, att3, vst, ks_off=wd // C_KV)
    o_win = _banded_attention(
        att, att, att, tq=C_WINDOW, npv=C_WINDOW // QBLK, max_dist=C_WINDOW - 1,
        q_off=0, k_off=wd // C_KV + 1, v_off=wd // C_KV + 3, kv_pairs=C_KV_HEADS // 2,
        kv_pair_of=lambda pi: pi // C_GROUP)
    outs = [o.reshape(b * s, wd) for o in (o_cmp, o_slc, o_win)]
    return outs, [], (part(8)[:, perm], w_gl), perm


def _layer(i, xt, xbs, b, s, tabs, p, a_w_in, a_w_out, b_w_in, b_sinks, b_w_out, c_w_in, c_w_ck, c_w_cv, c_pos,
           c_w_out, ln_g, ln_b, ple_w_proj, ple_w_gate):
    d = xt.shape[1]
    j, kind = divmod(i, N_MIXERS)
    w_gl = None
    if kind == 0:
        outs, aux, w_z = _mixer_a(xbs, b, s, a_w_in[j], tabs)
        w_out, name = a_w_out[j], "A"
    elif kind == 1:
        outs, aux, w_z, perm = _mixer_b(xbs, b, s, b_w_in[j], b_sinks[j], tabs)
        w_out, name = b_w_out[j][perm, :], "B"
    else:
        outs, aux, (w_z, w_gl), perm = _mixer_c(xbs, b, s, c_w_in[j], c_w_ck[j], c_w_cv[j], c_pos[j], tabs)
        w_out, name = c_w_out[j][perm, :], "C"
        w_gl = w_gl.astype(BF16)
    next_dils = _layer_dils(i + 1)
    res = _post(name, outs, aux, xt, xbs[1].reshape(b * s, d), p[i].reshape(b * s, -1), w_z.astype(BF16), w_gl,
                w_out.astype(BF16), ln_g[i].reshape(1, d), ln_b[i].reshape(1, d),
                ple_w_gate[i].astype(BF16), ple_w_proj[i].astype(BF16), b, out_dils=next_dils)
    new_xbs = {1: res[1].reshape(b, 1, s, d)}
    new_xbs.update(zip(next_dils, res[2:]))
    return res[0], new_xbs


def _layer_dils(i):
    if i < DEPTH and i % N_MIXERS == 0:
        return tuple(dil for _, dil in A_GROUPS if dil > 1)
    return ()


def kernel(x, p, a_w_in, a_w_out, b_w_in, b_sinks, b_w_out, c_w_in, c_w_ck, c_w_cv, c_pos, c_w_out,
           ln_g, ln_b, ple_w_proj, ple_w_gate):
    b, s, d = x.shape
    assert d == N_HEADS * HEAD_DIM and s % (QBLK * A_GROUPS[-1][1]) == 0 and s % C_WINDOW == 0
    tabs = _rope_tables(s)
    xt = x.reshape(b * s, d)
    xb = x.astype(BF16)
    xbs = {1: xb[:, None]}
    xbs.update({dil: _residue_major(xb, dil) for dil in _layer_dils(0)})
    for i in range(DEPTH):
        xt, xbs = _layer(i, xt, xbs, b, s, tabs, p, a_w_in, a_w_out, b_w_in, b_sinks, b_w_out, c_w_in, c_w_ck,
                         c_w_cv, c_pos, c_w_out, ln_g, ln_b, ple_w_proj, ple_w_gate)
    return xt.reshape(b, s, d)
```

```python
import functools

import numpy as np
import jax
import jax.numpy as jnp
from jax import lax
from jax.experimental import pallas as pl
from jax.experimental.pallas import tpu as pltpu

F32 = jnp.float32
BF16 = jnp.bfloat16

LANES = 128
VMEM_LIMIT_BYTES = 56 * 1024 * 1024

HEAD_DIM = 64
HALF = HEAD_DIM // 2
PAIR = 2 * HEAD_DIM
assert PAIR == LANES
N_HEADS = 16
N_PAIRS = N_HEADS // 2
ROPE_THETA = 10000.0
QBLK = 128
BAND_GROUP = 4
PROJ_ROWS = 128
POST_TILE = 512
POST_ROWS = 256
LN_EPS = 1e-5
DEPTH = 4
N_MIXERS = 3
DEEPNORM_ALPHA = (2 * DEPTH) ** 0.25
A_GROUPS = ((128, 1), (512, 4), (2048, 16))
B_KV_HEADS = 2
B_WINDOW = 128
C_KV_HEADS = 4
C_GROUP = N_HEADS // C_KV_HEADS
C_KV = C_KV_HEADS * HEAD_DIM
C_CMP_STRIDE = 16
C_CMP_LEN = 32
C_SEL_LEN = 64
C_N_SEL = 16
C_WINDOW = 512
C_SEL_OVERLAP = (1.0, 2.0, 2.0, 2.0, 1.0)
SEL_PER_CMP = C_SEL_LEN // C_CMP_STRIDE
SEL_TILE = 512
CMP_COL_STEP = 128
LOG2E = 1.4426950408889634
NEG_INF = float("-inf")
MASKED = -1e30

_NT = (((1,), (1,)), ((), ()))


def _params(n_grid):
    return pltpu.CompilerParams(
        dimension_semantics=("arbitrary",) * n_grid, vmem_limit_bytes=VMEM_LIMIT_BYTES)


def _proj_kernel(*refs, n_rope_tiles, n_tiles):
    it = iter(refs)
    x_ref, w_ref = next(it), next(it)
    if n_rope_tiles:
        tab_refs = [next(it), next(it), next(it)]
    o_ref = next(it)
    dil, sub, _ = x_ref.shape
    tn = w_ref.shape[1]
    n_chunks = dil * sub // PROJ_ROWS

    def chunk(ref, m, cols=slice(None)):
        if sub >= PROJ_ROWS:
            r, l0 = divmod(m * PROJ_ROWS, sub)
            return ref.at[r, l0:l0 + PROJ_ROWS, cols]
        k = PROJ_ROWS // sub
        return ref.at[m * k:(m + 1) * k, :, cols]

    def emit(rope):
        def matmul(m):
            xm = chunk(x_ref, m)[...].reshape(PROJ_ROWS, x_ref.shape[2])
            return jnp.dot(xm, w_ref[...], preferred_element_type=F32)

        def finish(m, acc):
            if rope:
                c, sa, sb = (chunk(t, m)[...].reshape(PROJ_ROWS, LANES) for t in tab_refs)
            for j in range(tn // LANES):
                cl = slice(j * LANES, (j + 1) * LANES)
                t = acc[:, cl]
                if rope:
                    t = t * c + pltpu.roll(t, LANES - HALF, 1) * sa + pltpu.roll(t, HALF, 1) * sb
                dst = chunk(o_ref, m, cl)
                dst[...] = t.astype(o_ref.dtype).reshape(dst.shape)

        acc = matmul(0)
        for m in range(n_chunks):
            nxt = matmul(m + 1) if m + 1 < n_chunks else None
            finish(m, acc)
            acc = nxt

    if n_rope_tiles == 0 or n_rope_tiles == n_tiles:
        emit(n_rope_tiles > 0)
    else:
        pl.when(pl.program_id(1) < n_rope_tiles)(lambda: emit(True))
        pl.when(pl.program_id(1) >= n_rope_tiles)(lambda: emit(False))


def _pick_tile(n, candidates):
    for c in candidates:
        if n % c == 0:
            return c
    raise ValueError(f"no tile for {n}")


def _proj(x, w, out_dtype, rope_tabs=None, n_rope_cols=0, tn=None):
    batch, dil, sub_len, k = x.shape
    n = w.shape[1]
    seq_len = dil * sub_len
    tm = _pick_tile(seq_len, (1024, 512, 256, 128))
    tn = tn or _pick_tile(np.gcd(n, n_rope_cols), (1024, 512, 384, 256, 128))
    sub = tm // dil
    assert n % tn == 0 and n_rope_cols % tn == 0 and tm % PROJ_ROWS == 0
    assert sub % 16 == 0 and (sub % PROJ_ROWS == 0 or PROJ_ROWS % sub == 0)
    n_seq_tiles = seq_len // tm
    in_specs = [pl.BlockSpec((None, dil, sub, k), lambda i, j: (i // n_seq_tiles, 0, i % n_seq_tiles, 0)),
                pl.BlockSpec((k, tn), lambda i, j: (0, j))]
    args = [x, w]
    if n_rope_cols:
        tab_spec = pl.BlockSpec((dil, sub, LANES), lambda i, j: (0, i % n_seq_tiles, 0))
        in_specs += [tab_spec] * 3
        args += list(rope_tabs)
    return pl.pallas_call(
        functools.partial(_proj_kernel, n_rope_tiles=n_rope_cols // tn, n_tiles=n // tn),
        grid=(batch * n_seq_tiles, n // tn),
        in_specs=in_specs,
        out_specs=pl.BlockSpec((None, dil, sub, tn), lambda i, j: (i // n_seq_tiles, 0, i % n_seq_tiles, j)),
        out_shape=jax.ShapeDtypeStruct((batch, dil, sub_len, n), out_dtype),
        compiler_params=_params(2),
        name=f"proj_d{dil}_r{n_rope_cols}",
    )(*args)


def _band_kernel(*refs, tq, npv, max_dist, kv_pair_of, has_sinks, want_lse):
    it = iter(refs)
    q_ref, kc_ref, kp_ref, vc_ref, vp_ref = (next(it) for _ in range(5))
    sink_ref = next(it) if has_sinks else None
    o_ref = next(it)
    lse_ref = next(it) if want_lse else None
    qb = pl.program_id(2)
    w = (npv + 1) * QBLK
    pv_rows = npv * QBLK
    lane = lax.broadcasted_iota(jnp.int32, (QBLK, LANES), 1)
    first_half = lane < HEAD_DIM
    qi = lax.broadcasted_iota(jnp.int32, (QBLK, w), 0)
    kj = lax.broadcasted_iota(jnp.int32, (QBLK, w), 1)
    dist = qi + npv * QBLK - kj
    band = (dist >= 0) & (dist <= max_dist)
    for sub in range(tq // QBLK):
        r0 = sub * QBLK
        kstart = qb * tq + r0 - npv * QBLK
        mask = band & (kj + kstart >= 0)
        lse_tile = jnp.zeros((QBLK, LANES), F32)
        for g0 in range(0, N_PAIRS, BAND_GROUP):
            staged = []
            for pi in range(g0, g0 + BAND_GROUP):
                cl = slice(kv_pair_of(pi) * LANES, (kv_pair_of(pi) + 1) * LANES)
                k_parts, v_parts = [], []
                if r0 < pv_rows:
                    k_parts.append(kp_ref[r0:pv_rows, cl])
                    v_parts.append(vp_ref[r0:pv_rows, cl])
                cs = max(r0 - pv_rows, 0)
                k_parts.append(kc_ref[cs:r0 + QBLK, cl])
                v_parts.append(vc_ref[cs:r0 + QBLK, cl])
                kwin = k_parts[0] if len(k_parts) == 1 else jnp.concatenate(k_parts, axis=0)
                vwin = v_parts[0] if len(v_parts) == 1 else jnp.concatenate(v_parts, axis=0)
                qp = q_ref[r0:r0 + QBLK, pi * LANES:(pi + 1) * LANES]
                for e in range(2):
                    qe = jnp.where(first_half if e == 0 else jnp.logical_not(first_half), qp, 0)
                    s = lax.dot_general(qe, kwin, _NT, preferred_element_type=F32)
                    staged.append((pi, e, jnp.where(mask, s, NEG_INF), vwin))
            outs = {}
            for pi, e, s, vwin in staged:
                m = jnp.max(s, axis=-1, keepdims=True)
                if has_sinks:
                    sk = sink_ref[2 * pi + e]
                    m = jnp.maximum(m, sk)
                p = jnp.exp2(s - m)
                l = jnp.sum(p, axis=-1, keepdims=True)
                if has_sinks:
                    l = l + jnp.exp2(sk - m)
                pv = jnp.dot(p.astype(BF16), vwin, preferred_element_type=F32)
                outs[(pi, e)] = pv / l
                if want_lse:
                    lse_tile = jnp.where(lane == 2 * pi + e, m + jnp.log2(l), lse_tile)
            for pi in range(g0, g0 + BAND_GROUP):
                o_ref[r0:r0 + QBLK, pi * LANES:(pi + 1) * LANES] = jnp.where(
                    first_half, outs[(pi, 0)], outs[(pi, 1)])
        if want_lse:
            lse_ref[r0:r0 + QBLK, :] = lse_tile


def _banded_attention(q_arr, k_arr, v_arr, *, tq, npv, max_dist, q_off, k_off, v_off,
                      kv_pairs, kv_pair_of, sinks=None, want_lse=False):
    b, dil, l, _ = q_arr.shape
    qw, kw = N_PAIRS * LANES, kv_pairs * LANES
    pv = npv * QBLK
    assert tq % pv == 0
    in_specs = [
        pl.BlockSpec((None, None, tq, qw), lambda bi, r, i: (bi, r, i, q_off)),
        pl.BlockSpec((None, None, tq, kw), lambda bi, r, i: (bi, r, i, k_off)),
        pl.BlockSpec((None, None, pv, kw), lambda bi, r, i: (bi, r, jnp.maximum(i * (tq // pv) - 1, 0), k_off)),
        pl.BlockSpec((None, None, tq, kw), lambda bi, r, i: (bi, r, i, v_off)),
        pl.BlockSpec((None, None, pv, kw), lambda bi, r, i: (bi, r, jnp.maximum(i * (tq // pv) - 1, 0), v_off)),
    ]
    args = [q_arr, k_arr, k_arr, v_arr, v_arr]
    if sinks is not None:
        in_specs.append(pl.BlockSpec(memory_space=pltpu.SMEM))
        args.append(sinks)
    out_specs = [pl.BlockSpec((None, None, tq, qw), lambda bi, r, i: (bi, r, i, 0))]
    out_shape = [jax.ShapeDtypeStruct((b, dil, l, qw), F32)]
    if want_lse:
        out_specs.append(pl.BlockSpec((None, None, tq, LANES), lambda bi, r, i: (bi, r, i, 0)))
        out_shape.append(jax.ShapeDtypeStruct((b, dil, l, LANES), F32))
    res = pl.pallas_call(
        functools.partial(_band_kernel, tq=tq, npv=npv, max_dist=max_dist, kv_pair_of=kv_pair_of,
                          has_sinks=sinks is not None, want_lse=want_lse),
        grid=(b, dil, l // tq),
        in_specs=in_specs,
        out_specs=out_specs,
        out_shape=out_shape,
        compiler_params=_params(3),
        name=f"band_d{dil}_w{max_dist}",
    )(*args)
    return res if want_lse else res[0]


def _compress_kernel(c_ref, pos_ref, w_ref, o_ref):
    c = c_ref[...]
    top = jnp.dot((c + pos_ref[0:1, :]).astype(BF16), w_ref[0], preferred_element_type=F32)
    bot = jnp.dot((c + pos_ref[1:2, :]).astype(BF16), w_ref[1], preferred_element_type=F32)
    nc = c.shape[0]
    o_ref[...] = (top + pltpu.roll(bot, nc - 1, 0)).astype(o_ref.dtype)


def _compress(chunks, pos, w):
    b, hk, nc, cw = chunks.shape
    return pl.pallas_call(
        _compress_kernel,
        grid=(b, hk),
        in_specs=[pl.BlockSpec((None, None, nc, cw), lambda bi, h: (bi, h, 0, 0)),
                  pl.BlockSpec((2, cw), lambda bi, h: (0, 0)),
                  pl.BlockSpec((2, cw, HEAD_DIM), lambda bi, h: (0, 0, 0))],
        out_specs=pl.BlockSpec((None, None, nc, HEAD_DIM), lambda bi, h: (bi, h, 0, 0)),
        out_shape=jax.ShapeDtypeStruct((b, hk, nc, HEAD_DIM), BF16),
        compiler_params=_params(2),
        name="nsa_compress",
    )(chunks, pos, w)


def _stack_group_queries(q_ref, mp, half):
    pairs = [C_GROUP * mp + i for i in range(C_GROUP)]
    return jnp.concatenate(
        [jnp.where(half, q_ref[:, pr * LANES:(pr + 1) * LANES], 0) for pr in pairs], axis=0)


def _store_group_heads(o_ref, val, mp, e, first_half):
    for i in range(C_GROUP):
        ol = slice((C_GROUP * mp + i) * LANES, (C_GROUP * mp + i + 1) * LANES)
        rows = slice(i * QBLK, (i + 1) * QBLK)
        if e == 0:
            o_ref[:, ol] = val[rows]
        else:
            o_ref[:, ol] = jnp.where(first_half, o_ref[:, ol], val[rows])


def _nsa_cmp_kernel(q_ref, kc_ref, vc_ref, ocmp_ref, sel_ref, impt_ref, *, n_sel):
    nc = kc_ref.shape[0]
    ns = nc // SEL_PER_CMP
    t0 = pl.program_id(1) * QBLK
    lane = lax.broadcasted_iota(jnp.int32, (QBLK, LANES), 1)
    first_half = lane < HEAD_DIM

    jj = lax.broadcasted_iota(jnp.int32, (ns, QBLK), 0)
    cur = (t0 + lax.broadcasted_iota(jnp.int32, (ns, QBLK), 1)) // C_SEL_LEN
    forced = (jj == 0) | (jj == cur) | (jj == cur - 1)
    bvalid = jj <= cur
    n_free = max(n_sel - 3, 0)

    def body(ncols):
        qi_c = lax.broadcasted_iota(jnp.int32, (QBLK, ncols), 0)
        nn_c = lax.broadcasted_iota(jnp.int32, (QBLK, ncols), 1)
        cvalid = nn_c * C_CMP_STRIDE + (C_CMP_LEN - 1) <= t0 + qi_c
        cvalid = jnp.concatenate([cvalid] * C_GROUP, axis=0)
        impt_ref[0:8, :] = jnp.zeros((8, QBLK), F32)
        if ncols < nc:
            impt_ref[8 + ncols:8 + nc, :] = jnp.zeros((nc - ncols, QBLK), F32)

        scores = []
        for kh in range(C_KV_HEADS):
            mp, e = divmod(kh, 2)
            cl = slice(mp * LANES, (mp + 1) * LANES)
            half = first_half if e == 0 else jnp.logical_not(first_half)
            qst = _stack_group_queries(q_ref, mp, half)
            sc = lax.dot_general(qst, kc_ref[0:ncols, cl], _NT, preferred_element_type=F32)
            scores.append(jnp.where(cvalid, sc, NEG_INF))

        imps = []
        for kh in range(C_KV_HEADS):
            mp, e = divmod(kh, 2)
            cl = slice(mp * LANES, (mp + 1) * LANES)
            sc = scores[kh]
            mx = jnp.max(sc, axis=-1, keepdims=True)
            mx = jnp.where(mx > NEG_INF, mx, 0.0)
            ee = jnp.exp2(sc - mx)
            pc = ee / jnp.maximum(jnp.sum(ee, axis=-1, keepdims=True), 1e-30)
            ocmp = jnp.dot(pc.astype(BF16), vc_ref[0:ncols, cl], preferred_element_type=F32)
            _store_group_heads(ocmp_ref, ocmp, mp, e, first_half)
            imp = pc[0:QBLK]
            for g in range(1, C_GROUP):
                imp = imp + pc[g * QBLK:(g + 1) * QBLK]
            imps.append(imp)

        for kh in range(C_KV_HEADS):
            for c in range(ncols // QBLK):
                impt_ref[8 + c * QBLK:8 + (c + 1) * QBLK, :] = imps[kh][:, c * QBLK:(c + 1) * QBLK].T
            imp_s = C_SEL_OVERLAP[0] * impt_ref[pl.ds(7, ns, stride=SEL_PER_CMP), :]
            for o_off in range(1, len(C_SEL_OVERLAP)):
                imp_s = imp_s + C_SEL_OVERLAP[o_off] * impt_ref[pl.ds(7 + o_off, ns, stride=SEL_PER_CMP), :]
            score = jnp.where(forced, NEG_INF, jnp.where(bvalid, imp_s, -1.0))
            selt = jnp.where(forced, 1.0, 0.0)
            for _ in range(n_free):
                best = jnp.max(score, axis=0, keepdims=True)
                first = jnp.min(jnp.where(score == best, jj, ns), axis=0, keepdims=True)
                hit = jj == first
                selt = jnp.where(hit, 1.0, selt)
                score = jnp.where(hit, NEG_INF, score)
            sel_ref[kh, 0:ns, :] = selt
            if ns < LANES:
                sel_ref[kh, ns:LANES, :] = jnp.zeros((LANES - ns, QBLK), F32)

    col_step = min(CMP_COL_STEP, nc)
    n_variants = nc // col_step
    tokens_per_step = col_step * C_CMP_STRIDE
    for k in range(n_variants):
        pl.when(t0 // tokens_per_step == k)(functools.partial(body, (k + 1) * col_step))


def _nsa_slc_kernel(qt_ref, sel_ref, ks_ref, vst_ref, oslc_ref, acc_ref):
    gq = C_GROUP * QBLK
    blocks_per_tile = SEL_TILE // C_SEL_LEN
    bias_rows = 16
    t0 = pl.program_id(1) * QBLK
    lane = lax.broadcasted_iota(jnp.int32, (QBLK, LANES), 1)
    first_half = lane < HEAD_DIM
    row = lax.broadcasted_iota(jnp.int32, (LANES, QBLK), 0)
    top_rows = row < HEAD_DIM
    n_full = t0 // SEL_TILE
    key_in_tile = lax.broadcasted_iota(jnp.int32, (SEL_TILE, gq), 0)
    query_pos = t0 + lax.broadcasted_iota(jnp.int32, (SEL_TILE, gq), 1) % QBLK
    block_of_key = lax.broadcasted_iota(jnp.int32, (SEL_TILE, LANES), 0) // C_SEL_LEN
    block_onehot = jnp.where(
        block_of_key == lax.broadcasted_iota(jnp.int32, (SEL_TILE, LANES), 1), 1.0, 0.0).astype(BF16)
    bias_pad = jnp.zeros((LANES - bias_rows, gq), BF16)

    qts = []
    for kh in range(C_KV_HEADS):
        mp, e = divmod(kh, 2)
        keep_rows = top_rows if e == 0 else jnp.logical_not(top_rows)
        qts.append(jnp.concatenate(
            [jnp.where(keep_rows, qt_ref[(C_GROUP * mp + i) * LANES:(C_GROUP * mp + i + 1) * LANES, :], 0)
             for i in range(C_GROUP)], axis=1))
    acc_ref[...] = jnp.zeros((C_KV_HEADS, LANES, gq), F32)

    def tile_step(kt, carry, diagonal):
        k0 = pl.multiple_of(kt * SEL_TILE, SEL_TILE)
        b0 = pl.multiple_of(kt * blocks_per_tile, blocks_per_tile)
        sts = []
        for kh in range(C_KV_HEADS):
            mp = kh // 2
            cl = slice(mp * LANES, (mp + 1) * LANES)
            keys = jnp.concatenate([ks_ref[pl.ds(k0, SEL_TILE), cl], block_onehot], axis=1)
            picked = sel_ref[kh, pl.ds(b0, blocks_per_tile), :]
            bias = jnp.concatenate([(picked - 1.0) * -MASKED] * C_GROUP, axis=1)
            bias = jnp.concatenate([bias, jnp.zeros((bias_rows - blocks_per_tile, gq), F32)], axis=0)
            queries = jnp.concatenate([qts[kh], bias.astype(BF16), bias_pad], axis=0)
            st = jnp.dot(keys, queries, preferred_element_type=F32)
            if diagonal:
                st = jnp.where(k0 + key_in_tile <= query_pos, st, MASKED)
            sts.append(st)
        new_carry = []
        for kh in range(C_KV_HEADS):
            mp = kh // 2
            cl = slice(mp * LANES, (mp + 1) * LANES)
            st = sts[kh]
            m_old = carry[kh]
            m_new = jnp.maximum(m_old, jnp.max(st, axis=0, keepdims=True))
            alpha = jnp.exp2(m_old - m_new)
            p = jnp.exp2(st - m_new)
            acc_ref[kh] = alpha * acc_ref[kh] + jnp.dot(
                vst_ref[kh % 2, kt, cl, :], p.astype(BF16), preferred_element_type=F32)
            new_carry += [m_new]
        return tuple(new_carry)

    init = (jnp.full((1, gq), NEG_INF, F32),) * C_KV_HEADS
    carry = lax.fori_loop(0, n_full, functools.partial(tile_step, diagonal=False), init)
    carry = lax.fori_loop(n_full, n_full + 1, functools.partial(tile_step, diagonal=True), carry)
    for mp in range(C_KV_HEADS // 2):
        ots = [acc_ref[2 * mp + e] / acc_ref[2 * mp + e, HEAD_DIM * (1 - e):HEAD_DIM * (1 - e) + 1, :]
               for e in range(2)]
        for i in range(C_GROUP):
            ol = slice((C_GROUP * mp + i) * LANES, (C_GROUP * mp + i + 1) * LANES)
            qs = slice(i * QBLK, (i + 1) * QBLK)
            oslc_ref[:, ol] = jnp.where(first_half, ots[0][:, qs].T, ots[1][:, qs].T)


def _nsa_select(q_arr, qt_arr, kcmp, vcmp, ks_arr, vst_arr, *, ks_off):
    b, s, _ = q_arr.shape
    nc = kcmp.shape[1]
    ns = nc // SEL_PER_CMP
    assert ns <= LANES and s % SEL_TILE == 0
    qw = N_PAIRS * LANES
    gq = C_GROUP * QBLK
    q_spec = pl.BlockSpec((None, QBLK, qw), lambda bi, i: (bi, i, 0))
    o_spec = pl.BlockSpec((None, QBLK, qw), lambda bi, i: (bi, i, 0))
    sel_spec = pl.BlockSpec((None, C_KV_HEADS, LANES, QBLK), lambda bi, i: (bi, 0, 0, i))
    o_shape = jax.ShapeDtypeStruct((b, s, qw), F32)
    o_cmp, sel = pl.pallas_call(
        functools.partial(_nsa_cmp_kernel, n_sel=min(C_N_SEL, ns)),
        grid=(b, s // QBLK),
        in_specs=[q_spec,
                  pl.BlockSpec((None, nc, C_KV), lambda bi, i: (bi, 0, 0)),
                  pl.BlockSpec((None, nc, C_KV), lambda bi, i: (bi, 0, 0))],
        out_specs=[o_spec, sel_spec],
        out_shape=[o_shape, jax.ShapeDtypeStruct((b, C_KV_HEADS, LANES, s), F32)],
        scratch_shapes=[pltpu.VMEM((8 + nc, QBLK), F32)],
        compiler_params=_params(2),
        name="nsa_compressed",
    )(q_arr, kcmp, vcmp)
    o_slc = pl.pallas_call(
        _nsa_slc_kernel,
        grid=(b, s // QBLK),
        in_specs=[pl.BlockSpec((None, qw, QBLK), lambda bi, i: (bi, 0, i)),
                  sel_spec,
                  pl.BlockSpec((None, s, C_KV), lambda bi, i: (bi, 0, ks_off)),
                  pl.BlockSpec((None, 2, s // SEL_TILE, C_KV, SEL_TILE), lambda bi, i: (bi, 0, 0, 0, 0))],
        out_specs=o_spec,
        out_shape=o_shape,
        scratch_shapes=[pltpu.VMEM((C_KV_HEADS, LANES, gq), F32)],
        compiler_params=_params(2),
        name="nsa_selected",
    )(qt_arr, sel, ks_arr, vst_arr)
    return o_cmp, o_slc


def _head_cols(tile, col, first_half):
    tm = tile.shape[0]
    a = jnp.broadcast_to(tile[:, col:col + 1], (tm, LANES))
    b = jnp.broadcast_to(tile[:, col + 1:col + 2], (tm, LANES))
    return jnp.where(first_half, a, b)


def _post_kernel(*refs, kind, n_staged, n_dil_outs):
    it = iter(refs)
    n_branch = {"A": 3, "B": 1, "C": 3}[kind]
    o_refs = [next(it) for _ in range(n_branch)]
    aux_refs = [next(it) for _ in range(3)] if kind == "A" else []
    x_ref, xb_ref, p_ref, wz_ref = (next(it) for _ in range(4))
    wgl_ref = next(it) if kind == "C" else None
    wo_ref, g_ref, b_ref, wg_ref, wp_ref, out_ref, outb_ref = (next(it) for _ in range(7))
    dil_out_refs = [next(it) for _ in range(n_dil_outs)]
    u_ref, z_ref = next(it), next(it)
    stage_refs = [next(it) for _ in range(n_staged + (1 if n_dil_outs else 0))]
    tm = x_ref.shape[0]

    def token_order(ref):
        if len(ref.shape) == 2:
            return lambda j: ref[:, j * LANES:(j + 1) * LANES]
        dil = ref.shape[0]
        stage = stage_refs.pop(0)
        for j in range(ref.shape[2] // LANES):
            for r in range(dil):
                stage[j, pl.ds(r, tm // dil, stride=dil), :] = ref[r, :, j * LANES:(j + 1) * LANES]
        return lambda j: stage[j]

    o_cols = [token_order(r) for r in o_refs]
    aux_cols = [token_order(r) for r in aux_refs]

    n_chunks = tm // POST_ROWS
    rows = [slice(c * POST_ROWS, (c + 1) * POST_ROWS) for c in range(n_chunks)]
    first_half = lax.broadcasted_iota(jnp.int32, (POST_ROWS, LANES), 1) < HEAD_DIM
    gates = [None] * n_chunks
    for c, rs in enumerate(rows):
        xb = xb_ref[rs, :]
        z_ref[rs, :] = jnp.dot(xb, wz_ref[...], preferred_element_type=F32)
        if kind == "C":
            gates[c] = jax.nn.sigmoid(jnp.dot(xb, wgl_ref[...], preferred_element_type=F32))

    hs = []
    for c, rs in enumerate(rows):
        if kind == "A":
            lses = [col(0)[rs] for col in aux_cols]
            mx = jnp.maximum(jnp.maximum(lses[0], lses[1]), lses[2])
            ws = [jnp.exp2(v - mx) for v in lses]
            den = ws[0] + ws[1] + ws[2]
            ws = [v / den for v in ws]
        for pi in range(N_PAIRS):
            cl = slice(pi * LANES, (pi + 1) * LANES)
            if kind == "A":
                o = _head_cols(ws[0], 2 * pi, first_half) * o_cols[0](pi)[rs]
                for g in range(1, 3):
                    o = o + _head_cols(ws[g], 2 * pi, first_half) * o_cols[g](pi)[rs]
            elif kind == "B":
                o = o_cols[0](pi)[rs]
            else:
                o = _head_cols(gates[c], 2 * pi, first_half) * o_cols[0](pi)[rs]
                for br in range(1, 3):
                    o = o + _head_cols(gates[c], br * N_HEADS + 2 * pi, first_half) * o_cols[br](pi)[rs]
            z = z_ref[rs, cl]
            u_ref[rs, cl] = (o * (z * jax.nn.sigmoid(z))).astype(BF16)
        hs.append(jnp.dot(u_ref[rs, :], wo_ref[...], preferred_element_type=F32))

    pre = []
    for c, rs in enumerate(rows):
        y = DEEPNORM_ALPHA * x_ref[rs, :] + hs[c]
        mu = jnp.mean(y, axis=-1, keepdims=True)
        yc = y - mu
        var = jnp.mean(yc * yc, axis=-1, keepdims=True)
        yn = yc * lax.rsqrt(var + LN_EPS) * g_ref[...] + b_ref[...]
        gate_logits = jnp.dot(yn.astype(BF16), wg_ref[...], preferred_element_type=F32)
        pp = jnp.dot(p_ref[rs, :].astype(BF16), wp_ref[...], preferred_element_type=F32)
        pre.append((yn, gate_logits, pp))

    for c, rs in enumerate(rows):
        yn, gate_logits, pp = pre[c]
        x_new = yn + jax.nn.sigmoid(gate_logits) * pp
        out_ref[rs, :] = x_new
        outb_ref[rs, :] = x_new.astype(BF16)
        if dil_out_refs:
            for j in range(x_new.shape[1] // LANES):
                stage_refs[-1][j, rs, :] = x_new[:, j * LANES:(j + 1) * LANES]
    if dil_out_refs:
        stage = stage_refs[-1]
        for o_ref in dil_out_refs:
            dil = o_ref.shape[0]
            for j in range(x_new.shape[1] // LANES):
                for r in range(dil):
                    o_ref[r, :, j * LANES:(j + 1) * LANES] = stage[j, pl.ds(r, tm // dil, stride=dil), :].astype(BF16)


def _post(kind, o_list, aux_list, x, xb, p, w_z, w_gl, w_out, ln_g, ln_b, w_gate, w_proj, batch, out_dils=()):
    t, d = x.shape
    tm = POST_TILE
    row = lambda w: pl.BlockSpec((tm, w), lambda i: (i, 0))
    full = lambda a: pl.BlockSpec(a.shape, lambda i: (0,) * a.ndim)

    branch_specs, stages = [], []
    for a in list(o_list) + list(aux_list):
        if a.ndim == 2:
            branch_specs.append(row(a.shape[1]))
        else:
            _, dil, sub_len, w = a.shape
            n_seq_tiles = sub_len * dil // tm
            branch_specs.append(pl.BlockSpec(
                (None, dil, tm // dil, w), lambda i, n=n_seq_tiles: (i // n, 0, i % n, 0)))
            stages.append(pltpu.VMEM((w // LANES, tm, LANES), F32))
    n_staged = len(stages)
    if out_dils:
        stages.append(pltpu.VMEM((d // LANES, tm, LANES), F32))
    seq_tiles = t // batch // tm
    weights = [w_z] + ([w_gl] if kind == "C" else []) + [w_out, ln_g, ln_b, w_gate, w_proj]
    args = list(o_list) + list(aux_list) + [x, xb, p] + weights
    in_specs = branch_specs + [row(d), row(d), row(p.shape[1])] + [full(w) for w in weights]
    return pl.pallas_call(
        functools.partial(_post_kernel, kind=kind, n_staged=n_staged, n_dil_outs=len(out_dils)),
        grid=(t // tm,),
        in_specs=in_specs,
        out_specs=[row(d), row(d)] + [
            pl.BlockSpec((None, dil, tm // dil, d), lambda i, n=seq_tiles: (i // n, 0, i % n, 0)) for dil in out_dils],
        out_shape=[jax.ShapeDtypeStruct((t, d), F32), jax.ShapeDtypeStruct((t, d), BF16)] + [
            jax.ShapeDtypeStruct((batch, dil, t // batch // dil, d), BF16) for dil in out_dils],
        scratch_shapes=[pltpu.VMEM((tm, d), BF16), pltpu.VMEM((tm, d), F32)] + stages,
        compiler_params=_params(1),
        name=f"post_{kind}",
    )(*args)


def _rope_tables(seq_len):
    inv = 1.0 / (ROPE_THETA ** (jnp.arange(0, HEAD_DIM, 2, dtype=F32) / HEAD_DIM))
    ang = jnp.arange(seq_len, dtype=F32)[:, None] * inv[None, :]
    cos, sin = jnp.cos(ang), jnp.sin(ang)
    zero = jnp.zeros_like(sin)
    cos_t = jnp.concatenate([cos] * 4, axis=1)
    sa_t = jnp.concatenate([-sin, zero, -sin, zero], axis=1)
    sb_t = jnp.concatenate([zero, sin, zero, sin], axis=1)
    return cos_t, sa_t, sb_t


def _residue_major(a, dil):
    *lead, s, w = a.shape
    return jnp.swapaxes(a.reshape(*lead, s // dil, dil, w), -3, -2)


def _head_cols_index(head_order):
    return np.concatenate([np.arange(h * HEAD_DIM, (h + 1) * HEAD_DIM) for h in head_order])


_B_HEAD_ORDER = [e * (N_HEADS // B_KV_HEADS) + i for i in range(N_PAIRS) for e in range(2)]
_C_HEAD_ORDER = [C_GROUP * (2 * m + e) + i for m in range(C_KV_HEADS // 2) for i in range(C_GROUP) for e in range(2)]


def _mixer_a(xbs, b, s, w_in, tabs):
    wd = N_HEADS * HEAD_DIM
    scale = HEAD_DIM ** -0.5 * LOG2E
    outs, lses = [], []
    for gi, (window, dil) in enumerate(A_GROUPS):
        base = 3 * gi * wd
        w_qkv = jnp.concatenate([w_in[:, base:base + wd] * scale, w_in[:, base + wd:base + 3 * wd]], axis=1)
        qkv = _proj(xbs[dil], w_qkv.astype(BF16), BF16, [_residue_major(t, dil) for t in tabs],
                    n_rope_cols=2 * wd)
        o, lse = _banded_attention(
            qkv, qkv, qkv, tq=min(512, s // dil), npv=1, max_dist=window // dil,
            q_off=0, k_off=1, v_off=2, kv_pairs=N_PAIRS, kv_pair_of=lambda pi: pi, want_lse=True)
        if dil == 1:
            o, lse = o.reshape(b * s, wd), lse.reshape(b * s, LANES)
        outs.append(o)
        lses.append(lse)
    return outs, lses, w_in[:, -wd:]


def _mixer_b(xbs, b, s, w_in, sinks, tabs):
    wd = N_HEADS * HEAD_DIM
    kvw = B_KV_HEADS * HEAD_DIM
    perm = _head_cols_index(_B_HEAD_ORDER)
    wq = w_in[:, :wd][:, perm] * (HEAD_DIM ** -0.5 * LOG2E)
    wk = w_in[:, wd:wd + kvw]
    wv = w_in[:, wd + kvw:wd + 2 * kvw]
    wz = w_in[:, wd + 2 * kvw:][:, perm]
    tabs1 = [t[None] for t in tabs]
    qk = _proj(xbs[1], jnp.concatenate([wq, wk], axis=1).astype(BF16), BF16, tabs1, n_rope_cols=wd + kvw)
    v = _proj(xbs[1], wv.astype(BF16), BF16)
    o = _banded_attention(
        qk, qk, v, tq=min(256, s), npv=1, max_dist=B_WINDOW - 1,
        q_off=0, k_off=wd // kvw, v_off=0, kv_pairs=1, kv_pair_of=lambda pi: 0,
        sinks=sinks[np.asarray(_B_HEAD_ORDER)].astype(F32) * LOG2E)
    return [o.reshape(b * s, wd)], [], wz, perm


def _mixer_c(xbs, b, s, w_in, w_ck, w_cv, pos, tabs):
    wd = N_HEADS * HEAD_DIM
    perm = _head_cols_index(_C_HEAD_ORDER)
    cols = np.cumsum([0, wd] + [C_KV] * 6 + [3 * N_HEADS, wd])
    part = lambda i: w_in[:, cols[i]:cols[i + 1]]
    wq = part(0)[:, perm] * (HEAD_DIM ** -0.5 * LOG2E)
    w_gl = part(7).reshape(-1, 3, N_HEADS)[:, :, np.asarray(_C_HEAD_ORDER)].reshape(-1, 3 * N_HEADS)
    w_gl = jnp.pad(w_gl, ((0, 0), (0, LANES - 3 * N_HEADS)))
    w_att = jnp.concatenate([wq, part(3), part(5), part(4), part(6)], axis=1)
    tabs1 = [t[None] for t in tabs]
    att = _proj(xbs[1], w_att.astype(BF16), BF16, tabs1, n_rope_cols=wd + 2 * C_KV)
    cmp_in = _proj(xbs[1], jnp.concatenate([part(1), part(2)], axis=1).astype(BF16), F32, tabs1,
                   n_rope_cols=C_KV, tn=C_KV).reshape(b, s, 2 * C_KV)

    nc = s // C_CMP_STRIDE
    cw = C_CMP_STRIDE * HEAD_DIM

    def chunks(t):
        return jnp.transpose(t.reshape(b, s, C_KV_HEADS, HEAD_DIM), (0, 2, 1, 3)).reshape(b, C_KV_HEADS, nc, cw)

    pos2 = pos.reshape(2, cw)

    def compressed(t, w):
        c = _compress(chunks(t), pos2, w.reshape(2, cw, HEAD_DIM).astype(BF16))
        return jnp.transpose(c, (0, 2, 1, 3)).reshape(b, nc, C_KV)

    kcmp, vcmp = compressed(cmp_in[:, :, :C_KV], w_ck), compressed(cmp_in[:, :, C_KV:], w_cv)
    att3 = att.reshape(b, s, -1)
    qt = jnp.transpose(att3[:, :, :wd], (0, 2, 1))
    vs = att3[:, :, wd + 2 * C_KV:wd + 3 * C_KV]
    vst = jnp.transpose(vs.reshape(b, s // SEL_TILE, SEL_TILE, C_KV), (0, 1, 3, 2))
    in_half0 = (jnp.arange(C_KV) % PAIR < HEAD_DIM)[None, None, :, None]
    vst = jnp.stack([jnp.where(in_half0, vst, 1), jnp.where(in_half0, 1, vst)], axis=1)
    o_cmp, o_slc = _nsa_select(att3, qt, kcmp, vcmp, att3, vst, ks_off=wd // C_KV)
    o_win = _banded_attention(
        att, att, att, tq=C_WINDOW, npv=C_WINDOW // QBLK, max_dist=C_WINDOW - 1,
        q_off=0, k_off=wd // C_KV + 1, v_off=wd // C_KV + 3, kv_pairs=C_KV_HEADS // 2,
        kv_pair_of=lambda pi: pi // C_GROUP)
    outs = [o.reshape(b * s, wd) for o in (o_cmp, o_slc, o_win)]
    return outs, [], (part(8)[:, perm], w_gl), perm


def _layer(i, xt, xbs, b, s, tabs, p, a_w_in, a_w_out, b_w_in, b_sinks, b_w_out, c_w_in, c_w_ck, c_w_cv, c_pos,
           c_w_out, ln_g, ln_b, ple_w_proj, ple_w_gate):
    d = xt.shape[1]
    j, kind = divmod(i, N_MIXERS)
    w_gl = None
    if kind == 0:
        outs, aux, w_z = _mixer_a(xbs, b, s, a_w_in[j], tabs)
        w_out, name = a_w_out[j], "A"
    elif kind == 1:
        outs, aux, w_z, perm = _mixer_b(xbs, b, s, b_w_in[j], b_sinks[j], tabs)
        w_out, name = b_w_out[j][perm, :], "B"
    else:
        outs, aux, (w_z, w_gl), perm = _mixer_c(xbs, b, s, c_w_in[j], c_w_ck[j], c_w_cv[j], c_pos[j], tabs)
        w_out, name = c_w_out[j][perm, :], "C"
        w_gl = w_gl.astype(BF16)
    next_dils = _layer_dils(i + 1)
    res = _post(name, outs, aux, xt, xbs[1].reshape(b * s, d), p[i].reshape(b * s, -1), w_z.astype(BF16), w_gl,
                w_out.astype(BF16), ln_g[i].reshape(1, d), ln_b[i].reshape(1, d),
                ple_w_gate[i].astype(BF16), ple_w_proj[i].astype(BF16), b, out_dils=next_dils)
    new_xbs = {1: res[1].reshape(b, 1, s, d)}
    new_xbs.update(zip(next_dils, res[2:]))
    return res[0], new_xbs


def _layer_dils(i):
    if i < DEPTH and i % N_MIXERS == 0:
        return tuple(dil for _, dil in A_GROUPS if dil > 1)
    return ()


def kernel(x, p, a_w_in, a_w_out, b_w_in, b_sinks, b_w_out, c_w_in, c_w_ck, c_w_cv, c_pos, c_w_out,
           ln_g, ln_b, ple_w_proj, ple_w_gate):
    b, s, d = x.shape
    assert d == N_HEADS * HEAD_DIM and s % (QBLK * A_GROUPS[-1][1]) == 0 and s % C_WINDOW == 0
    tabs = _rope_tables(s)
    xt = x.reshape(b * s, d)
    xb = x.astype(BF16)
    xbs = {1: xb[:, None]}
    xbs.update({dil: _residue_major(xb, dil) for dil in _layer_dils(0)})
    for i in range(DEPTH):
        xt, xbs = _layer(i, xt, xbs, b, s, tabs, p, a_w_in, a_w_out, b_w_in, b_sinks, b_w_out, c_w_in, c_w_ck,
                         c_w_cv, c_pos, c_w_out, ln_g, ln_b, ple_w_proj, ple_w_gate)
    return xt.reshape(b, s, d)
```

```python
import functools

import numpy as np
import jax
import jax.numpy as jnp
from jax import lax
from jax.experimental import pallas as pl
from jax.experimental.pallas import tpu as pltpu

F32 = jnp.float32
BF16 = jnp.bfloat16

LANES = 128
VMEM_LIMIT_BYTES = 56 * 1024 * 1024

HEAD_DIM = 64
HALF = HEAD_DIM // 2
PAIR = 2 * HEAD_DIM
assert PAIR == LANES
N_HEADS = 16
N_PAIRS = N_HEADS // 2
ROPE_THETA = 10000.0
QBLK = 128
BAND_GROUP = 4
PROJ_ROWS = 128
POST_TILE = 512
POST_ROWS = 256
LN_EPS = 1e-5
DEPTH = 4
N_MIXERS = 3
DEEPNORM_ALPHA = (2 * DEPTH) ** 0.25
A_GROUPS = ((128, 1), (512, 4), (2048, 16))
B_KV_HEADS = 2
B_WINDOW = 128
C_KV_HEADS = 4
C_GROUP = N_HEADS // C_KV_HEADS
C_KV = C_KV_HEADS * HEAD_DIM
C_CMP_STRIDE = 16
C_CMP_LEN = 32
C_SEL_LEN = 64
C_N_SEL = 16
C_WINDOW = 512
C_SEL_OVERLAP = (1.0, 2.0, 2.0, 2.0, 1.0)
SEL_PER_CMP = C_SEL_LEN // C_CMP_STRIDE
SEL_TILE = 512
CMP_COL_STEP = 128
LOG2E = 1.4426950408889634
NEG_INF = float("-inf")
MASKED = -1e30

_NT = (((1,), (1,)), ((), ()))


def _params(n_grid):
    return pltpu.CompilerParams(
        dimension_semantics=("arbitrary",) * n_grid, vmem_limit_bytes=VMEM_LIMIT_BYTES)


def _proj_kernel(*refs, n_rope_tiles, n_tiles):
    it = iter(refs)
    x_ref, w_ref = next(it), next(it)
    if n_rope_tiles:
        tab_refs = [next(it), next(it), next(it)]
    o_ref = next(it)
    dil, sub, _ = x_ref.shape
    tn = w_ref.shape[1]
    n_chunks = dil * sub // PROJ_ROWS

    def chunk(ref, m, cols=slice(None)):
        if sub >= PROJ_ROWS:
            r, l0 = divmod(m * PROJ_ROWS, sub)
            return ref.at[r, l0:l0 + PROJ_ROWS, cols]
        k = PROJ_ROWS // sub
        return ref.at[m * k:(m + 1) * k, :, cols]

    def emit(rope):
        def matmul(m):
            xm = chunk(x_ref, m)[...].reshape(PROJ_ROWS, x_ref.shape[2])
            return jnp.dot(xm, w_ref[...], preferred_element_type=F32)

        def finish(m, acc):
            if rope:
                c, sa, sb = (chunk(t, m)[...].reshape(PROJ_ROWS, LANES) for t in tab_refs)
            for j in range(tn // LANES):
                cl = slice(j * LANES, (j + 1) * LANES)
                t = acc[:, cl]
                if rope:
                    t = t * c + pltpu.roll(t, LANES - HALF, 1) * sa + pltpu.roll(t, HALF, 1) * sb
                dst = chunk(o_ref, m, cl)
                dst[...] = t.astype(o_ref.dtype).reshape(dst.shape)

        acc = matmul(0)
        for m in range(n_chunks):
            nxt = matmul(m + 1) if m + 1 < n_chunks else None
            finish(m, acc)
            acc = nxt

    if n_rope_tiles == 0 or n_rope_tiles == n_tiles:
        emit(n_rope_tiles > 0)
    else:
        pl.when(pl.program_id(1) < n_rope_tiles)(lambda: emit(True))
        pl.when(pl.program_id(1) >= n_rope_tiles)(lambda: emit(False))


def _pick_tile(n, candidates):
    for c in candidates:
        if n % c == 0:
            return c
    raise ValueError(f"no tile for {n}")


def _proj(x, w, out_dtype, rope_tabs=None, n_rope_cols=0, tn=None, w_cols=None):
    batch, dil, sub_len, k = x.shape
    layer, col0, n = w_cols if w_cols is not None else (None, 0, w.shape[-1])
    seq_len = dil * sub_len
    tm = _pick_tile(seq_len, (1024, 512, 256, 128))
    tn = tn or _pick_tile(np.gcd(n, n_rope_cols), (1024, 512, 384, 256, 128))
    sub = tm // dil
    assert n % tn == 0 and n_rope_cols % tn == 0 and tm % PROJ_ROWS == 0
    assert sub % 16 == 0 and (sub % PROJ_ROWS == 0 or PROJ_ROWS % sub == 0)
    n_seq_tiles = seq_len // tm
    assert col0 % tn == 0
    w_spec = (pl.BlockSpec((k, tn), lambda i, j: (0, j)) if layer is None else
              pl.BlockSpec((None, k, tn), lambda i, j: (layer, 0, j + col0 // tn)))
    in_specs = [pl.BlockSpec((None, dil, sub, k), lambda i, j: (i // n_seq_tiles, 0, i % n_seq_tiles, 0)), w_spec]
    args = [x, w]
    if n_rope_cols:
        tab_spec = pl.BlockSpec((dil, sub, LANES), lambda i, j: (0, i % n_seq_tiles, 0))
        in_specs += [tab_spec] * 3
        args += list(rope_tabs)
    return pl.pallas_call(
        functools.partial(_proj_kernel, n_rope_tiles=n_rope_cols // tn, n_tiles=n // tn),
        grid=(batch * n_seq_tiles, n // tn),
        in_specs=in_specs,
        out_specs=pl.BlockSpec((None, dil, sub, tn), lambda i, j: (i // n_seq_tiles, 0, i % n_seq_tiles, j)),
        out_shape=jax.ShapeDtypeStruct((batch, dil, sub_len, n), out_dtype),
        compiler_params=_params(2),
        name=f"proj_d{dil}_r{n_rope_cols}",
    )(*args)


def _band_kernel(*refs, tq, npv, max_dist, kv_pair_of, has_sinks, want_lse):
    it = iter(refs)
    q_ref, kc_ref, kp_ref, vc_ref, vp_ref = (next(it) for _ in range(5))
    sink_ref = next(it) if has_sinks else None
    o_ref = next(it)
    lse_ref = next(it) if want_lse else None
    qb = pl.program_id(2)
    w = (npv + 1) * QBLK
    pv_rows = npv * QBLK
    lane = lax.broadcasted_iota(jnp.int32, (QBLK, LANES), 1)
    first_half = lane < HEAD_DIM
    qi = lax.broadcasted_iota(jnp.int32, (QBLK, w), 0)
    kj = lax.broadcasted_iota(jnp.int32, (QBLK, w), 1)
    dist = qi + npv * QBLK - kj
    band = (dist >= 0) & (dist <= max_dist)
    for sub in range(tq // QBLK):
        r0 = sub * QBLK
        kstart = qb * tq + r0 - npv * QBLK
        mask = band & (kj + kstart >= 0)
        lse_tile = jnp.zeros((QBLK, LANES), F32)
        for g0 in range(0, N_PAIRS, BAND_GROUP):
            staged = []
            for pi in range(g0, g0 + BAND_GROUP):
                cl = slice(kv_pair_of(pi) * LANES, (kv_pair_of(pi) + 1) * LANES)
                k_parts, v_parts = [], []
                if r0 < pv_rows:
                    k_parts.append(kp_ref[r0:pv_rows, cl])
                    v_parts.append(vp_ref[r0:pv_rows, cl])
                cs = max(r0 - pv_rows, 0)
                k_parts.append(kc_ref[cs:r0 + QBLK, cl])
                v_parts.append(vc_ref[cs:r0 + QBLK, cl])
                kwin = k_parts[0] if len(k_parts) == 1 else jnp.concatenate(k_parts, axis=0)
                vwin = v_parts[0] if len(v_parts) == 1 else jnp.concatenate(v_parts, axis=0)
                qp = q_ref[r0:r0 + QBLK, pi * LANES:(pi + 1) * LANES]
                for e in range(2):
                    qe = jnp.where(first_half if e == 0 else jnp.logical_not(first_half), qp, 0)
                    s = lax.dot_general(qe, kwin, _NT, preferred_element_type=F32)
                    staged.append((pi, e, jnp.where(mask, s, NEG_INF), vwin))
            outs = {}
            for pi, e, s, vwin in staged:
                m = jnp.max(s, axis=-1, keepdims=True)
                if has_sinks:
                    sk = sink_ref[2 * pi + e]
                    m = jnp.maximum(m, sk)
                p = jnp.exp2(s - m)
                l = jnp.sum(p, axis=-1, keepdims=True)
                if has_sinks:
                    l = l + jnp.exp2(sk - m)
                pv = jnp.dot(p.astype(BF16), vwin, preferred_element_type=F32)
                outs[(pi, e)] = pv / l
                if want_lse:
                    lse_tile = jnp.where(lane == 2 * pi + e, m + jnp.log2(l), lse_tile)
            for pi in range(g0, g0 + BAND_GROUP):
                o_ref[r0:r0 + QBLK, pi * LANES:(pi + 1) * LANES] = jnp.where(
                    first_half, outs[(pi, 0)], outs[(pi, 1)])
        if want_lse:
            lse_ref[r0:r0 + QBLK, :] = lse_tile


def _banded_attention(q_arr, k_arr, v_arr, *, tq, npv, max_dist, q_off, k_off, v_off,
                      kv_pairs, kv_pair_of, sinks=None, want_lse=False):
    b, dil, l, _ = q_arr.shape
    qw, kw = N_PAIRS * LANES, kv_pairs * LANES
    pv = npv * QBLK
    assert tq % pv == 0
    in_specs = [
        pl.BlockSpec((None, None, tq, qw), lambda bi, r, i: (bi, r, i, q_off)),
        pl.BlockSpec((None, None, tq, kw), lambda bi, r, i: (bi, r, i, k_off)),
        pl.BlockSpec((None, None, pv, kw), lambda bi, r, i: (bi, r, jnp.maximum(i * (tq // pv) - 1, 0), k_off)),
        pl.BlockSpec((None, None, tq, kw), lambda bi, r, i: (bi, r, i, v_off)),
        pl.BlockSpec((None, None, pv, kw), lambda bi, r, i: (bi, r, jnp.maximum(i * (tq // pv) - 1, 0), v_off)),
    ]
    args = [q_arr, k_arr, k_arr, v_arr, v_arr]
    if sinks is not None:
        in_specs.append(pl.BlockSpec(memory_space=pltpu.SMEM))
        args.append(sinks)
    out_specs = [pl.BlockSpec((None, None, tq, qw), lambda bi, r, i: (bi, r, i, 0))]
    out_shape = [jax.ShapeDtypeStruct((b, dil, l, qw), F32)]
    if want_lse:
        out_specs.append(pl.BlockSpec((None, None, tq, LANES), lambda bi, r, i: (bi, r, i, 0)))
        out_shape.append(jax.ShapeDtypeStruct((b, dil, l, LANES), F32))
    res = pl.pallas_call(
        functools.partial(_band_kernel, tq=tq, npv=npv, max_dist=max_dist, kv_pair_of=kv_pair_of,
                          has_sinks=sinks is not None, want_lse=want_lse),
        grid=(b, dil, l // tq),
        in_specs=in_specs,
        out_specs=out_specs,
        out_shape=out_shape,
        compiler_params=_params(3),
        name=f"band_d{dil}_w{max_dist}",
    )(*args)
    return res if want_lse else res[0]


def _compress_kernel(c_ref, pos_ref, w_ref, o_ref):
    c = c_ref[...]
    top = jnp.dot((c + pos_ref[0:1, :]).astype(BF16), w_ref[0], preferred_element_type=F32)
    bot = jnp.dot((c + pos_ref[1:2, :]).astype(BF16), w_ref[1], preferred_element_type=F32)
    nc = c.shape[0]
    o_ref[...] = (top + pltpu.roll(bot, nc - 1, 0)).astype(o_ref.dtype)


def _compress(chunks, pos, w):
    b, hk, nc, cw = chunks.shape
    return pl.pallas_call(
        _compress_kernel,
        grid=(b, hk),
        in_specs=[pl.BlockSpec((None, None, nc, cw), lambda bi, h: (bi, h, 0, 0)),
                  pl.BlockSpec((2, cw), lambda bi, h: (0, 0)),
                  pl.BlockSpec((2, cw, HEAD_DIM), lambda bi, h: (0, 0, 0))],
        out_specs=pl.BlockSpec((None, None, nc, HEAD_DIM), lambda bi, h: (bi, h, 0, 0)),
        out_shape=jax.ShapeDtypeStruct((b, hk, nc, HEAD_DIM), BF16),
        compiler_params=_params(2),
        name="nsa_compress",
    )(chunks, pos, w)


def _stack_group_queries(q_ref, mp, half):
    pairs = [C_GROUP * mp + i for i in range(C_GROUP)]
    return jnp.concatenate(
        [jnp.where(half, q_ref[:, pr * LANES:(pr + 1) * LANES], 0) for pr in pairs], axis=0)


def _store_group_heads(o_ref, val, mp, e, first_half):
    for i in range(C_GROUP):
        ol = slice((C_GROUP * mp + i) * LANES, (C_GROUP * mp + i + 1) * LANES)
        rows = slice(i * QBLK, (i + 1) * QBLK)
        if e == 0:
            o_ref[:, ol] = val[rows]
        else:
            o_ref[:, ol] = jnp.where(first_half, o_ref[:, ol], val[rows])


def _nsa_cmp_kernel(q_ref, kc_ref, vc_ref, ocmp_ref, sel_ref, impt_ref, *, n_sel):
    nc = kc_ref.shape[0]
    ns = nc // SEL_PER_CMP
    t0 = pl.program_id(1) * QBLK
    lane = lax.broadcasted_iota(jnp.int32, (QBLK, LANES), 1)
    first_half = lane < HEAD_DIM

    jj = lax.broadcasted_iota(jnp.int32, (ns, QBLK), 0)
    cur = (t0 + lax.broadcasted_iota(jnp.int32, (ns, QBLK), 1)) // C_SEL_LEN
    forced = (jj == 0) | (jj == cur) | (jj == cur - 1)
    bvalid = jj <= cur
    n_free = max(n_sel - 3, 0)

    def body(ncols):
        qi_c = lax.broadcasted_iota(jnp.int32, (QBLK, ncols), 0)
        nn_c = lax.broadcasted_iota(jnp.int32, (QBLK, ncols), 1)
        cvalid = nn_c * C_CMP_STRIDE + (C_CMP_LEN - 1) <= t0 + qi_c
        cvalid = jnp.concatenate([cvalid] * C_GROUP, axis=0)
        impt_ref[0:8, :] = jnp.zeros((8, QBLK), F32)
        if ncols < nc:
            impt_ref[8 + ncols:8 + nc, :] = jnp.zeros((nc - ncols, QBLK), F32)

        scores = []
        for kh in range(C_KV_HEADS):
            mp, e = divmod(kh, 2)
            cl = slice(mp * LANES, (mp + 1) * LANES)
            half = first_half if e == 0 else jnp.logical_not(first_half)
            qst = _stack_group_queries(q_ref, mp, half)
            sc = lax.dot_general(qst, kc_ref[0:ncols, cl], _NT, preferred_element_type=F32)
            scores.append(jnp.where(cvalid, sc, NEG_INF))

        imps = []
        for kh in range(C_KV_HEADS):
            mp, e = divmod(kh, 2)
            cl = slice(mp * LANES, (mp + 1) * LANES)
            sc = scores[kh]
            mx = jnp.max(sc, axis=-1, keepdims=True)
            mx = jnp.where(mx > NEG_INF, mx, 0.0)
            ee = jnp.exp2(sc - mx)
            pc = ee / jnp.maximum(jnp.sum(ee, axis=-1, keepdims=True), 1e-30)
            ocmp = jnp.dot(pc.astype(BF16), vc_ref[0:ncols, cl], preferred_element_type=F32)
            _store_group_heads(ocmp_ref, ocmp, mp, e, first_half)
            imp = pc[0:QBLK]
            for g in range(1, C_GROUP):
                imp = imp + pc[g * QBLK:(g + 1) * QBLK]
            imps.append(imp)

        for kh in range(C_KV_HEADS):
            for c in range(ncols // QBLK):
                impt_ref[8 + c * QBLK:8 + (c + 1) * QBLK, :] = imps[kh][:, c * QBLK:(c + 1) * QBLK].T
            imp_s = C_SEL_OVERLAP[0] * impt_ref[pl.ds(7, ns, stride=SEL_PER_CMP), :]
            for o_off in range(1, len(C_SEL_OVERLAP)):
                imp_s = imp_s + C_SEL_OVERLAP[o_off] * impt_ref[pl.ds(7 + o_off, ns, stride=SEL_PER_CMP), :]
            score = jnp.where(forced, NEG_INF, jnp.where(bvalid, imp_s, -1.0))
            selt = jnp.where(forced, 1.0, 0.0)
            for _ in range(n_free):
                best = jnp.max(score, axis=0, keepdims=True)
                first = jnp.min(jnp.where(score == best, jj, ns), axis=0, keepdims=True)
                hit = jj == first
                selt = jnp.where(hit, 1.0, selt)
                score = jnp.where(hit, NEG_INF, score)
            sel_ref[kh, 0:ns, :] = selt
            if ns < LANES:
                sel_ref[kh, ns:LANES, :] = jnp.zeros((LANES - ns, QBLK), F32)

    col_step = min(CMP_COL_STEP, nc)
    n_variants = nc // col_step
    tokens_per_step = col_step * C_CMP_STRIDE
    for k in range(n_variants):
        pl.when(t0 // tokens_per_step == k)(functools.partial(body, (k + 1) * col_step))


def _nsa_slc_kernel(q_ref, sel_ref, ks_ref, vst_ref, oslc_ref, acc_ref):
    gq = C_GROUP * QBLK
    blocks_per_tile = SEL_TILE // C_SEL_LEN
    bias_rows = 16
    t0 = pl.program_id(1) * QBLK
    lane = lax.broadcasted_iota(jnp.int32, (QBLK, LANES), 1)
    first_half = lane < HEAD_DIM
    row = lax.broadcasted_iota(jnp.int32, (LANES, QBLK), 0)
    top_rows = row < HEAD_DIM
    n_full = t0 // SEL_TILE
    key_in_tile = lax.broadcasted_iota(jnp.int32, (SEL_TILE, gq), 0)
    query_pos = t0 + lax.broadcasted_iota(jnp.int32, (SEL_TILE, gq), 1) % QBLK
    block_of_key = lax.broadcasted_iota(jnp.int32, (SEL_TILE, LANES), 0) // C_SEL_LEN
    block_onehot = jnp.where(
        block_of_key == lax.broadcasted_iota(jnp.int32, (SEL_TILE, LANES), 1), 1.0, 0.0).astype(BF16)
    bias_pad = jnp.zeros((LANES - bias_rows, gq), BF16)

    q_t = [q_ref[:, pr * LANES:(pr + 1) * LANES].astype(F32).T for pr in range(N_PAIRS)]
    qts = []
    for kh in range(C_KV_HEADS):
        mp, e = divmod(kh, 2)
        keep_rows = top_rows if e == 0 else jnp.logical_not(top_rows)
        qts.append(jnp.concatenate(
            [jnp.where(keep_rows, q_t[C_GROUP * mp + i], 0).astype(BF16) for i in range(C_GROUP)],
            axis=1))
    acc_ref[...] = jnp.zeros((C_KV_HEADS, LANES, gq), F32)

    def tile_step(kt, carry, diagonal):
        k0 = pl.multiple_of(kt * SEL_TILE, SEL_TILE)
        b0 = pl.multiple_of(kt * blocks_per_tile, blocks_per_tile)
        sts = []
        for kh in range(C_KV_HEADS):
            mp = kh // 2
            cl = slice(mp * LANES, (mp + 1) * LANES)
            keys = jnp.concatenate([ks_ref[pl.ds(k0, SEL_TILE), cl], block_onehot], axis=1)
            picked = sel_ref[kh, pl.ds(b0, blocks_per_tile), :]
            bias = jnp.concatenate([(picked - 1.0) * -MASKED] * C_GROUP, axis=1)
            bias = jnp.concatenate([bias, jnp.zeros((bias_rows - blocks_per_tile, gq), F32)], axis=0)
            queries = jnp.concatenate([qts[kh], bias.astype(BF16), bias_pad], axis=0)
            st = jnp.dot(keys, queries, preferred_element_type=F32)
            if diagonal:
                st = jnp.where(k0 + key_in_tile <= query_pos, st, MASKED)
            sts.append(st)
        new_carry = []
        for kh in range(C_KV_HEADS):
            mp = kh // 2
            cl = slice(mp * LANES, (mp + 1) * LANES)
            st = sts[kh]
            m_old = carry[kh]
            m_new = jnp.maximum(m_old, jnp.max(st, axis=0, keepdims=True))
            alpha = jnp.exp2(m_old - m_new)
            p = jnp.exp2(st - m_new)
            acc_ref[kh] = alpha * acc_ref[kh] + jnp.dot(
                vst_ref[kh % 2, kt, cl, :], p.astype(BF16), preferred_element_type=F32)
            new_carry += [m_new]
        return tuple(new_carry)

    init = (jnp.full((1, gq), NEG_INF, F32),) * C_KV_HEADS
    carry = lax.fori_loop(0, n_full, functools.partial(tile_step, diagonal=False), init)
    carry = lax.fori_loop(n_full, n_full + 1, functools.partial(tile_step, diagonal=True), carry)
    for mp in range(C_KV_HEADS // 2):
        ots = [acc_ref[2 * mp + e] / acc_ref[2 * mp + e, HEAD_DIM * (1 - e):HEAD_DIM * (1 - e) + 1, :]
               for e in range(2)]
        for i in range(C_GROUP):
            ol = slice((C_GROUP * mp + i) * LANES, (C_GROUP * mp + i + 1) * LANES)
            qs = slice(i * QBLK, (i + 1) * QBLK)
            oslc_ref[:, ol] = jnp.where(first_half, ots[0][:, qs].T, ots[1][:, qs].T)


def _nsa_select(q_arr, kcmp, vcmp, ks_arr, vst_arr, *, ks_off):
    b, s, _ = q_arr.shape
    nc = kcmp.shape[1]
    ns = nc // SEL_PER_CMP
    assert ns <= LANES and s % SEL_TILE == 0
    qw = N_PAIRS * LANES
    gq = C_GROUP * QBLK
    q_spec = pl.BlockSpec((None, QBLK, qw), lambda bi, i: (bi, i, 0))
    o_spec = pl.BlockSpec((None, QBLK, qw), lambda bi, i: (bi, i, 0))
    sel_spec = pl.BlockSpec((None, C_KV_HEADS, LANES, QBLK), lambda bi, i: (bi, 0, 0, i))
    o_shape = jax.ShapeDtypeStruct((b, s, qw), F32)
    o_cmp, sel = pl.pallas_call(
        functools.partial(_nsa_cmp_kernel, n_sel=min(C_N_SEL, ns)),
        grid=(b, s // QBLK),
        in_specs=[q_spec,
                  pl.BlockSpec((None, nc, C_KV), lambda bi, i: (bi, 0, 0)),
                  pl.BlockSpec((None, nc, C_KV), lambda bi, i: (bi, 0, 0))],
        out_specs=[o_spec, sel_spec],
        out_shape=[o_shape, jax.ShapeDtypeStruct((b, C_KV_HEADS, LANES, s), F32)],
        scratch_shapes=[pltpu.VMEM((8 + nc, QBLK), F32)],
        compiler_params=_params(2),
        name="nsa_compressed",
    )(q_arr, kcmp, vcmp)
    o_slc = pl.pallas_call(
        _nsa_slc_kernel,
        grid=(b, s // QBLK),
        in_specs=[q_spec,
                  sel_spec,
                  pl.BlockSpec((None, s, C_KV), lambda bi, i: (bi, 0, ks_off)),
                  pl.BlockSpec((None, 2, s // SEL_TILE, C_KV, SEL_TILE), lambda bi, i: (bi, 0, 0, 0, 0))],
        out_specs=o_spec,
        out_shape=o_shape,
        scratch_shapes=[pltpu.VMEM((C_KV_HEADS, LANES, gq), F32)],
        compiler_params=_params(2),
        name="nsa_selected",
    )(q_arr, sel, ks_arr, vst_arr)
    return o_cmp, o_slc


def _head_cols(tile, col, first_half):
    tm = tile.shape[0]
    a = jnp.broadcast_to(tile[:, col:col + 1], (tm, LANES))
    b = jnp.broadcast_to(tile[:, col + 1:col + 2], (tm, LANES))
    return jnp.where(first_half, a, b)


def _post_kernel(*refs, kind, n_staged, n_dil_outs):
    it = iter(refs)
    n_branch = {"A": 3, "B": 1, "C": 3}[kind]
    o_refs = [next(it) for _ in range(n_branch)]
    aux_refs = [next(it) for _ in range(3)] if kind == "A" else []
    x_ref, xb_ref, p_ref, wz_ref = (next(it) for _ in range(4))
    wgl_ref = next(it) if kind == "C" else None
    wo_ref, g_ref, b_ref, wg_ref, wp_ref, out_ref, outb_ref = (next(it) for _ in range(7))
    dil_out_refs = [next(it) for _ in range(n_dil_outs)]
    u_ref, z_ref = next(it), next(it)
    stage_refs = [next(it) for _ in range(n_staged + (1 if n_dil_outs else 0))]
    tm = x_ref.shape[0]

    def token_order(ref):
        if len(ref.shape) == 2:
            return lambda j: ref[:, j * LANES:(j + 1) * LANES]
        dil = ref.shape[0]
        stage = stage_refs.pop(0)
        for j in range(ref.shape[2] // LANES):
            for r in range(dil):
                stage[j, pl.ds(r, tm // dil, stride=dil), :] = ref[r, :, j * LANES:(j + 1) * LANES]
        return lambda j: stage[j]

    o_cols = [token_order(r) for r in o_refs]
    aux_cols = [token_order(r) for r in aux_refs]

    n_chunks = tm // POST_ROWS
    rows = [slice(c * POST_ROWS, (c + 1) * POST_ROWS) for c in range(n_chunks)]
    first_half = lax.broadcasted_iota(jnp.int32, (POST_ROWS, LANES), 1) < HEAD_DIM
    gates = [None] * n_chunks
    for c, rs in enumerate(rows):
        xb = xb_ref[rs, :]
        z_ref[rs, :] = jnp.dot(xb, wz_ref[...], preferred_element_type=F32)
        if kind == "C":
            gates[c] = jax.nn.sigmoid(jnp.dot(xb, wgl_ref[...], preferred_element_type=F32))

    hs = []
    for c, rs in enumerate(rows):
        if kind == "A":
            lses = [col(0)[rs] for col in aux_cols]
            mx = jnp.maximum(jnp.maximum(lses[0], lses[1]), lses[2])
            ws = [jnp.exp2(v - mx) for v in lses]
            den = ws[0] + ws[1] + ws[2]
            ws = [v / den for v in ws]
        for pi in range(N_PAIRS):
            cl = slice(pi * LANES, (pi + 1) * LANES)
            if kind == "A":
                o = _head_cols(ws[0], 2 * pi, first_half) * o_cols[0](pi)[rs]
                for g in range(1, 3):
                    o = o + _head_cols(ws[g], 2 * pi, first_half) * o_cols[g](pi)[rs]
            elif kind == "B":
                o = o_cols[0](pi)[rs]
            else:
                o = _head_cols(gates[c], 2 * pi, first_half) * o_cols[0](pi)[rs]
                for br in range(1, 3):
                    o = o + _head_cols(gates[c], br * N_HEADS + 2 * pi, first_half) * o_cols[br](pi)[rs]
            z = z_ref[rs, cl]
            u_ref[rs, cl] = (o * (z * jax.nn.sigmoid(z))).astype(BF16)
        hs.append(jnp.dot(u_ref[rs, :], wo_ref[...], preferred_element_type=F32))

    pre = []
    for c, rs in enumerate(rows):
        y = DEEPNORM_ALPHA * x_ref[rs, :] + hs[c]
        mu = jnp.mean(y, axis=-1, keepdims=True)
        yc = y - mu
        var = jnp.mean(yc * yc, axis=-1, keepdims=True)
        yn = yc * lax.rsqrt(var + LN_EPS) * g_ref[...] + b_ref[...]
        gate_logits = jnp.dot(yn.astype(BF16), wg_ref[...], preferred_element_type=F32)
        pp = jnp.dot(p_ref[rs, :].astype(BF16), wp_ref[...], preferred_element_type=F32)
        pre.append((yn, gate_logits, pp))

    for c, rs in enumerate(rows):
        yn, gate_logits, pp = pre[c]
        x_new = yn + jax.nn.sigmoid(gate_logits) * pp
        out_ref[rs, :] = x_new
        outb_ref[rs, :] = x_new.astype(BF16)
        if dil_out_refs:
            for j in range(x_new.shape[1] // LANES):
                stage_refs[-1][j, rs, :] = x_new[:, j * LANES:(j + 1) * LANES]
    if dil_out_refs:
        stage = stage_refs[-1]
        for o_ref in dil_out_refs:
            dil = o_ref.shape[0]
            for j in range(x_new.shape[1] // LANES):
                for r in range(dil):
                    o_ref[r, :, j * LANES:(j + 1) * LANES] = stage[j, pl.ds(r, tm // dil, stride=dil), :].astype(BF16)


def _post(kind, o_list, aux_list, x, xb, p, w_z, w_gl, w_out, ln_g, ln_b, w_gate, w_proj, batch, out_dils=()):
    t, d = x.shape
    tm = POST_TILE
    row = lambda w: pl.BlockSpec((tm, w), lambda i: (i, 0))
    full = lambda a: pl.BlockSpec(a.shape, lambda i: (0,) * a.ndim)

    branch_specs, stages = [], []
    for a in list(o_list) + list(aux_list):
        if a.ndim == 2:
            branch_specs.append(row(a.shape[1]))
        else:
            _, dil, sub_len, w = a.shape
            n_seq_tiles = sub_len * dil // tm
            branch_specs.append(pl.BlockSpec(
                (None, dil, tm // dil, w), lambda i, n=n_seq_tiles: (i // n, 0, i % n, 0)))
            stages.append(pltpu.VMEM((w // LANES, tm, LANES), F32))
    n_staged = len(stages)
    if out_dils:
        stages.append(pltpu.VMEM((d // LANES, tm, LANES), F32))
    seq_tiles = t // batch // tm
    weights = [w_z] + ([w_gl] if kind == "C" else []) + [w_out, ln_g, ln_b, w_gate, w_proj]
    args = list(o_list) + list(aux_list) + [x, xb, p] + weights
    in_specs = branch_specs + [row(d), row(d), row(p.shape[1])] + [full(w) for w in weights]
    return pl.pallas_call(
        functools.partial(_post_kernel, kind=kind, n_staged=n_staged, n_dil_outs=len(out_dils)),
        grid=(t // tm,),
        in_specs=in_specs,
        out_specs=[row(d), row(d)] + [
            pl.BlockSpec((None, dil, tm // dil, d), lambda i, n=seq_tiles: (i // n, 0, i % n, 0)) for dil in out_dils],
        out_shape=[jax.ShapeDtypeStruct((t, d), F32), jax.ShapeDtypeStruct((t, d), BF16)] + [
            jax.ShapeDtypeStruct((batch, dil, t // batch // dil, d), BF16) for dil in out_dils],
        scratch_shapes=[pltpu.VMEM((tm, d), BF16), pltpu.VMEM((tm, d), F32)] + stages,
        compiler_params=_params(1),
        name=f"post_{kind}",
    )(*args)


def _cast_kernel(x_ref, xb_ref, *rest):
    dil_out_refs, stage = rest[:-1], rest[-1]
    tm, d = x_ref.shape
    x = x_ref[...]
    xb_ref[...] = x.astype(BF16)
    if dil_out_refs:
        for j in range(d // LANES):
            stage[j] = x[:, j * LANES:(j + 1) * LANES]
        for o_ref in dil_out_refs:
            dil = o_ref.shape[0]
            for j in range(d // LANES):
                for r in range(dil):
                    o_ref[r, :, j * LANES:(j + 1) * LANES] = stage[j, pl.ds(r, tm // dil, stride=dil), :].astype(BF16)


def _cast_stream(x, batch, out_dils):
    t, d = x.shape
    tm = POST_TILE
    seq_tiles = t // batch // tm
    row = pl.BlockSpec((tm, d), lambda i: (i, 0))
    return pl.pallas_call(
        _cast_kernel,
        grid=(t // tm,),
        in_specs=[row],
        out_specs=[row] + [
            pl.BlockSpec((None, dil, tm // dil, d), lambda i, n=seq_tiles: (i // n, 0, i % n, 0)) for dil in out_dils],
        out_shape=[jax.ShapeDtypeStruct((t, d), BF16)] + [
            jax.ShapeDtypeStruct((batch, dil, t // batch // dil, d), BF16) for dil in out_dils],
        scratch_shapes=[pltpu.VMEM((d // LANES, tm, LANES), F32)],
        compiler_params=_params(1),
        name="cast_stream",
    )(x)


def _rope_tables(seq_len):
    inv = 1.0 / (ROPE_THETA ** (jnp.arange(0, HEAD_DIM, 2, dtype=F32) / HEAD_DIM))
    ang = jnp.arange(seq_len, dtype=F32)[:, None] * inv[None, :]
    cos, sin = jnp.cos(ang), jnp.sin(ang)
    zero = jnp.zeros_like(sin)
    cos_t = jnp.concatenate([cos] * 4, axis=1)
    sa_t = jnp.concatenate([-sin, zero, -sin, zero], axis=1)
    sb_t = jnp.concatenate([zero, sin, zero, sin], axis=1)
    return cos_t, sa_t, sb_t


def _residue_major(a, dil):
    *lead, s, w = a.shape
    return jnp.swapaxes(a.reshape(*lead, s // dil, dil, w), -3, -2)


def _head_cols_index(head_order):
    return np.concatenate([np.arange(h * HEAD_DIM, (h + 1) * HEAD_DIM) for h in head_order])


_B_HEAD_ORDER = [e * (N_HEADS // B_KV_HEADS) + i for i in range(N_PAIRS) for e in range(2)]
_C_HEAD_ORDER = [C_GROUP * (2 * m + e) + i for m in range(C_KV_HEADS // 2) for i in range(C_GROUP) for e in range(2)]


def _prep_a_weights(a_w_in):
    wd = N_HEADS * HEAD_DIM
    col = np.arange(a_w_in.shape[-1])
    is_q = (col < 3 * len(A_GROUPS) * wd) & (col // wd % 3 == 0)
    scale = np.where(is_q, HEAD_DIM ** -0.5 * LOG2E, 1.0).astype(np.float32)
    return (a_w_in * scale).astype(BF16)


def _mixer_a(xbs, b, s, a_w, j, tabs):
    wd = N_HEADS * HEAD_DIM
    outs, lses = [], []
    for gi, (window, dil) in enumerate(A_GROUPS):
        qkv = _proj(xbs[dil], a_w, BF16, [_residue_major(t, dil) for t in tabs],
                    n_rope_cols=2 * wd, w_cols=(j, 3 * gi * wd, 3 * wd))
        o, lse = _banded_attention(
            qkv, qkv, qkv, tq=min(512, s // dil), npv=1, max_dist=window // dil,
            q_off=0, k_off=1, v_off=2, kv_pairs=N_PAIRS, kv_pair_of=lambda pi: pi, want_lse=True)
        if dil == 1:
            o, lse = o.reshape(b * s, wd), lse.reshape(b * s, LANES)
        outs.append(o)
        lses.append(lse)
    return outs, lses, a_w[j, :, -wd:]


def _mixer_b(xbs, b, s, w_in, sinks, tabs):
    wd = N_HEADS * HEAD_DIM
    kvw = B_KV_HEADS * HEAD_DIM
    perm = _head_cols_index(_B_HEAD_ORDER)
    wq = w_in[:, :wd][:, perm] * (HEAD_DIM ** -0.5 * LOG2E)
    wk = w_in[:, wd:wd + kvw]
    wv = w_in[:, wd + kvw:wd + 2 * kvw]
    wz = w_in[:, wd + 2 * kvw:][:, perm]
    tabs1 = [t[None] for t in tabs]
    qk = _proj(xbs[1], jnp.concatenate([wq, wk], axis=1).astype(BF16), BF16, tabs1, n_rope_cols=wd + kvw)
    v = _proj(xbs[1], wv.astype(BF16), BF16)
    o = _banded_attention(
        qk, qk, v, tq=min(256, s), npv=1, max_dist=B_WINDOW - 1,
        q_off=0, k_off=wd // kvw, v_off=0, kv_pairs=1, kv_pair_of=lambda pi: 0,
        sinks=sinks[np.asarray(_B_HEAD_ORDER)].astype(F32) * LOG2E)
    return [o.reshape(b * s, wd)], [], wz, perm


def _mixer_c(xbs, b, s, w_in, w_ck, w_cv, pos, tabs):
    wd = N_HEADS * HEAD_DIM
    perm = _head_cols_index(_C_HEAD_ORDER)
    cols = np.cumsum([0, wd] + [C_KV] * 6 + [3 * N_HEADS, wd])
    part = lambda i: w_in[:, cols[i]:cols[i + 1]]
    wq = part(0)[:, perm] * (HEAD_DIM ** -0.5 * LOG2E)
    w_gl = part(7).reshape(-1, 3, N_HEADS)[:, :, np.asarray(_C_HEAD_ORDER)].reshape(-1, 3 * N_HEADS)
    w_gl = jnp.pad(w_gl, ((0, 0), (0, LANES - 3 * N_HEADS)))
    w_att = jnp.concatenate([wq, part(3), part(5), part(4), part(6)], axis=1)
    tabs1 = [t[None] for t in tabs]
    att = _proj(xbs[1], w_att.astype(BF16), BF16, tabs1, n_rope_cols=wd + 2 * C_KV)
    cmp_in = _proj(xbs[1], jnp.concatenate([part(1), part(2)], axis=1).astype(BF16), F32, tabs1,
                   n_rope_cols=C_KV, tn=C_KV).reshape(b, s, 2 * C_KV)

    nc = s // C_CMP_STRIDE
    cw = C_CMP_STRIDE * HEAD_DIM

    def chunks(t):
        return jnp.transpose(t.reshape(b, s, C_KV_HEADS, HEAD_DIM), (0, 2, 1, 3)).reshape(b, C_KV_HEADS, nc, cw)

    pos2 = pos.reshape(2, cw)

    def compressed(t, w):
        c = _compress(chunks(t), pos2, w.reshape(2, cw, HEAD_DIM).astype(BF16))
        return jnp.transpose(c, (0, 2, 1, 3)).reshape(b, nc, C_KV)

    kcmp, vcmp = compressed(cmp_in[:, :, :C_KV], w_ck), compressed(cmp_in[:, :, C_KV:], w_cv)
    att3 = att.reshape(b, s, -1)
    vs = att3[:, :, wd + 2 * C_KV:wd + 3 * C_KV]
    vst = jnp.transpose(vs.reshape(b, s // SEL_TILE, SEL_TILE, C_KV), (0, 1, 3, 2))
    in_half0 = (jnp.arange(C_KV) % PAIR < HEAD_DIM)[None, None, :, None]
    vst = jnp.stack([jnp.where(in_half0, vst, 1), jnp.where(in_half0, 1, vst)], axis=1)
    o_cmp, o_slc = _nsa_select(att3, kcmp, vcmp, att3, vst, ks_off=wd // C_KV)
    o_win = _banded_attention(
        att, att, att, tq=C_WINDOW, npv=C_WINDOW // QBLK, max_dist=C_WINDOW - 1,
        q_off=0, k_off=wd // C_KV + 1, v_off=wd // C_KV + 3, kv_pairs=C_KV_HEADS // 2,
        kv_pair_of=lambda pi: pi // C_GROUP)
    outs = [o.reshape(b * s, wd) for o in (o_cmp, o_slc, o_win)]
    return outs, [], (part(8)[:, perm], w_gl), perm


def _layer(i, xt, xbs, b, s, tabs, p, a_w_in, a_w_out, b_w_in, b_sinks, b_w_out, c_w_in, c_w_ck, c_w_cv, c_pos,
           c_w_out, ln_g, ln_b, ple_w_proj, ple_w_gate):
    d = xt.shape[1]
    j, kind = divmod(i, N_MIXERS)
    w_gl = None
    if kind == 0:
        outs, aux, w_z = _mixer_a(xbs, b, s, a_w_in, j, tabs)
        w_out, name = a_w_out[j], "A"
    elif kind == 1:
        outs, aux, w_z, perm = _mixer_b(xbs, b, s, b_w_in[j], b_sinks[j], tabs)
        w_out, name = b_w_out[j][perm, :], "B"
    else:
        outs, aux, (w_z, w_gl), perm = _mixer_c(xbs, b, s, c_w_in[j], c_w_ck[j], c_w_cv[j], c_pos[j], tabs)
        w_out, name = c_w_out[j][perm, :], "C"
        w_gl = w_gl.astype(BF16)
    next_dils = _layer_dils(i + 1)
    res = _post(name, outs, aux, xt, xbs[1].reshape(b * s, d), p[i].reshape(b * s, -1), w_z.astype(BF16), w_gl,
                w_out.astype(BF16), ln_g[i].reshape(1, d), ln_b[i].reshape(1, d),
                ple_w_gate[i].astype(BF16), ple_w_proj[i].astype(BF16), b, out_dils=next_dils)
    new_xbs = {1: res[1].reshape(b, 1, s, d)}
    new_xbs.update(zip(next_dils, res[2:]))
    return res[0], new_xbs


def _layer_dils(i):
    if i < DEPTH and i % N_MIXERS == 0:
        return tuple(dil for _, dil in A_GROUPS if dil > 1)
    return ()


def kernel(x, p, a_w_in, a_w_out, b_w_in, b_sinks, b_w_out, c_w_in, c_w_ck, c_w_cv, c_pos, c_w_out,
           ln_g, ln_b, ple_w_proj, ple_w_gate):
    b, s, d = x.shape
    assert d == N_HEADS * HEAD_DIM and s % (QBLK * A_GROUPS[-1][1]) == 0 and s % C_WINDOW == 0
    tabs = _rope_tables(s)
    xt = x.reshape(b * s, d)
    dils0 = _layer_dils(0)
    copies = _cast_stream(xt, b, dils0)
    xbs = {1: copies[0].reshape(b, 1, s, d)}
    xbs.update(zip(dils0, copies[1:]))
    a_w_in = _prep_a_weights(a_w_in)
    for i in range(DEPTH):
        xt, xbs = _layer(i, xt, xbs, b, s, tabs, p, a_w_in, a_w_out, b_w_in, b_sinks, b_w_out, c_w_in, c_w_ck,
                         c_w_cv, c_pos, c_w_out, ln_g, ln_b, ple_w_proj, ple_w_gate)
    return xt.reshape(b, s, d)
```

```python
import functools

import numpy as np
import jax
import jax.numpy as jnp
from jax import lax
from jax.experimental import pallas as pl
from jax.experimental.pallas import tpu as pltpu

F32 = jnp.float32
BF16 = jnp.bfloat16

LANES = 128
VMEM_LIMIT_BYTES = 56 * 1024 * 1024

HEAD_DIM = 64
HALF = HEAD_DIM // 2
PAIR = 2 * HEAD_DIM
assert PAIR == LANES
N_HEADS = 16
N_PAIRS = N_HEADS // 2
ROPE_THETA = 10000.0
QBLK = 128
BAND_GROUP = 4
PROJ_ROWS = 128
POST_TILE = 512
POST_ROWS = 256
LN_EPS = 1e-5
DEPTH = 4
N_MIXERS = 3
DEEPNORM_ALPHA = (2 * DEPTH) ** 0.25
A_GROUPS = ((128, 1), (512, 4), (2048, 16))
B_KV_HEADS = 2
B_WINDOW = 128
C_KV_HEADS = 4
C_GROUP = N_HEADS // C_KV_HEADS
C_KV = C_KV_HEADS * HEAD_DIM
C_CMP_STRIDE = 16
C_CMP_LEN = 32
C_SEL_LEN = 64
C_N_SEL = 16
C_WINDOW = 512
C_SEL_OVERLAP = (1.0, 2.0, 2.0, 2.0, 1.0)
SEL_PER_CMP = C_SEL_LEN // C_CMP_STRIDE
SEL_TILE = 512
CMP_COL_STEP = 128
LOG2E = 1.4426950408889634
NEG_INF = float("-inf")
MASKED = -1e30

_NT = (((1,), (1,)), ((), ()))


def _params(n_grid):
    return pltpu.CompilerParams(
        dimension_semantics=("arbitrary",) * n_grid, vmem_limit_bytes=VMEM_LIMIT_BYTES)


def _proj_kernel(*refs, n_rope_tiles, n_tiles):
    it = iter(refs)
    x_ref, w_ref = next(it), next(it)
    if n_rope_tiles:
        tab_refs = [next(it), next(it), next(it)]
    o_ref = next(it)
    dil, sub, _ = x_ref.shape
    tn = w_ref.shape[1]
    n_chunks = dil * sub // PROJ_ROWS

    def chunk(ref, m, cols=slice(None)):
        if sub >= PROJ_ROWS:
            r, l0 = divmod(m * PROJ_ROWS, sub)
            return ref.at[r, l0:l0 + PROJ_ROWS, cols]
        k = PROJ_ROWS // sub
        return ref.at[m * k:(m + 1) * k, :, cols]

    def emit(rope):
        def matmul(m):
            xm = chunk(x_ref, m)[...].reshape(PROJ_ROWS, x_ref.shape[2])
            return jnp.dot(xm, w_ref[...], preferred_element_type=F32)

        def finish(m, acc):
            if rope:
                c, sa, sb = (chunk(t, m)[...].reshape(PROJ_ROWS, LANES) for t in tab_refs)
            for j in range(tn // LANES):
                cl = slice(j * LANES, (j + 1) * LANES)
                t = acc[:, cl]
                if rope:
                    t = t * c + pltpu.roll(t, LANES - HALF, 1) * sa + pltpu.roll(t, HALF, 1) * sb
                dst = chunk(o_ref, m, cl)
                dst[...] = t.astype(o_ref.dtype).reshape(dst.shape)

        acc = matmul(0)
        for m in range(n_chunks):
            nxt = matmul(m + 1) if m + 1 < n_chunks else None
            finish(m, acc)
            acc = nxt

    if n_rope_tiles == 0 or n_rope_tiles == n_tiles:
        emit(n_rope_tiles > 0)
    else:
        pl.when(pl.program_id(1) < n_rope_tiles)(lambda: emit(True))
        pl.when(pl.program_id(1) >= n_rope_tiles)(lambda: emit(False))


def _pick_tile(n, candidates):
    for c in candidates:
        if n % c == 0:
            return c
    raise ValueError(f"no tile for {n}")


def _proj(x, w, out_dtype, rope_tabs=None, n_rope_cols=0, tn=None, w_cols=None):
    batch, dil, sub_len, k = x.shape
    layer, col0, n = w_cols if w_cols is not None else (None, 0, w.shape[-1])
    seq_len = dil * sub_len
    tm = _pick_tile(seq_len, (2048, 1024, 512, 256, 128))
    tn = tn or _pick_tile(np.gcd(n, n_rope_cols), (1024, 512, 384, 256, 128))
    sub = tm // dil
    assert n % tn == 0 and n_rope_cols % tn == 0 and tm % PROJ_ROWS == 0
    assert sub % 16 == 0 and (sub % PROJ_ROWS == 0 or PROJ_ROWS % sub == 0)
    n_seq_tiles = seq_len // tm
    assert col0 % tn == 0
    w_spec = (pl.BlockSpec((k, tn), lambda i, j: (0, j)) if layer is None else
              pl.BlockSpec((None, k, tn), lambda i, j: (layer, 0, j + col0 // tn)))
    in_specs = [pl.BlockSpec((None, dil, sub, k), lambda i, j: (i // n_seq_tiles, 0, i % n_seq_tiles, 0)), w_spec]
    args = [x, w]
    if n_rope_cols:
        tab_spec = pl.BlockSpec((dil, sub, LANES), lambda i, j: (0, i % n_seq_tiles, 0))
        in_specs += [tab_spec] * 3
        args += list(rope_tabs)
    return pl.pallas_call(
        functools.partial(_proj_kernel, n_rope_tiles=n_rope_cols // tn, n_tiles=n // tn),
        grid=(batch * n_seq_tiles, n // tn),
        in_specs=in_specs,
        out_specs=pl.BlockSpec((None, dil, sub, tn), lambda i, j: (i // n_seq_tiles, 0, i % n_seq_tiles, j)),
        out_shape=jax.ShapeDtypeStruct((batch, dil, sub_len, n), out_dtype),
        compiler_params=_params(2),
        name=f"proj_d{dil}_r{n_rope_cols}",
    )(*args)


def _band_kernel(*refs, tq, npv, max_dist, kv_pair_of, has_sinks, want_lse):
    it = iter(refs)
    q_ref, kc_ref, kp_ref, vc_ref, vp_ref = (next(it) for _ in range(5))
    sink_ref = next(it) if has_sinks else None
    o_ref = next(it)
    lse_ref = next(it) if want_lse else None
    qb = pl.program_id(2)
    w = (npv + 1) * QBLK
    pv_rows = npv * QBLK
    lane = lax.broadcasted_iota(jnp.int32, (QBLK, LANES), 1)
    first_half = lane < HEAD_DIM
    qi = lax.broadcasted_iota(jnp.int32, (QBLK, w), 0)
    kj = lax.broadcasted_iota(jnp.int32, (QBLK, w), 1)
    dist = qi + npv * QBLK - kj
    band = (dist >= 0) & (dist <= max_dist)
    for sub in range(tq // QBLK):
        r0 = sub * QBLK
        kstart = qb * tq + r0 - npv * QBLK
        mask = band & (kj + kstart >= 0)
        lse_tile = jnp.zeros((QBLK, LANES), F32)
        for g0 in range(0, N_PAIRS, BAND_GROUP):
            staged = []
            for pi in range(g0, g0 + BAND_GROUP):
                cl = slice(kv_pair_of(pi) * LANES, (kv_pair_of(pi) + 1) * LANES)
                k_parts, v_parts = [], []
                if r0 < pv_rows:
                    k_parts.append(kp_ref[r0:pv_rows, cl])
                    v_parts.append(vp_ref[r0:pv_rows, cl])
                cs = max(r0 - pv_rows, 0)
                k_parts.append(kc_ref[cs:r0 + QBLK, cl])
                v_parts.append(vc_ref[cs:r0 + QBLK, cl])
                kwin = k_parts[0] if len(k_parts) == 1 else jnp.concatenate(k_parts, axis=0)
                vwin = v_parts[0] if len(v_parts) == 1 else jnp.concatenate(v_parts, axis=0)
                qp = q_ref[r0:r0 + QBLK, pi * LANES:(pi + 1) * LANES]
                for e in range(2):
                    qe = jnp.where(first_half if e == 0 else jnp.logical_not(first_half), qp, 0)
                    s = lax.dot_general(qe, kwin, _NT, preferred_element_type=F32)
                    staged.append((pi, e, jnp.where(mask, s, NEG_INF), vwin))
            outs = {}
            for pi, e, s, vwin in staged:
                m = jnp.max(s, axis=-1, keepdims=True)
                if has_sinks:
                    sk = sink_ref[2 * pi + e]
                    m = jnp.maximum(m, sk)
                p = jnp.exp2(s - m)
                l = jnp.sum(p, axis=-1, keepdims=True)
                if has_sinks:
                    l = l + jnp.exp2(sk - m)
                pv = jnp.dot(p.astype(BF16), vwin, preferred_element_type=F32)
                outs[(pi, e)] = pv / l
                if want_lse:
                    lse_tile = jnp.where(lane == 2 * pi + e, m + jnp.log2(l), lse_tile)
            for pi in range(g0, g0 + BAND_GROUP):
                o_ref[r0:r0 + QBLK, pi * LANES:(pi + 1) * LANES] = jnp.where(
                    first_half, outs[(pi, 0)], outs[(pi, 1)])
        if want_lse:
            lse_ref[r0:r0 + QBLK, :] = lse_tile


def _banded_attention(q_arr, k_arr, v_arr, *, tq, npv, max_dist, q_off, k_off, v_off,
                      kv_pairs, kv_pair_of, sinks=None, want_lse=False):
    b, dil, l, _ = q_arr.shape
    qw, kw = N_PAIRS * LANES, kv_pairs * LANES
    pv = npv * QBLK
    assert tq % pv == 0
    in_specs = [
        pl.BlockSpec((None, None, tq, qw), lambda bi, r, i: (bi, r, i, q_off)),
        pl.BlockSpec((None, None, tq, kw), lambda bi, r, i: (bi, r, i, k_off)),
        pl.BlockSpec((None, None, pv, kw), lambda bi, r, i: (bi, r, jnp.maximum(i * (tq // pv) - 1, 0), k_off)),
        pl.BlockSpec((None, None, tq, kw), lambda bi, r, i: (bi, r, i, v_off)),
        pl.BlockSpec((None, None, pv, kw), lambda bi, r, i: (bi, r, jnp.maximum(i * (tq // pv) - 1, 0), v_off)),
    ]
    args = [q_arr, k_arr, k_arr, v_arr, v_arr]
    if sinks is not None:
        in_specs.append(pl.BlockSpec(memory_space=pltpu.SMEM))
        args.append(sinks)
    out_specs = [pl.BlockSpec((None, None, tq, qw), lambda bi, r, i: (bi, r, i, 0))]
    out_shape = [jax.ShapeDtypeStruct((b, dil, l, qw), F32)]
    if want_lse:
        out_specs.append(pl.BlockSpec((None, None, tq, LANES), lambda bi, r, i: (bi, r, i, 0)))
        out_shape.append(jax.ShapeDtypeStruct((b, dil, l, LANES), F32))
    res = pl.pallas_call(
        functools.partial(_band_kernel, tq=tq, npv=npv, max_dist=max_dist, kv_pair_of=kv_pair_of,
                          has_sinks=sinks is not None, want_lse=want_lse),
        grid=(b, dil, l // tq),
        in_specs=in_specs,
        out_specs=out_specs,
        out_shape=out_shape,
        compiler_params=_params(3),
        name=f"band_d{dil}_w{max_dist}",
    )(*args)
    return res if want_lse else res[0]


def _compress_kernel(c_ref, pos_ref, w_ref, o_ref):
    c = c_ref[...]
    top = jnp.dot((c + pos_ref[0:1, :]).astype(BF16), w_ref[0], preferred_element_type=F32)
    bot = jnp.dot((c + pos_ref[1:2, :]).astype(BF16), w_ref[1], preferred_element_type=F32)
    nc = c.shape[0]
    o_ref[...] = (top + pltpu.roll(bot, nc - 1, 0)).astype(o_ref.dtype)


def _compress(chunks, pos, w):
    b, hk, nc, cw = chunks.shape
    return pl.pallas_call(
        _compress_kernel,
        grid=(b, hk),
        in_specs=[pl.BlockSpec((None, None, nc, cw), lambda bi, h: (bi, h, 0, 0)),
                  pl.BlockSpec((2, cw), lambda bi, h: (0, 0)),
                  pl.BlockSpec((2, cw, HEAD_DIM), lambda bi, h: (0, 0, 0))],
        out_specs=pl.BlockSpec((None, None, nc, HEAD_DIM), lambda bi, h: (bi, h, 0, 0)),
        out_shape=jax.ShapeDtypeStruct((b, hk, nc, HEAD_DIM), BF16),
        compiler_params=_params(2),
        name="nsa_compress",
    )(chunks, pos, w)


def _stack_group_queries(q_ref, mp, half):
    pairs = [C_GROUP * mp + i for i in range(C_GROUP)]
    return jnp.concatenate(
        [jnp.where(half, q_ref[:, pr * LANES:(pr + 1) * LANES], 0) for pr in pairs], axis=0)


def _store_group_heads(o_ref, val, mp, e, first_half):
    for i in range(C_GROUP):
        ol = slice((C_GROUP * mp + i) * LANES, (C_GROUP * mp + i + 1) * LANES)
        rows = slice(i * QBLK, (i + 1) * QBLK)
        if e == 0:
            o_ref[:, ol] = val[rows]
        else:
            o_ref[:, ol] = jnp.where(first_half, o_ref[:, ol], val[rows])


def _nsa_cmp_kernel(q_ref, kc_ref, vc_ref, ocmp_ref, sel_ref, impt_ref, *, n_sel):
    nc = kc_ref.shape[0]
    ns = nc // SEL_PER_CMP
    t0 = pl.program_id(1) * QBLK
    lane = lax.broadcasted_iota(jnp.int32, (QBLK, LANES), 1)
    first_half = lane < HEAD_DIM

    jj = lax.broadcasted_iota(jnp.int32, (ns, QBLK), 0)
    cur = (t0 + lax.broadcasted_iota(jnp.int32, (ns, QBLK), 1)) // C_SEL_LEN
    forced = (jj == 0) | (jj == cur) | (jj == cur - 1)
    bvalid = jj <= cur
    n_free = max(n_sel - 3, 0)

    def body(ncols):
        qi_c = lax.broadcasted_iota(jnp.int32, (QBLK, ncols), 0)
        nn_c = lax.broadcasted_iota(jnp.int32, (QBLK, ncols), 1)
        cvalid = nn_c * C_CMP_STRIDE + (C_CMP_LEN - 1) <= t0 + qi_c
        cvalid = jnp.concatenate([cvalid] * C_GROUP, axis=0)
        impt_ref[0:8, :] = jnp.zeros((8, QBLK), F32)
        if ncols < nc:
            impt_ref[8 + ncols:8 + nc, :] = jnp.zeros((nc - ncols, QBLK), F32)

        scores = []
        for kh in range(C_KV_HEADS):
            mp, e = divmod(kh, 2)
            cl = slice(mp * LANES, (mp + 1) * LANES)
            half = first_half if e == 0 else jnp.logical_not(first_half)
            qst = _stack_group_queries(q_ref, mp, half)
            sc = lax.dot_general(qst, kc_ref[0:ncols, cl], _NT, preferred_element_type=F32)
            scores.append(jnp.where(cvalid, sc, NEG_INF))

        imps = []
        for kh in range(C_KV_HEADS):
            mp, e = divmod(kh, 2)
            cl = slice(mp * LANES, (mp + 1) * LANES)
            sc = scores[kh]
            mx = jnp.max(sc, axis=-1, keepdims=True)
            mx = jnp.where(mx > NEG_INF, mx, 0.0)
            ee = jnp.exp2(sc - mx)
            pc = ee / jnp.maximum(jnp.sum(ee, axis=-1, keepdims=True), 1e-30)
            ocmp = jnp.dot(pc.astype(BF16), vc_ref[0:ncols, cl], preferred_element_type=F32)
            _store_group_heads(ocmp_ref, ocmp, mp, e, first_half)
            imp = pc[0:QBLK]
            for g in range(1, C_GROUP):
                imp = imp + pc[g * QBLK:(g + 1) * QBLK]
            imps.append(imp)

        for kh in range(C_KV_HEADS):
            for c in range(ncols // QBLK):
                impt_ref[8 + c * QBLK:8 + (c + 1) * QBLK, :] = imps[kh][:, c * QBLK:(c + 1) * QBLK].T
            imp_s = C_SEL_OVERLAP[0] * impt_ref[pl.ds(7, ns, stride=SEL_PER_CMP), :]
            for o_off in range(1, len(C_SEL_OVERLAP)):
                imp_s = imp_s + C_SEL_OVERLAP[o_off] * impt_ref[pl.ds(7 + o_off, ns, stride=SEL_PER_CMP), :]
            score = jnp.where(forced, NEG_INF, jnp.where(bvalid, imp_s, -1.0))
            selt = jnp.where(forced, 1.0, 0.0)
            for _ in range(n_free):
                best = jnp.max(score, axis=0, keepdims=True)
                first = jnp.min(jnp.where(score == best, jj, ns), axis=0, keepdims=True)
                hit = jj == first
                selt = jnp.where(hit, 1.0, selt)
                score = jnp.where(hit, NEG_INF, score)
            sel_ref[kh, 0:ns, :] = selt
            if ns < LANES:
                sel_ref[kh, ns:LANES, :] = jnp.zeros((LANES - ns, QBLK), F32)

    col_step = min(CMP_COL_STEP, nc)
    n_variants = nc // col_step
    tokens_per_step = col_step * C_CMP_STRIDE
    for k in range(n_variants):
        pl.when(t0 // tokens_per_step == k)(functools.partial(body, (k + 1) * col_step))


def _nsa_slc_kernel(q_ref, sel_ref, ks_ref, vst_ref, oslc_ref, acc_ref):
    gq = C_GROUP * QBLK
    blocks_per_tile = SEL_TILE // C_SEL_LEN
    bias_rows = 16
    t0 = pl.program_id(1) * QBLK
    lane = lax.broadcasted_iota(jnp.int32, (QBLK, LANES), 1)
    first_half = lane < HEAD_DIM
    row = lax.broadcasted_iota(jnp.int32, (LANES, QBLK), 0)
    top_rows = row < HEAD_DIM
    n_full = t0 // SEL_TILE
    key_in_tile = lax.broadcasted_iota(jnp.int32, (SEL_TILE, gq), 0)
    query_pos = t0 + lax.broadcasted_iota(jnp.int32, (SEL_TILE, gq), 1) % QBLK
    block_of_key = lax.broadcasted_iota(jnp.int32, (SEL_TILE, LANES), 0) // C_SEL_LEN
    block_onehot = jnp.where(
        block_of_key == lax.broadcasted_iota(jnp.int32, (SEL_TILE, LANES), 1), 1.0, 0.0).astype(BF16)
    bias_pad = jnp.zeros((LANES - bias_rows, gq), BF16)

    q_t = [q_ref[:, pr * LANES:(pr + 1) * LANES].astype(F32).T for pr in range(N_PAIRS)]
    qts = []
    for kh in range(C_KV_HEADS):
        mp, e = divmod(kh, 2)
        keep_rows = top_rows if e == 0 else jnp.logical_not(top_rows)
        qts.append(jnp.concatenate(
            [jnp.where(keep_rows, q_t[C_GROUP * mp + i], 0).astype(BF16) for i in range(C_GROUP)],
            axis=1))
    acc_ref[...] = jnp.zeros((C_KV_HEADS, LANES, gq), F32)

    def tile_step(kt, carry, diagonal):
        k0 = pl.multiple_of(kt * SEL_TILE, SEL_TILE)
        b0 = pl.multiple_of(kt * blocks_per_tile, blocks_per_tile)
        sts = []
        for kh in range(C_KV_HEADS):
            mp = kh // 2
            cl = slice(mp * LANES, (mp + 1) * LANES)
            keys = jnp.concatenate([ks_ref[pl.ds(k0, SEL_TILE), cl], block_onehot], axis=1)
            picked = sel_ref[kh, pl.ds(b0, blocks_per_tile), :]
            bias = jnp.concatenate([(picked - 1.0) * -MASKED] * C_GROUP, axis=1)
            bias = jnp.concatenate([bias, jnp.zeros((bias_rows - blocks_per_tile, gq), F32)], axis=0)
            queries = jnp.concatenate([qts[kh], bias.astype(BF16), bias_pad], axis=0)
            st = jnp.dot(keys, queries, preferred_element_type=F32)
            if diagonal:
                st = jnp.where(k0 + key_in_tile <= query_pos, st, MASKED)
            sts.append(st)
        new_carry = []
        for kh in range(C_KV_HEADS):
            mp = kh // 2
            cl = slice(mp * LANES, (mp + 1) * LANES)
            st = sts[kh]
            m_old = carry[kh]
            m_new = jnp.maximum(m_old, jnp.max(st, axis=0, keepdims=True))
            alpha = jnp.exp2(m_old - m_new)
            p = jnp.exp2(st - m_new)
            acc_ref[kh] = alpha * acc_ref[kh] + jnp.dot(
                vst_ref[kh % 2, kt, cl, :], p.astype(BF16), preferred_element_type=F32)
            new_carry += [m_new]
        return tuple(new_carry)

    init = (jnp.full((1, gq), NEG_INF, F32),) * C_KV_HEADS
    carry = lax.fori_loop(0, n_full, functools.partial(tile_step, diagonal=False), init)
    carry = lax.fori_loop(n_full, n_full + 1, functools.partial(tile_step, diagonal=True), carry)
    for mp in range(C_KV_HEADS // 2):
        ots = [acc_ref[2 * mp + e] / acc_ref[2 * mp + e, HEAD_DIM * (1 - e):HEAD_DIM * (1 - e) + 1, :]
               for e in range(2)]
        for i in range(C_GROUP):
            ol = slice((C_GROUP * mp + i) * LANES, (C_GROUP * mp + i + 1) * LANES)
            qs = slice(i * QBLK, (i + 1) * QBLK)
            oslc_ref[:, ol] = jnp.where(first_half, ots[0][:, qs].T, ots[1][:, qs].T)


def _nsa_select(q_arr, kcmp, vcmp, ks_arr, vst_arr, *, ks_off):
    b, s, _ = q_arr.shape
    nc = kcmp.shape[1]
    ns = nc // SEL_PER_CMP
    assert ns <= LANES and s % SEL_TILE == 0
    qw = N_PAIRS * LANES
    gq = C_GROUP * QBLK
    q_spec = pl.BlockSpec((None, QBLK, qw), lambda bi, i: (bi, i, 0))
    o_spec = pl.BlockSpec((None, QBLK, qw), lambda bi, i: (bi, i, 0))
    sel_spec = pl.BlockSpec((None, C_KV_HEADS, LANES, QBLK), lambda bi, i: (bi, 0, 0, i))
    o_shape = jax.ShapeDtypeStruct((b, s, qw), F32)
    o_cmp, sel = pl.pallas_call(
        functools.partial(_nsa_cmp_kernel, n_sel=min(C_N_SEL, ns)),
        grid=(b, s // QBLK),
        in_specs=[q_spec,
                  pl.BlockSpec((None, nc, C_KV), lambda bi, i: (bi, 0, 0)),
                  pl.BlockSpec((None, nc, C_KV), lambda bi, i: (bi, 0, 0))],
        out_specs=[o_spec, sel_spec],
        out_shape=[o_shape, jax.ShapeDtypeStruct((b, C_KV_HEADS, LANES, s), F32)],
        scratch_shapes=[pltpu.VMEM((8 + nc, QBLK), F32)],
        compiler_params=_params(2),
        name="nsa_compressed",
    )(q_arr, kcmp, vcmp)
    o_slc = pl.pallas_call(
        _nsa_slc_kernel,
        grid=(b, s // QBLK),
        in_specs=[q_spec,
                  sel_spec,
                  pl.BlockSpec((None, s, C_KV), lambda bi, i: (bi, 0, ks_off)),
                  pl.BlockSpec((None, 2, s // SEL_TILE, C_KV, SEL_TILE), lambda bi, i: (bi, 0, 0, 0, 0))],
        out_specs=o_spec,
        out_shape=o_shape,
        scratch_shapes=[pltpu.VMEM((C_KV_HEADS, LANES, gq), F32)],
        compiler_params=_params(2),
        name="nsa_selected",
    )(q_arr, sel, ks_arr, vst_arr)
    return o_cmp, o_slc


def _head_cols(tile, col, first_half):
    tm = tile.shape[0]
    a = jnp.broadcast_to(tile[:, col:col + 1], (tm, LANES))
    b = jnp.broadcast_to(tile[:, col + 1:col + 2], (tm, LANES))
    return jnp.where(first_half, a, b)


def _post_kernel(*refs, kind, n_staged, n_dil_outs):
    it = iter(refs)
    n_branch = {"A": 3, "B": 1, "C": 3}[kind]
    o_refs = [next(it) for _ in range(n_branch)]
    aux_refs = [next(it) for _ in range(3)] if kind == "A" else []
    x_ref, xb_ref, p_ref, wz_ref = (next(it) for _ in range(4))
    wgl_ref = next(it) if kind == "C" else None
    wo_ref, g_ref, b_ref, wg_ref, wp_ref, out_ref, outb_ref = (next(it) for _ in range(7))
    dil_out_refs = [next(it) for _ in range(n_dil_outs)]
    u_ref, z_ref = next(it), next(it)
    stage_refs = [next(it) for _ in range(n_staged + (1 if n_dil_outs else 0))]
    tm = x_ref.shape[0]

    def token_order(ref):
        if len(ref.shape) == 2:
            return lambda j: ref[:, j * LANES:(j + 1) * LANES]
        dil = ref.shape[0]
        stage = stage_refs.pop(0)
        for j in range(ref.shape[2] // LANES):
            for r in range(dil):
                stage[j, pl.ds(r, tm // dil, stride=dil), :] = ref[r, :, j * LANES:(j + 1) * LANES]
        return lambda j: stage[j]

    o_cols = [token_order(r) for r in o_refs]
    aux_cols = [token_order(r) for r in aux_refs]

    n_chunks = tm // POST_ROWS
    rows = [slice(c * POST_ROWS, (c + 1) * POST_ROWS) for c in range(n_chunks)]
    first_half = lax.broadcasted_iota(jnp.int32, (POST_ROWS, LANES), 1) < HEAD_DIM
    gates = [None] * n_chunks
    for c, rs in enumerate(rows):
        xb = xb_ref[rs, :]
        z_ref[rs, :] = jnp.dot(xb, wz_ref[...], preferred_element_type=F32)
        if kind == "C":
            gates[c] = jax.nn.sigmoid(jnp.dot(xb, wgl_ref[...], preferred_element_type=F32))

    hs = []
    for c, rs in enumerate(rows):
        if kind == "A":
            lses = [col(0)[rs] for col in aux_cols]
            mx = jnp.maximum(jnp.maximum(lses[0], lses[1]), lses[2])
            ws = [jnp.exp2(v - mx) for v in lses]
            den = ws[0] + ws[1] + ws[2]
            ws = [v / den for v in ws]
        for pi in range(N_PAIRS):
            cl = slice(pi * LANES, (pi + 1) * LANES)
            if kind == "A":
                o = _head_cols(ws[0], 2 * pi, first_half) * o_cols[0](pi)[rs]
                for g in range(1, 3):
                    o = o + _head_cols(ws[g], 2 * pi, first_half) * o_cols[g](pi)[rs]
            elif kind == "B":
                o = o_cols[0](pi)[rs]
            else:
                o = _head_cols(gates[c], 2 * pi, first_half) * o_cols[0](pi)[rs]
                for br in range(1, 3):
                    o = o + _head_cols(gates[c], br * N_HEADS + 2 * pi, first_half) * o_cols[br](pi)[rs]
            z = z_ref[rs, cl]
            u_ref[rs, cl] = (o * (z * jax.nn.sigmoid(z))).astype(BF16)
        hs.append(jnp.dot(u_ref[rs, :], wo_ref[...], preferred_element_type=F32))

    pre = []
    for c, rs in enumerate(rows):
        y = DEEPNORM_ALPHA * x_ref[rs, :] + hs[c]
        mu = jnp.mean(y, axis=-1, keepdims=True)
        yc = y - mu
        var = jnp.mean(yc * yc, axis=-1, keepdims=True)
        yn = yc * lax.rsqrt(var + LN_EPS) * g_ref[...] + b_ref[...]
        gate_logits = jnp.dot(yn.astype(BF16), wg_ref[...], preferred_element_type=F32)
        pp = jnp.dot(p_ref[rs, :].astype(BF16), wp_ref[...], preferred_element_type=F32)
        pre.append((yn, gate_logits, pp))

    for c, rs in enumerate(rows):
        yn, gate_logits, pp = pre[c]
        x_new = yn + jax.nn.sigmoid(gate_logits) * pp
        out_ref[rs, :] = x_new
        outb_ref[rs, :] = x_new.astype(BF16)
        if dil_out_refs:
            for j in range(x_new.shape[1] // LANES):
                stage_refs[-1][j, rs, :] = x_new[:, j * LANES:(j + 1) * LANES]
    if dil_out_refs:
        stage = stage_refs[-1]
        for o_ref in dil_out_refs:
            dil = o_ref.shape[0]
            for j in range(x_new.shape[1] // LANES):
                for r in range(dil):
                    o_ref[r, :, j * LANES:(j + 1) * LANES] = stage[j, pl.ds(r, tm // dil, stride=dil), :].astype(BF16)


def _post(kind, o_list, aux_list, x, xb, p, w_z, w_gl, w_out, ln_g, ln_b, w_gate, w_proj, batch, out_dils=()):
    t, d = x.shape
    tm = POST_TILE
    row = lambda w: pl.BlockSpec((tm, w), lambda i: (i, 0))
    full = lambda a: pl.BlockSpec(a.shape, lambda i: (0,) * a.ndim)

    branch_specs, stages = [], []
    for a in list(o_list) + list(aux_list):
        if a.ndim == 2:
            branch_specs.append(row(a.shape[1]))
        else:
            _, dil, sub_len, w = a.shape
            n_seq_tiles = sub_len * dil // tm
            branch_specs.append(pl.BlockSpec(
                (None, dil, tm // dil, w), lambda i, n=n_seq_tiles: (i // n, 0, i % n, 0)))
            stages.append(pltpu.VMEM((w // LANES, tm, LANES), F32))
    n_staged = len(stages)
    if out_dils:
        stages.append(pltpu.VMEM((d // LANES, tm, LANES), F32))
    seq_tiles = t // batch // tm
    weights = [w_z] + ([w_gl] if kind == "C" else []) + [w_out, ln_g, ln_b, w_gate, w_proj]
    args = list(o_list) + list(aux_list) + [x, xb, p[0]] + weights
    p_arr, p_layer = p
    p_spec = pl.BlockSpec((None, tm, p_arr.shape[2]), lambda i: (p_layer, i, 0))
    in_specs = branch_specs + [row(d), row(d), p_spec] + [full(w) for w in weights]
    return pl.pallas_call(
        functools.partial(_post_kernel, kind=kind, n_staged=n_staged, n_dil_outs=len(out_dils)),
        grid=(t // tm,),
        in_specs=in_specs,
        out_specs=[row(d), row(d)] + [
            pl.BlockSpec((None, dil, tm // dil, d), lambda i, n=seq_tiles: (i // n, 0, i % n, 0)) for dil in out_dils],
        out_shape=[jax.ShapeDtypeStruct((t, d), F32), jax.ShapeDtypeStruct((t, d), BF16)] + [
            jax.ShapeDtypeStruct((batch, dil, t // batch // dil, d), BF16) for dil in out_dils],
        scratch_shapes=[pltpu.VMEM((tm, d), BF16), pltpu.VMEM((tm, d), F32)] + stages,
        compiler_params=_params(1),
        name=f"post_{kind}",
    )(*args)


def _cast_kernel(x_ref, xb_ref, *rest):
    dil_out_refs, stage = rest[:-1], rest[-1]
    tm, d = x_ref.shape
    x = x_ref[...]
    xb_ref[...] = x.astype(BF16)
    if dil_out_refs:
        for j in range(d // LANES):
            stage[j] = x[:, j * LANES:(j + 1) * LANES]
        for o_ref in dil_out_refs:
            dil = o_ref.shape[0]
            for j in range(d // LANES):
                for r in range(dil):
                    o_ref[r, :, j * LANES:(j + 1) * LANES] = stage[j, pl.ds(r, tm // dil, stride=dil), :].astype(BF16)


def _cast_stream(x, batch, out_dils):
    t, d = x.shape
    tm = POST_TILE
    seq_tiles = t // batch // tm
    row = pl.BlockSpec((tm, d), lambda i: (i, 0))
    return pl.pallas_call(
        _cast_kernel,
        grid=(t // tm,),
        in_specs=[row],
        out_specs=[row] + [
            pl.BlockSpec((None, dil, tm // dil, d), lambda i, n=seq_tiles: (i // n, 0, i % n, 0)) for dil in out_dils],
        out_shape=[jax.ShapeDtypeStruct((t, d), BF16)] + [
            jax.ShapeDtypeStruct((batch, dil, t // batch // dil, d), BF16) for dil in out_dils],
        scratch_shapes=[pltpu.VMEM((d // LANES, tm, LANES), F32)],
        compiler_params=_params(1),
        name="cast_stream",
    )(x)


def _rope_tables(seq_len):
    inv = 1.0 / (ROPE_THETA ** (jnp.arange(0, HEAD_DIM, 2, dtype=F32) / HEAD_DIM))
    ang = jnp.arange(seq_len, dtype=F32)[:, None] * inv[None, :]
    cos, sin = jnp.cos(ang), jnp.sin(ang)
    zero = jnp.zeros_like(sin)
    cos_t = jnp.concatenate([cos] * 4, axis=1)
    sa_t = jnp.concatenate([-sin, zero, -sin, zero], axis=1)
    sb_t = jnp.concatenate([zero, sin, zero, sin], axis=1)
    return cos_t, sa_t, sb_t


def _residue_major(a, dil):
    *lead, s, w = a.shape
    return jnp.swapaxes(a.reshape(*lead, s // dil, dil, w), -3, -2)


def _head_cols_index(head_order):
    return np.concatenate([np.arange(h * HEAD_DIM, (h + 1) * HEAD_DIM) for h in head_order])


_B_HEAD_ORDER = [e * (N_HEADS // B_KV_HEADS) + i for i in range(N_PAIRS) for e in range(2)]
_C_HEAD_ORDER = [C_GROUP * (2 * m + e) + i for m in range(C_KV_HEADS // 2) for i in range(C_GROUP) for e in range(2)]


def _prep_a_weights(a_w_in):
    wd = N_HEADS * HEAD_DIM
    col = np.arange(a_w_in.shape[-1])
    is_q = (col < 3 * len(A_GROUPS) * wd) & (col // wd % 3 == 0)
    scale = np.where(is_q, HEAD_DIM ** -0.5 * LOG2E, 1.0).astype(np.float32)
    return (a_w_in * scale).astype(BF16)


def _mixer_a(xbs, b, s, a_w, j, tabs):
    wd = N_HEADS * HEAD_DIM
    outs, lses = [], []
    for gi, (window, dil) in enumerate(A_GROUPS):
        qkv = _proj(xbs[dil], a_w, BF16, [_residue_major(t, dil) for t in tabs],
                    n_rope_cols=2 * wd, w_cols=(j, 3 * gi * wd, 3 * wd))
        o, lse = _banded_attention(
            qkv, qkv, qkv, tq=min(512, s // dil), npv=1, max_dist=window // dil,
            q_off=0, k_off=1, v_off=2, kv_pairs=N_PAIRS, kv_pair_of=lambda pi: pi, want_lse=True)
        if dil == 1:
            o, lse = o.reshape(b * s, wd), lse.reshape(b * s, LANES)
        outs.append(o)
        lses.append(lse)
    return outs, lses, a_w[j, :, -wd:]


def _mixer_b(xbs, b, s, w_in, sinks, tabs):
    wd = N_HEADS * HEAD_DIM
    kvw = B_KV_HEADS * HEAD_DIM
    perm = _head_cols_index(_B_HEAD_ORDER)
    wq = w_in[:, :wd][:, perm] * (HEAD_DIM ** -0.5 * LOG2E)
    wk = w_in[:, wd:wd + kvw]
    wv = w_in[:, wd + kvw:wd + 2 * kvw]
    wz = w_in[:, wd + 2 * kvw:][:, perm]
    tabs1 = [t[None] for t in tabs]
    qk = _proj(xbs[1], jnp.concatenate([wq, wk], axis=1).astype(BF16), BF16, tabs1, n_rope_cols=wd + kvw)
    v = _proj(xbs[1], wv.astype(BF16), BF16)
    o = _banded_attention(
        qk, qk, v, tq=min(256, s), npv=1, max_dist=B_WINDOW - 1,
        q_off=0, k_off=wd // kvw, v_off=0, kv_pairs=1, kv_pair_of=lambda pi: 0,
        sinks=sinks[np.asarray(_B_HEAD_ORDER)].astype(F32) * LOG2E)
    return [o.reshape(b * s, wd)], [], wz, perm


def _mixer_c(xbs, b, s, w_in, w_ck, w_cv, pos, tabs):
    wd = N_HEADS * HEAD_DIM
    perm = _head_cols_index(_C_HEAD_ORDER)
    cols = np.cumsum([0, wd] + [C_KV] * 6 + [3 * N_HEADS, wd])
    part = lambda i: w_in[:, cols[i]:cols[i + 1]]
    wq = part(0)[:, perm] * (HEAD_DIM ** -0.5 * LOG2E)
    w_gl = part(7).reshape(-1, 3, N_HEADS)[:, :, np.asarray(_C_HEAD_ORDER)].reshape(-1, 3 * N_HEADS)
    w_gl = jnp.pad(w_gl, ((0, 0), (0, LANES - 3 * N_HEADS)))
    w_att = jnp.concatenate([wq, part(3), part(5), part(4), part(6)], axis=1)
    tabs1 = [t[None] for t in tabs]
    att = _proj(xbs[1], w_att.astype(BF16), BF16, tabs1, n_rope_cols=wd + 2 * C_KV)
    cmp_in = _proj(xbs[1], jnp.concatenate([part(1), part(2)], axis=1).astype(BF16), F32, tabs1,
                   n_rope_cols=C_KV, tn=C_KV).reshape(b, s, 2 * C_KV)

    nc = s // C_CMP_STRIDE
    cw = C_CMP_STRIDE * HEAD_DIM

    def chunks(t):
        return jnp.transpose(t.reshape(b, s, C_KV_HEADS, HEAD_DIM), (0, 2, 1, 3)).reshape(b, C_KV_HEADS, nc, cw)

    pos2 = pos.reshape(2, cw)

    def compressed(t, w):
        c = _compress(chunks(t), pos2, w.reshape(2, cw, HEAD_DIM).astype(BF16))
        return jnp.transpose(c, (0, 2, 1, 3)).reshape(b, nc, C_KV)

    kcmp, vcmp = compressed(cmp_in[:, :, :C_KV], w_ck), compressed(cmp_in[:, :, C_KV:], w_cv)
    att3 = att.reshape(b, s, -1)
    vs = att3[:, :, wd + 2 * C_KV:wd + 3 * C_KV]
    vst = jnp.transpose(vs.reshape(b, s // SEL_TILE, SEL_TILE, C_KV), (0, 1, 3, 2))
    in_half0 = (jnp.arange(C_KV) % PAIR < HEAD_DIM)[None, None, :, None]
    vst = jnp.stack([jnp.where(in_half0, vst, 1), jnp.where(in_half0, 1, vst)], axis=1)
    o_cmp, o_slc = _nsa_select(att3, kcmp, vcmp, att3, vst, ks_off=wd // C_KV)
    o_win = _banded_attention(
        att, att, att, tq=C_WINDOW, npv=C_WINDOW // QBLK, max_dist=C_WINDOW - 1,
        q_off=0, k_off=wd // C_KV + 1, v_off=wd // C_KV + 3, kv_pairs=C_KV_HEADS // 2,
        kv_pair_of=lambda pi: pi // C_GROUP)
    outs = [o.reshape(b * s, wd) for o in (o_cmp, o_slc, o_win)]
    return outs, [], (part(8)[:, perm], w_gl), perm


def _layer(i, xt, xbs, b, s, tabs, p, a_w_in, a_w_out, b_w_in, b_sinks, b_w_out, c_w_in, c_w_ck, c_w_cv, c_pos,
           c_w_out, ln_g, ln_b, ple_w_proj, ple_w_gate):
    d = xt.shape[1]
    j, kind = divmod(i, N_MIXERS)
    w_gl = None
    if kind == 0:
        outs, aux, w_z = _mixer_a(xbs, b, s, a_w_in, j, tabs)
        w_out, name = a_w_out[j], "A"
    elif kind == 1:
        outs, aux, w_z, perm = _mixer_b(xbs, b, s, b_w_in[j], b_sinks[j], tabs)
        w_out, name = b_w_out[j][perm, :], "B"
    else:
        outs, aux, (w_z, w_gl), perm = _mixer_c(xbs, b, s, c_w_in[j], c_w_ck[j], c_w_cv[j], c_pos[j], tabs)
        w_out, name = c_w_out[j][perm, :], "C"
        w_gl = w_gl.astype(BF16)
    next_dils = _layer_dils(i + 1)
    res = _post(name, outs, aux, xt, xbs[1].reshape(b * s, d), (p.reshape(p.shape[0], b * s, -1), i), w_z.astype(BF16), w_gl,
                w_out.astype(BF16), ln_g[i].reshape(1, d), ln_b[i].reshape(1, d),
                ple_w_gate[i].astype(BF16), ple_w_proj[i].astype(BF16), b, out_dils=next_dils)
    new_xbs = {1: res[1].reshape(b, 1, s, d)}
    new_xbs.update(zip(next_dils, res[2:]))
    return res[0], new_xbs


def _layer_dils(i):
    if i < DEPTH and i % N_MIXERS == 0:
        return tuple(dil for _, dil in A_GROUPS if dil > 1)
    return ()


def kernel(x, p, a_w_in, a_w_out, b_w_in, b_sinks, b_w_out, c_w_in, c_w_ck, c_w_cv, c_pos, c_w_out,
           ln_g, ln_b, ple_w_proj, ple_w_gate):
    b, s, d = x.shape
    assert d == N_HEADS * HEAD_DIM and s % (QBLK * A_GROUPS[-1][1]) == 0 and s % C_WINDOW == 0
    tabs = _rope_tables(s)
    xt = x.reshape(b * s, d)
    dils0 = _layer_dils(0)
    copies = _cast_stream(xt, b, dils0)
    xbs = {1: copies[0].reshape(b, 1, s, d)}
    xbs.update(zip(dils0, copies[1:]))
    a_w_in = _prep_a_weights(a_w_in)
    for i in range(DEPTH):
        xt, xbs = _layer(i, xt, xbs, b, s, tabs, p, a_w_in, a_w_out, b_w_in, b_sinks, b_w_out, c_w_in, c_w_ck,
                         c_w_cv, c_pos, c_w_out, ln_g, ln_b, ple_w_proj, ple_w_gate)
    return xt.reshape(b, s, d)
```

```python
import functools

import numpy as np
import jax
import jax.numpy as jnp
from jax import lax
from jax.experimental import pallas as pl
from jax.experimental.pallas import tpu as pltpu

F32 = jnp.float32
BF16 = jnp.bfloat16

LANES = 128
VMEM_LIMIT_BYTES = 56 * 1024 * 1024

HEAD_DIM = 64
HALF = HEAD_DIM // 2
PAIR = 2 * HEAD_DIM
assert PAIR == LANES
N_HEADS = 16
N_PAIRS = N_HEADS // 2
ROPE_THETA = 10000.0
QBLK = 128
BAND_GROUP = 4
PROJ_ROWS = 128
POST_TILE = 512
POST_ROWS = 256
LN_EPS = 1e-5
DEPTH = 4
N_MIXERS = 3
DEEPNORM_ALPHA = (2 * DEPTH) ** 0.25
A_GROUPS = ((128, 1), (512, 4), (2048, 16))
B_KV_HEADS = 2
B_WINDOW = 128
C_KV_HEADS = 4
C_GROUP = N_HEADS // C_KV_HEADS
C_KV = C_KV_HEADS * HEAD_DIM
C_CMP_STRIDE = 16
C_CMP_LEN = 32
C_SEL_LEN = 64
C_N_SEL = 16
C_WINDOW = 512
C_SEL_OVERLAP = (1.0, 2.0, 2.0, 2.0, 1.0)
SEL_PER_CMP = C_SEL_LEN // C_CMP_STRIDE
SEL_TILE = 512
CMP_COL_STEP = 128
LOG2E = 1.4426950408889634
NEG_INF = float("-inf")
MASKED = -1e30

_NT = (((1,), (1,)), ((), ()))


def _params(n_grid):
    return pltpu.CompilerParams(
        dimension_semantics=("arbitrary",) * n_grid, vmem_limit_bytes=VMEM_LIMIT_BYTES)


def _proj_kernel(*refs, n_rope_tiles, n_tiles):
    it = iter(refs)
    x_ref, w_ref = next(it), next(it)
    if n_rope_tiles:
        tab_refs = [next(it), next(it), next(it)]
    o_ref = next(it)
    dil, sub, _ = x_ref.shape
    tn = w_ref.shape[1]
    n_chunks = dil * sub // PROJ_ROWS

    def chunk(ref, m, cols=slice(None)):
        if sub >= PROJ_ROWS:
            r, l0 = divmod(m * PROJ_ROWS, sub)
            return ref.at[r, l0:l0 + PROJ_ROWS, cols]
        k = PROJ_ROWS // sub
        return ref.at[m * k:(m + 1) * k, :, cols]

    def emit(rope):
        def matmul(m):
            xm = chunk(x_ref, m)[...].reshape(PROJ_ROWS, x_ref.shape[2])
            return jnp.dot(xm, w_ref[...], preferred_element_type=F32)

        def finish(m, acc):
            if rope:
                c, sa, sb = (chunk(t, m)[...].reshape(PROJ_ROWS, LANES) for t in tab_refs)
            for j in range(tn // LANES):
                cl = slice(j * LANES, (j + 1) * LANES)
                t = acc[:, cl]
                if rope:
                    t = t * c + pltpu.roll(t, LANES - HALF, 1) * sa + pltpu.roll(t, HALF, 1) * sb
                dst = chunk(o_ref, m, cl)
                dst[...] = t.astype(o_ref.dtype).reshape(dst.shape)

        acc = matmul(0)
        for m in range(n_chunks):
            nxt = matmul(m + 1) if m + 1 < n_chunks else None
            finish(m, acc)
            acc = nxt

    if n_rope_tiles == 0 or n_rope_tiles == n_tiles:
        emit(n_rope_tiles > 0)
    else:
        pl.when(pl.program_id(1) < n_rope_tiles)(lambda: emit(True))
        pl.when(pl.program_id(1) >= n_rope_tiles)(lambda: emit(False))


def _pick_tile(n, candidates):
    for c in candidates:
        if n % c == 0:
            return c
    raise ValueError(f"no tile for {n}")


def _proj(x, w, out_dtype, rope_tabs=None, n_rope_cols=0, tn=None, w_cols=None):
    batch, dil, sub_len, k = x.shape
    layer, col0, n = w_cols if w_cols is not None else (None, 0, w.shape[-1])
    seq_len = dil * sub_len
    tm = _pick_tile(seq_len, (2048, 1024, 512, 256, 128))
    tn = tn or _pick_tile(np.gcd(n, n_rope_cols), (1024, 512, 384, 256, 128))
    sub = tm // dil
    assert n % tn == 0 and n_rope_cols % tn == 0 and tm % PROJ_ROWS == 0
    assert sub % 16 == 0 and (sub % PROJ_ROWS == 0 or PROJ_ROWS % sub == 0)
    n_seq_tiles = seq_len // tm
    assert col0 % tn == 0
    w_spec = (pl.BlockSpec((k, tn), lambda i, j: (0, j)) if layer is None else
              pl.BlockSpec((None, k, tn), lambda i, j: (layer, 0, j + col0 // tn)))
    in_specs = [pl.BlockSpec((None, dil, sub, k), lambda i, j: (i // n_seq_tiles, 0, i % n_seq_tiles, 0)), w_spec]
    args = [x, w]
    if n_rope_cols:
        tab_spec = pl.BlockSpec((dil, sub, LANES), lambda i, j: (0, i % n_seq_tiles, 0))
        in_specs += [tab_spec] * 3
        args += list(rope_tabs)
    return pl.pallas_call(
        functools.partial(_proj_kernel, n_rope_tiles=n_rope_cols // tn, n_tiles=n // tn),
        grid=(batch * n_seq_tiles, n // tn),
        in_specs=in_specs,
        out_specs=pl.BlockSpec((None, dil, sub, tn), lambda i, j: (i // n_seq_tiles, 0, i % n_seq_tiles, j)),
        out_shape=jax.ShapeDtypeStruct((batch, dil, sub_len, n), out_dtype),
        compiler_params=_params(2),
        name=f"proj_d{dil}_r{n_rope_cols}",
    )(*args)


def _band_kernel(*refs, tq, npv, max_dist, kv_pair_of, has_sinks, want_lse):
    it = iter(refs)
    q_ref, kc_ref, kp_ref, vc_ref, vp_ref = (next(it) for _ in range(5))
    sink_ref = next(it) if has_sinks else None
    o_ref = next(it)
    lse_ref = next(it) if want_lse else None
    qb = pl.program_id(2)
    w = (npv + 1) * QBLK
    pv_rows = npv * QBLK
    lane = lax.broadcasted_iota(jnp.int32, (QBLK, LANES), 1)
    first_half = lane < HEAD_DIM
    qi = lax.broadcasted_iota(jnp.int32, (QBLK, w), 0)
    kj = lax.broadcasted_iota(jnp.int32, (QBLK, w), 1)
    dist = qi + npv * QBLK - kj
    band = (dist >= 0) & (dist <= max_dist)
    for sub in range(tq // QBLK):
        r0 = sub * QBLK
        kstart = qb * tq + r0 - npv * QBLK
        mask = band & (kj + kstart >= 0)
        lse_tile = jnp.zeros((QBLK, LANES), F32)
        for g0 in range(0, N_PAIRS, BAND_GROUP):
            staged = []
            for pi in range(g0, g0 + BAND_GROUP):
                cl = slice(kv_pair_of(pi) * LANES, (kv_pair_of(pi) + 1) * LANES)
                k_parts, v_parts = [], []
                if r0 < pv_rows:
                    k_parts.append(kp_ref[r0:pv_rows, cl])
                    v_parts.append(vp_ref[r0:pv_rows, cl])
                cs = max(r0 - pv_rows, 0)
                k_parts.append(kc_ref[cs:r0 + QBLK, cl])
                v_parts.append(vc_ref[cs:r0 + QBLK, cl])
                kwin = k_parts[0] if len(k_parts) == 1 else jnp.concatenate(k_parts, axis=0)
                vwin = v_parts[0] if len(v_parts) == 1 else jnp.concatenate(v_parts, axis=0)
                qp = q_ref[r0:r0 + QBLK, pi * LANES:(pi + 1) * LANES]
                for e in range(2):
                    qe = jnp.where(first_half if e == 0 else jnp.logical_not(first_half), qp, 0)
                    s = lax.dot_general(qe, kwin, _NT, preferred_element_type=F32)
                    staged.append((pi, e, jnp.where(mask, s, NEG_INF), vwin))
            outs = {}
            for pi, e, s, vwin in staged:
                m = jnp.max(s, axis=-1, keepdims=True)
                if has_sinks:
                    sk = sink_ref[2 * pi + e]
                    m = jnp.maximum(m, sk)
                p = jnp.exp2(s - m)
                l = jnp.sum(p, axis=-1, keepdims=True)
                if has_sinks:
                    l = l + jnp.exp2(sk - m)
                pv = jnp.dot(p.astype(BF16), vwin, preferred_element_type=F32)
                outs[(pi, e)] = pv / l
                if want_lse:
                    lse_tile = jnp.where(lane == 2 * pi + e, m + jnp.log2(l), lse_tile)
            for pi in range(g0, g0 + BAND_GROUP):
                o_ref[r0:r0 + QBLK, pi * LANES:(pi + 1) * LANES] = jnp.where(
                    first_half, outs[(pi, 0)], outs[(pi, 1)])
        if want_lse:
            lse_ref[r0:r0 + QBLK, :] = lse_tile


def _banded_attention(q_arr, k_arr, v_arr, *, tq, npv, max_dist, q_off, k_off, v_off,
                      kv_pairs, kv_pair_of, sinks=None, want_lse=False):
    b, dil, l, _ = q_arr.shape
    qw, kw = N_PAIRS * LANES, kv_pairs * LANES
    pv = npv * QBLK
    assert tq % pv == 0
    in_specs = [
        pl.BlockSpec((None, None, tq, qw), lambda bi, r, i: (bi, r, i, q_off)),
        pl.BlockSpec((None, None, tq, kw), lambda bi, r, i: (bi, r, i, k_off)),
        pl.BlockSpec((None, None, pv, kw), lambda bi, r, i: (bi, r, jnp.maximum(i * (tq // pv) - 1, 0), k_off)),
        pl.BlockSpec((None, None, tq, kw), lambda bi, r, i: (bi, r, i, v_off)),
        pl.BlockSpec((None, None, pv, kw), lambda bi, r, i: (bi, r, jnp.maximum(i * (tq // pv) - 1, 0), v_off)),
    ]
    args = [q_arr, k_arr, k_arr, v_arr, v_arr]
    if sinks is not None:
        in_specs.append(pl.BlockSpec(memory_space=pltpu.SMEM))
        args.append(sinks)
    out_specs = [pl.BlockSpec((None, None, tq, qw), lambda bi, r, i: (bi, r, i, 0))]
    out_shape = [jax.ShapeDtypeStruct((b, dil, l, qw), F32)]
    if want_lse:
        out_specs.append(pl.BlockSpec((None, None, tq, LANES), lambda bi, r, i: (bi, r, i, 0)))
        out_shape.append(jax.ShapeDtypeStruct((b, dil, l, LANES), F32))
    res = pl.pallas_call(
        functools.partial(_band_kernel, tq=tq, npv=npv, max_dist=max_dist, kv_pair_of=kv_pair_of,
                          has_sinks=sinks is not None, want_lse=want_lse),
        grid=(b, dil, l // tq),
        in_specs=in_specs,
        out_specs=out_specs,
        out_shape=out_shape,
        compiler_params=_params(3),
        name=f"band_d{dil}_w{max_dist}",
    )(*args)
    return res if want_lse else res[0]


def _compress_kernel(c_ref, pos_ref, w_ref, o_ref):
    c = c_ref[...]
    top = jnp.dot((c + pos_ref[0:1, :]).astype(BF16), w_ref[0], preferred_element_type=F32)
    bot = jnp.dot((c + pos_ref[1:2, :]).astype(BF16), w_ref[1], preferred_element_type=F32)
    nc = c.shape[0]
    o_ref[...] = (top + pltpu.roll(bot, nc - 1, 0)).astype(o_ref.dtype)


def _compress(chunks, pos, w):
    b, hk, nc, cw = chunks.shape
    return pl.pallas_call(
        _compress_kernel,
        grid=(b, hk),
        in_specs=[pl.BlockSpec((None, None, nc, cw), lambda bi, h: (bi, h, 0, 0)),
                  pl.BlockSpec((2, cw), lambda bi, h: (0, 0)),
                  pl.BlockSpec((2, cw, HEAD_DIM), lambda bi, h: (0, 0, 0))],
        out_specs=pl.BlockSpec((None, None, nc, HEAD_DIM), lambda bi, h: (bi, h, 0, 0)),
        out_shape=jax.ShapeDtypeStruct((b, hk, nc, HEAD_DIM), BF16),
        compiler_params=_params(2),
        name="nsa_compress",
    )(chunks, pos, w)


def _stack_group_queries(q_ref, mp, half):
    pairs = [C_GROUP * mp + i for i in range(C_GROUP)]
    return jnp.concatenate(
        [jnp.where(half, q_ref[:, pr * LANES:(pr + 1) * LANES], 0) for pr in pairs], axis=0)


def _store_group_heads(o_ref, val, mp, e, first_half):
    for i in range(C_GROUP):
        ol = slice((C_GROUP * mp + i) * LANES, (C_GROUP * mp + i + 1) * LANES)
        rows = slice(i * QBLK, (i + 1) * QBLK)
        if e == 0:
            o_ref[:, ol] = val[rows]
        else:
            o_ref[:, ol] = jnp.where(first_half, o_ref[:, ol], val[rows])


def _nsa_cmp_kernel(q_ref, kc_ref, vc_ref, ocmp_ref, sel_ref, impt_ref, *, n_sel):
    nc = kc_ref.shape[0]
    ns = nc // SEL_PER_CMP
    t0 = pl.program_id(1) * QBLK
    lane = lax.broadcasted_iota(jnp.int32, (QBLK, LANES), 1)
    first_half = lane < HEAD_DIM

    n_free = max(n_sel - 3, 0)

    def body(ncols):
        nv = ncols // SEL_PER_CMP
        assert nv >= n_sel
        jj = lax.broadcasted_iota(jnp.int32, (nv, QBLK), 0)
        cur = (t0 + lax.broadcasted_iota(jnp.int32, (nv, QBLK), 1)) // C_SEL_LEN
        forced = (jj == 0) | (jj == cur) | (jj == cur - 1)
        bvalid = jj <= cur
        qi_c = lax.broadcasted_iota(jnp.int32, (QBLK, ncols), 0)
        nn_c = lax.broadcasted_iota(jnp.int32, (QBLK, ncols), 1)
        cvalid = nn_c * C_CMP_STRIDE + (C_CMP_LEN - 1) <= t0 + qi_c
        cvalid = jnp.concatenate([cvalid] * C_GROUP, axis=0)
        impt_ref[0:8, :] = jnp.zeros((8, QBLK), F32)
        if ncols < nc:
            impt_ref[8 + ncols:8 + nc, :] = jnp.zeros((nc - ncols, QBLK), F32)

        scores = []
        for kh in range(C_KV_HEADS):
            mp, e = divmod(kh, 2)
            cl = slice(mp * LANES, (mp + 1) * LANES)
            half = first_half if e == 0 else jnp.logical_not(first_half)
            qst = _stack_group_queries(q_ref, mp, half)
            sc = lax.dot_general(qst, kc_ref[0:ncols, cl], _NT, preferred_element_type=F32)
            scores.append(jnp.where(cvalid, sc, NEG_INF))

        imps = []
        for kh in range(C_KV_HEADS):
            mp, e = divmod(kh, 2)
            cl = slice(mp * LANES, (mp + 1) * LANES)
            sc = scores[kh]
            mx = jnp.max(sc, axis=-1, keepdims=True)
            mx = jnp.where(mx > NEG_INF, mx, 0.0)
            ee = jnp.exp2(sc - mx)
            pc = ee / jnp.maximum(jnp.sum(ee, axis=-1, keepdims=True), 1e-30)
            ocmp = jnp.dot(pc.astype(BF16), vc_ref[0:ncols, cl], preferred_element_type=F32)
            _store_group_heads(ocmp_ref, ocmp, mp, e, first_half)
            imp = pc[0:QBLK]
            for g in range(1, C_GROUP):
                imp = imp + pc[g * QBLK:(g + 1) * QBLK]
            imps.append(imp)

        for kh in range(C_KV_HEADS):
            for c in range(ncols // QBLK):
                impt_ref[8 + c * QBLK:8 + (c + 1) * QBLK, :] = imps[kh][:, c * QBLK:(c + 1) * QBLK].T
            imp_s = C_SEL_OVERLAP[0] * impt_ref[pl.ds(7, nv, stride=SEL_PER_CMP), :]
            for o_off in range(1, len(C_SEL_OVERLAP)):
                imp_s = imp_s + C_SEL_OVERLAP[o_off] * impt_ref[pl.ds(7 + o_off, nv, stride=SEL_PER_CMP), :]
            score = jnp.where(forced, NEG_INF, jnp.where(bvalid, imp_s, -1.0))
            selt = jnp.where(forced, 1.0, 0.0)
            for _ in range(n_free):
                best = jnp.max(score, axis=0, keepdims=True)
                first = jnp.min(jnp.where(score == best, jj, nv), axis=0, keepdims=True)
                hit = jj == first
                selt = jnp.where(hit, 1.0, selt)
                score = jnp.where(hit, NEG_INF, score)
            sel_ref[kh, 0:nv, :] = selt
            if nv < LANES:
                sel_ref[kh, nv:LANES, :] = jnp.zeros((LANES - nv, QBLK), F32)

    col_step = min(CMP_COL_STEP, nc)
    n_variants = nc // col_step
    tokens_per_step = col_step * C_CMP_STRIDE
    for k in range(n_variants):
        pl.when(t0 // tokens_per_step == k)(functools.partial(body, (k + 1) * col_step))


def _nsa_slc_kernel(q_ref, sel_ref, ks_ref, vst_ref, oslc_ref, acc_ref):
    gq = C_GROUP * QBLK
    blocks_per_tile = SEL_TILE // C_SEL_LEN
    bias_rows = 16
    t0 = pl.program_id(1) * QBLK
    lane = lax.broadcasted_iota(jnp.int32, (QBLK, LANES), 1)
    first_half = lane < HEAD_DIM
    row = lax.broadcasted_iota(jnp.int32, (LANES, QBLK), 0)
    top_rows = row < HEAD_DIM
    n_full = t0 // SEL_TILE
    key_in_tile = lax.broadcasted_iota(jnp.int32, (SEL_TILE, gq), 0)
    query_pos = t0 + lax.broadcasted_iota(jnp.int32, (SEL_TILE, gq), 1) % QBLK
    block_of_key = lax.broadcasted_iota(jnp.int32, (SEL_TILE, LANES), 0) // C_SEL_LEN
    block_onehot = jnp.where(
        block_of_key == lax.broadcasted_iota(jnp.int32, (SEL_TILE, LANES), 1), 1.0, 0.0).astype(BF16)
    bias_pad = jnp.zeros((LANES - bias_rows, gq), BF16)

    q_t = [q_ref[:, pr * LANES:(pr + 1) * LANES].astype(F32).T for pr in range(N_PAIRS)]
    qts = []
    for kh in range(C_KV_HEADS):
        mp, e = divmod(kh, 2)
        keep_rows = top_rows if e == 0 else jnp.logical_not(top_rows)
        qts.append(jnp.concatenate(
            [jnp.where(keep_rows, q_t[C_GROUP * mp + i], 0).astype(BF16) for i in range(C_GROUP)],
            axis=1))
    acc_ref[...] = jnp.zeros((C_KV_HEADS, LANES, gq), F32)

    def tile_step(kt, carry, diagonal):
        k0 = pl.multiple_of(kt * SEL_TILE, SEL_TILE)
        b0 = pl.multiple_of(kt * blocks_per_tile, blocks_per_tile)
        sts = []
        for kh in range(C_KV_HEADS):
            mp = kh // 2
            cl = slice(mp * LANES, (mp + 1) * LANES)
            keys = jnp.concatenate([ks_ref[pl.ds(k0, SEL_TILE), cl], block_onehot], axis=1)
            picked = sel_ref[kh, pl.ds(b0, blocks_per_tile), :]
            bias = jnp.concatenate([(picked - 1.0) * -MASKED] * C_GROUP, axis=1)
            bias = jnp.concatenate([bias, jnp.zeros((bias_rows - blocks_per_tile, gq), F32)], axis=0)
            queries = jnp.concatenate([qts[kh], bias.astype(BF16), bias_pad], axis=0)
            st = jnp.dot(keys, queries, preferred_element_type=F32)
            if diagonal:
                st = jnp.where(k0 + key_in_tile <= query_pos, st, MASKED)
            sts.append(st)
        new_carry = []
        for kh in range(C_KV_HEADS):
            mp = kh // 2
            cl = slice(mp * LANES, (mp + 1) * LANES)
            st = sts[kh]
            m_old = carry[kh]
            m_new = jnp.maximum(m_old, jnp.max(st, axis=0, keepdims=True))
            alpha = jnp.exp2(m_old - m_new)
            p = jnp.exp2(st - m_new)
            acc_ref[kh] = alpha * acc_ref[kh] + jnp.dot(
                vst_ref[kh % 2, kt, cl, :], p.astype(BF16), preferred_element_type=F32)
            new_carry += [m_new]
        return tuple(new_carry)

    init = (jnp.full((1, gq), NEG_INF, F32),) * C_KV_HEADS
    carry = lax.fori_loop(0, n_full, functools.partial(tile_step, diagonal=False), init)
    carry = lax.fori_loop(n_full, n_full + 1, functools.partial(tile_step, diagonal=True), carry)
    for mp in range(C_KV_HEADS // 2):
        ots = [acc_ref[2 * mp + e] / acc_ref[2 * mp + e, HEAD_DIM * (1 - e):HEAD_DIM * (1 - e) + 1, :]
               for e in range(2)]
        for i in range(C_GROUP):
            ol = slice((C_GROUP * mp + i) * LANES, (C_GROUP * mp + i + 1) * LANES)
            qs = slice(i * QBLK, (i + 1) * QBLK)
            oslc_ref[:, ol] = jnp.where(first_half, ots[0][:, qs].T, ots[1][:, qs].T)


def _nsa_select(q_arr, kcmp, vcmp, ks_arr, vst_arr, *, ks_off):
    b, s, _ = q_arr.shape
    nc = kcmp.shape[1]
    ns = nc // SEL_PER_CMP
    assert ns <= LANES and s % SEL_TILE == 0
    qw = N_PAIRS * LANES
    gq = C_GROUP * QBLK
    q_spec = pl.BlockSpec((None, QBLK, qw), lambda bi, i: (bi, i, 0))
    o_spec = pl.BlockSpec((None, QBLK, qw), lambda bi, i: (bi, i, 0))
    sel_spec = pl.BlockSpec((None, C_KV_HEADS, LANES, QBLK), lambda bi, i: (bi, 0, 0, i))
    o_shape = jax.ShapeDtypeStruct((b, s, qw), F32)
    o_cmp, sel = pl.pallas_call(
        functools.partial(_nsa_cmp_kernel, n_sel=min(C_N_SEL, ns)),
        grid=(b, s // QBLK),
        in_specs=[q_spec,
                  pl.BlockSpec((None, nc, C_KV), lambda bi, i: (bi, 0, 0)),
                  pl.BlockSpec((None, nc, C_KV), lambda bi, i: (bi, 0, 0))],
        out_specs=[o_spec, sel_spec],
        out_shape=[o_shape, jax.ShapeDtypeStruct((b, C_KV_HEADS, LANES, s), F32)],
        scratch_shapes=[pltpu.VMEM((8 + nc, QBLK), F32)],
        compiler_params=_params(2),
        name="nsa_compressed",
    )(q_arr, kcmp, vcmp)
    o_slc = pl.pallas_call(
        _nsa_slc_kernel,
        grid=(b, s // QBLK),
        in_specs=[q_spec,
                  sel_spec,
                  pl.BlockSpec((None, s, C_KV), lambda bi, i: (bi, 0, ks_off)),
                  pl.BlockSpec((None, 2, s // SEL_TILE, C_KV, SEL_TILE), lambda bi, i: (bi, 0, 0, 0, 0))],
        out_specs=o_spec,
        out_shape=o_shape,
        scratch_shapes=[pltpu.VMEM((C_KV_HEADS, LANES, gq), F32)],
        compiler_params=_params(2),
        name="nsa_selected",
    )(q_arr, sel, ks_arr, vst_arr)
    return o_cmp, o_slc


def _head_cols(tile, col, first_half):
    tm = tile.shape[0]
    a = jnp.broadcast_to(tile[:, col:col + 1], (tm, LANES))
    b = jnp.broadcast_to(tile[:, col + 1:col + 2], (tm, LANES))
    return jnp.where(first_half, a, b)


def _post_kernel(*refs, kind, n_staged, n_dil_outs):
    it = iter(refs)
    n_branch = {"A": 3, "B": 1, "C": 3}[kind]
    o_refs = [next(it) for _ in range(n_branch)]
    aux_refs = [next(it) for _ in range(3)] if kind == "A" else []
    x_ref, xb_ref, p_ref, wz_ref = (next(it) for _ in range(4))
    wgl_ref = next(it) if kind == "C" else None
    wo_ref, g_ref, b_ref, wg_ref, wp_ref, out_ref, outb_ref = (next(it) for _ in range(7))
    dil_out_refs = [next(it) for _ in range(n_dil_outs)]
    u_ref, z_ref = next(it), next(it)
    stage_refs = [next(it) for _ in range(n_staged + (1 if n_dil_outs else 0))]
    tm = x_ref.shape[0]

    def token_order(ref):
        if len(ref.shape) == 2:
            return lambda j: ref[:, j * LANES:(j + 1) * LANES]
        dil = ref.shape[0]
        stage = stage_refs.pop(0)
        for j in range(ref.shape[2] // LANES):
            for r in range(dil):
                stage[j, pl.ds(r, tm // dil, stride=dil), :] = ref[r, :, j * LANES:(j + 1) * LANES]
        return lambda j: stage[j]

    o_cols = [token_order(r) for r in o_refs]
    aux_cols = [token_order(r) for r in aux_refs]

    n_chunks = tm // POST_ROWS
    rows = [slice(c * POST_ROWS, (c + 1) * POST_ROWS) for c in range(n_chunks)]
    first_half = lax.broadcasted_iota(jnp.int32, (POST_ROWS, LANES), 1) < HEAD_DIM
    gates = [None] * n_chunks
    for c, rs in enumerate(rows):
        xb = xb_ref[rs, :]
        z_ref[rs, :] = jnp.dot(xb, wz_ref[...], preferred_element_type=F32)
        if kind == "C":
            gates[c] = jax.nn.sigmoid(jnp.dot(xb, wgl_ref[...], preferred_element_type=F32))

    hs = []
    for c, rs in enumerate(rows):
        if kind == "A":
            lses = [col(0)[rs] for col in aux_cols]
            mx = jnp.maximum(jnp.maximum(lses[0], lses[1]), lses[2])
            ws = [jnp.exp2(v - mx) for v in lses]
            den = ws[0] + ws[1] + ws[2]
            ws = [v / den for v in ws]
        for pi in range(N_PAIRS):
            cl = slice(pi * LANES, (pi + 1) * LANES)
            if kind == "A":
                o = _head_cols(ws[0], 2 * pi, first_half) * o_cols[0](pi)[rs]
                for g in range(1, 3):
                    o = o + _head_cols(ws[g], 2 * pi, first_half) * o_cols[g](pi)[rs]
            elif kind == "B":
                o = o_cols[0](pi)[rs]
            else:
                o = _head_cols(gates[c], 2 * pi, first_half) * o_cols[0](pi)[rs]
                for br in range(1, 3):
                    o = o + _head_cols(gates[c], br * N_HEADS + 2 * pi, first_half) * o_cols[br](pi)[rs]
            z = z_ref[rs, cl]
            u_ref[rs, cl] = (o * (z * jax.nn.sigmoid(z))).astype(BF16)
        hs.append(jnp.dot(u_ref[rs, :], wo_ref[...], preferred_element_type=F32))

    pre = []
    for c, rs in enumerate(rows):
        y = DEEPNORM_ALPHA * x_ref[rs, :] + hs[c]
        mu = jnp.mean(y, axis=-1, keepdims=True)
        yc = y - mu
        var = jnp.mean(yc * yc, axis=-1, keepdims=True)
        yn = yc * lax.rsqrt(var + LN_EPS) * g_ref[...] + b_ref[...]
        gate_logits = jnp.dot(yn.astype(BF16), wg_ref[...], preferred_element_type=F32)
        pp = jnp.dot(p_ref[rs, :].astype(BF16), wp_ref[...], preferred_element_type=F32)
        pre.append((yn, gate_logits, pp))

    for c, rs in enumerate(rows):
        yn, gate_logits, pp = pre[c]
        x_new = yn + jax.nn.sigmoid(gate_logits) * pp
        out_ref[rs, :] = x_new
        outb_ref[rs, :] = x_new.astype(BF16)
        if dil_out_refs:
            for j in range(x_new.shape[1] // LANES):
                stage_refs[-1][j, rs, :] = x_new[:, j * LANES:(j + 1) * LANES]
    if dil_out_refs:
        stage = stage_refs[-1]
        for o_ref in dil_out_refs:
            dil = o_ref.shape[0]
            for j in range(x_new.shape[1] // LANES):
                for r in range(dil):
                    o_ref[r, :, j * LANES:(j + 1) * LANES] = stage[j, pl.ds(r, tm // dil, stride=dil), :].astype(BF16)


def _post(kind, o_list, aux_list, x, xb, p, w_z, w_gl, w_out, ln_g, ln_b, w_gate, w_proj, batch, out_dils=()):
    t, d = x.shape
    tm = POST_TILE
    row = lambda w: pl.BlockSpec((tm, w), lambda i: (i, 0))
    full = lambda a: pl.BlockSpec(a.shape, lambda i: (0,) * a.ndim)

    branch_specs, stages = [], []
    for a in list(o_list) + list(aux_list):
        if a.ndim == 2:
            branch_specs.append(row(a.shape[1]))
        else:
            _, dil, sub_len, w = a.shape
            n_seq_tiles = sub_len * dil // tm
            branch_specs.append(pl.BlockSpec(
                (None, dil, tm // dil, w), lambda i, n=n_seq_tiles: (i // n, 0, i % n, 0)))
            stages.append(pltpu.VMEM((w // LANES, tm, LANES), F32))
    n_staged = len(stages)
    if out_dils:
        stages.append(pltpu.VMEM((d // LANES, tm, LANES), F32))
    seq_tiles = t // batch // tm
    weights = [w_z] + ([w_gl] if kind == "C" else []) + [w_out, ln_g, ln_b, w_gate, w_proj]
    args = list(o_list) + list(aux_list) + [x, xb, p[0]] + weights
    p_arr, p_layer = p
    p_spec = pl.BlockSpec((None, tm, p_arr.shape[2]), lambda i: (p_layer, i, 0))
    in_specs = branch_specs + [row(d), row(d), p_spec] + [full(w) for w in weights]
    return pl.pallas_call(
        functools.partial(_post_kernel, kind=kind, n_staged=n_staged, n_dil_outs=len(out_dils)),
        grid=(t // tm,),
        in_specs=in_specs,
        out_specs=[row(d), row(d)] + [
            pl.BlockSpec((None, dil, tm // dil, d), lambda i, n=seq_tiles: (i // n, 0, i % n, 0)) for dil in out_dils],
        out_shape=[jax.ShapeDtypeStruct((t, d), F32), jax.ShapeDtypeStruct((t, d), BF16)] + [
            jax.ShapeDtypeStruct((batch, dil, t // batch // dil, d), BF16) for dil in out_dils],
        scratch_shapes=[pltpu.VMEM((tm, d), BF16), pltpu.VMEM((tm, d), F32)] + stages,
        compiler_params=_params(1),
        name=f"post_{kind}",
    )(*args)


def _cast_kernel(x_ref, xb_ref, *rest):
    dil_out_refs, stage = rest[:-1], rest[-1]
    tm, d = x_ref.shape
    x = x_ref[...]
    xb_ref[...] = x.astype(BF16)
    if dil_out_refs:
        for j in range(d // LANES):
            stage[j] = x[:, j * LANES:(j + 1) * LANES]
        for o_ref in dil_out_refs:
            dil = o_ref.shape[0]
            for j in range(d // LANES):
                for r in range(dil):
                    o_ref[r, :, j * LANES:(j + 1) * LANES] = stage[j, pl.ds(r, tm // dil, stride=dil), :].astype(BF16)


def _cast_stream(x, batch, out_dils):
    t, d = x.shape
    tm = POST_TILE
    seq_tiles = t // batch // tm
    row = pl.BlockSpec((tm, d), lambda i: (i, 0))
    return pl.pallas_call(
        _cast_kernel,
        grid=(t // tm,),
        in_specs=[row],
        out_specs=[row] + [
            pl.BlockSpec((None, dil, tm // dil, d), lambda i, n=seq_tiles: (i // n, 0, i % n, 0)) for dil in out_dils],
        out_shape=[jax.ShapeDtypeStruct((t, d), BF16)] + [
            jax.ShapeDtypeStruct((batch, dil, t // batch // dil, d), BF16) for dil in out_dils],
        scratch_shapes=[pltpu.VMEM((d // LANES, tm, LANES), F32)],
        compiler_params=_params(1),
        name="cast_stream",
    )(x)


def _rope_tables(seq_len):
    inv = 1.0 / (ROPE_THETA ** (jnp.arange(0, HEAD_DIM, 2, dtype=F32) / HEAD_DIM))
    ang = jnp.arange(seq_len, dtype=F32)[:, None] * inv[None, :]
    cos, sin = lax.optimization_barrier((jnp.cos(ang), jnp.sin(ang)))
    zero = jnp.zeros_like(sin)
    cos_t = jnp.concatenate([cos] * 4, axis=1)
    sa_t = jnp.concatenate([-sin, zero, -sin, zero], axis=1)
    sb_t = jnp.concatenate([zero, sin, zero, sin], axis=1)
    return cos_t, sa_t, sb_t


def _residue_major(a, dil):
    *lead, s, w = a.shape
    return jnp.swapaxes(a.reshape(*lead, s // dil, dil, w), -3, -2)


def _head_cols_index(head_order):
    return np.concatenate([np.arange(h * HEAD_DIM, (h + 1) * HEAD_DIM) for h in head_order])


_B_HEAD_ORDER = [e * (N_HEADS // B_KV_HEADS) + i for i in range(N_PAIRS) for e in range(2)]
_C_HEAD_ORDER = [C_GROUP * (2 * m + e) + i for m in range(C_KV_HEADS // 2) for i in range(C_GROUP) for e in range(2)]


def _prep_a_weights(a_w_in):
    wd = N_HEADS * HEAD_DIM
    col = np.arange(a_w_in.shape[-1])
    is_q = (col < 3 * len(A_GROUPS) * wd) & (col // wd % 3 == 0)
    scale = np.where(is_q, HEAD_DIM ** -0.5 * LOG2E, 1.0).astype(np.float32)
    return (a_w_in * scale).astype(BF16)


def _mixer_a(xbs, b, s, a_w, j, tabs):
    wd = N_HEADS * HEAD_DIM
    outs, lses = [], []
    for gi, (window, dil) in enumerate(A_GROUPS):
        qkv = _proj(xbs[dil], a_w, BF16, [_residue_major(t, dil) for t in tabs],
                    n_rope_cols=2 * wd, w_cols=(j, 3 * gi * wd, 3 * wd))
        o, lse = _banded_attention(
            qkv, qkv, qkv, tq=min(512, s // dil), npv=1, max_dist=window // dil,
            q_off=0, k_off=1, v_off=2, kv_pairs=N_PAIRS, kv_pair_of=lambda pi: pi, want_lse=True)
        if dil == 1:
            o, lse = o.reshape(b * s, wd), lse.reshape(b * s, LANES)
        outs.append(o)
        lses.append(lse)
    return outs, lses, a_w[j, :, -wd:]


def _mixer_b(xbs, b, s, w_in, sinks, tabs):
    wd = N_HEADS * HEAD_DIM
    kvw = B_KV_HEADS * HEAD_DIM
    perm = _head_cols_index(_B_HEAD_ORDER)
    wq = w_in[:, :wd][:, perm] * (HEAD_DIM ** -0.5 * LOG2E)
    wk = w_in[:, wd:wd + kvw]
    wv = w_in[:, wd + kvw:wd + 2 * kvw]
    wz = w_in[:, wd + 2 * kvw:][:, perm]
    tabs1 = [t[None] for t in tabs]
    qk = _proj(xbs[1], jnp.concatenate([wq, wk], axis=1).astype(BF16), BF16, tabs1, n_rope_cols=wd + kvw)
    v = _proj(xbs[1], wv.astype(BF16), BF16)
    o = _banded_attention(
        qk, qk, v, tq=min(256, s), npv=1, max_dist=B_WINDOW - 1,
        q_off=0, k_off=wd // kvw, v_off=0, kv_pairs=1, kv_pair_of=lambda pi: 0,
        sinks=sinks[np.asarray(_B_HEAD_ORDER)].astype(F32) * LOG2E)
    return [o.reshape(b * s, wd)], [], wz, perm


def _mixer_c(xbs, b, s, w_in, w_ck, w_cv, pos, tabs):
    wd = N_HEADS * HEAD_DIM
    perm = _head_cols_index(_C_HEAD_ORDER)
    cols = np.cumsum([0, wd] + [C_KV] * 6 + [3 * N_HEADS, wd])
    part = lambda i: w_in[:, cols[i]:cols[i + 1]]
    wq = part(0)[:, perm] * (HEAD_DIM ** -0.5 * LOG2E)
    w_gl = part(7).reshape(-1, 3, N_HEADS)[:, :, np.asarray(_C_HEAD_ORDER)].reshape(-1, 3 * N_HEADS)
    w_gl = jnp.pad(w_gl, ((0, 0), (0, LANES - 3 * N_HEADS)))
    w_att = jnp.concatenate([wq, part(3), part(5), part(4), part(6)], axis=1)
    tabs1 = [t[None] for t in tabs]
    att = _proj(xbs[1], w_att.astype(BF16), BF16, tabs1, n_rope_cols=wd + 2 * C_KV)
    cmp_in = _proj(xbs[1], jnp.concatenate([part(1), part(2)], axis=1).astype(BF16), F32, tabs1,
                   n_rope_cols=C_KV, tn=C_KV).reshape(b, s, 2 * C_KV)

    nc = s // C_CMP_STRIDE
    cw = C_CMP_STRIDE * HEAD_DIM

    def chunks(t):
        return jnp.transpose(t.reshape(b, s, C_KV_HEADS, HEAD_DIM), (0, 2, 1, 3)).reshape(b, C_KV_HEADS, nc, cw)

    pos2 = pos.reshape(2, cw)

    def compressed(t, w):
        c = _compress(chunks(t), pos2, w.reshape(2, cw, HEAD_DIM).astype(BF16))
        return jnp.transpose(c, (0, 2, 1, 3)).reshape(b, nc, C_KV)

    kcmp, vcmp = compressed(cmp_in[:, :, :C_KV], w_ck), compressed(cmp_in[:, :, C_KV:], w_cv)
    att3 = att.reshape(b, s, -1)
    vs = att3[:, :, wd + 2 * C_KV:wd + 3 * C_KV]
    vst = jnp.transpose(vs.reshape(b, s // SEL_TILE, SEL_TILE, C_KV), (0, 1, 3, 2))
    in_half0 = (jnp.arange(C_KV) % PAIR < HEAD_DIM)[None, None, :, None]
    vst = jnp.stack([jnp.where(in_half0, vst, 1), jnp.where(in_half0, 1, vst)], axis=1)
    o_cmp, o_slc = _nsa_select(att3, kcmp, vcmp, att3, vst, ks_off=wd // C_KV)
    o_win = _banded_attention(
        att, att, att, tq=C_WINDOW, npv=C_WINDOW // QBLK, max_dist=C_WINDOW - 1,
        q_off=0, k_off=wd // C_KV + 1, v_off=wd // C_KV + 3, kv_pairs=C_KV_HEADS // 2,
        kv_pair_of=lambda pi: pi // C_GROUP)
    outs = [o.reshape(b * s, wd) for o in (o_cmp, o_slc, o_win)]
    return outs, [], (part(8)[:, perm], w_gl), perm


def _layer(i, xt, xbs, b, s, tabs, p, a_w_in, a_w_out, b_w_in, b_sinks, b_w_out, c_w_in, c_w_ck, c_w_cv, c_pos,
           c_w_out, ln_g, ln_b, ple_w_proj, ple_w_gate):
    d = xt.shape[1]
    j, kind = divmod(i, N_MIXERS)
    w_gl = None
    if kind == 0:
        outs, aux, w_z = _mixer_a(xbs, b, s, a_w_in, j, tabs)
        w_out, name = a_w_out[j], "A"
    elif kind == 1:
        outs, aux, w_z, perm = _mixer_b(xbs, b, s, b_w_in[j], b_sinks[j], tabs)
        w_out, name = b_w_out[j][perm, :], "B"
    else:
        outs, aux, (w_z, w_gl), perm = _mixer_c(xbs, b, s, c_w_in[j], c_w_ck[j], c_w_cv[j], c_pos[j], tabs)
        w_out, name = c_w_out[j][perm, :], "C"
        w_gl = w_gl.astype(BF16)
    next_dils = _layer_dils(i + 1)
    res = _post(name, outs, aux, xt, xbs[1].reshape(b * s, d), (p.reshape(p.shape[0], b * s, -1), i), w_z.astype(BF16), w_gl,
                w_out.astype(BF16), ln_g[i].reshape(1, d), ln_b[i].reshape(1, d),
                ple_w_gate[i].astype(BF16), ple_w_proj[i].astype(BF16), b, out_dils=next_dils)
    new_xbs = {1: res[1].reshape(b, 1, s, d)}
    new_xbs.update(zip(next_dils, res[2:]))
    return res[0], new_xbs


def _layer_dils(i):
    if i < DEPTH and i % N_MIXERS == 0:
        return tuple(dil for _, dil in A_GROUPS if dil > 1)
    return ()


def kernel(x, p, a_w_in, a_w_out, b_w_in, b_sinks, b_w_out, c_w_in, c_w_ck, c_w_cv, c_pos, c_w_out,
           ln_g, ln_b, ple_w_proj, ple_w_gate):
    b, s, d = x.shape
    assert d == N_HEADS * HEAD_DIM and s % (QBLK * A_GROUPS[-1][1]) == 0 and s % C_WINDOW == 0
    tabs = _rope_tables(s)
    xt = x.reshape(b * s, d)
    dils0 = _layer_dils(0)
    copies = _cast_stream(xt, b, dils0)
    xbs = {1: copies[0].reshape(b, 1, s, d)}
    xbs.update(zip(dils0, copies[1:]))
    a_w_in = _prep_a_weights(a_w_in)
    for i in range(DEPTH):
        xt, xbs = _layer(i, xt, xbs, b, s, tabs, p, a_w_in, a_w_out, b_w_in, b_sinks, b_w_out, c_w_in, c_w_ck,
                         c_w_cv, c_pos, c_w_out, ln_g, ln_b, ple_w_proj, ple_w_gate)
    return xt.reshape(b, s, d)
```

```python
import functools

import numpy as np
import jax
import jax.numpy as jnp
from jax import lax
from jax.experimental import pallas as pl
from jax.experimental.pallas import tpu as pltpu

F32 = jnp.float32
BF16 = jnp.bfloat16

LANES = 128
SUBLANES = 8
BF16_SUBLANES = 16
VMEM_LIMIT_BYTES = 56 * 1024 * 1024

HEAD_DIM = 64
HALF = HEAD_DIM // 2
PAIR = 2 * HEAD_DIM
assert PAIR == LANES
N_HEADS = 16
N_PAIRS = N_HEADS // 2
ROPE_THETA = 10000.0
QBLK = 128
BAND_GROUP = 4
PROJ_ROWS = 128
POST_TILE = 512
POST_ROWS = 256
LN_EPS = 1e-5
DEPTH = 4
N_MIXERS = 3
DEEPNORM_ALPHA = (2 * DEPTH) ** 0.25
A_GROUPS = ((128, 1), (512, 4), (2048, 16))
B_KV_HEADS = 2
B_WINDOW = 128
C_KV_HEADS = 4
C_GROUP = N_HEADS // C_KV_HEADS
C_KV = C_KV_HEADS * HEAD_DIM
C_CMP_STRIDE = 16
C_CMP_LEN = 32
C_SEL_LEN = 64
C_N_SEL = 16
C_WINDOW = 512
C_SEL_OVERLAP = (1.0, 2.0, 2.0, 2.0, 1.0)
SEL_PER_CMP = C_SEL_LEN // C_CMP_STRIDE
SEL_TILE = 512
CMP_COL_STEP = 128
LOG2E = 1.4426950408889634
NEG_INF = float("-inf")
MASKED = -1e30

_NT = (((1,), (1,)), ((), ()))


def _params(n_grid):
    return pltpu.CompilerParams(
        dimension_semantics=("arbitrary",) * n_grid, vmem_limit_bytes=VMEM_LIMIT_BYTES)


def _proj_kernel(*refs, n_rope_tiles, n_tiles):
    it = iter(refs)
    x_ref, w_ref = next(it), next(it)
    if n_rope_tiles:
        tab_refs = [next(it), next(it), next(it)]
    o_ref = next(it)
    dil, sub, _ = x_ref.shape
    tn = w_ref.shape[1]
    n_chunks = dil * sub // PROJ_ROWS

    def chunk(ref, m, cols=slice(None)):
        if sub >= PROJ_ROWS:
            r, l0 = divmod(m * PROJ_ROWS, sub)
            return ref.at[r, l0:l0 + PROJ_ROWS, cols]
        k = PROJ_ROWS // sub
        return ref.at[m * k:(m + 1) * k, :, cols]

    def emit(rope):
        def matmul(m):
            xm = chunk(x_ref, m)[...].reshape(PROJ_ROWS, x_ref.shape[2])
            return jnp.dot(xm, w_ref[...], preferred_element_type=F32)

        def finish(m, acc):
            if rope:
                c, sa, sb = (chunk(t, m)[...].reshape(PROJ_ROWS, LANES) for t in tab_refs)
            for j in range(tn // LANES):
                cl = slice(j * LANES, (j + 1) * LANES)
                t = acc[:, cl]
                if rope:
                    t = t * c + pltpu.roll(t, LANES - HALF, 1) * sa + pltpu.roll(t, HALF, 1) * sb
                dst = chunk(o_ref, m, cl)
                dst[...] = t.astype(o_ref.dtype).reshape(dst.shape)

        acc = matmul(0)
        for m in range(n_chunks):
            nxt = matmul(m + 1) if m + 1 < n_chunks else None
            finish(m, acc)
            acc = nxt

    if n_rope_tiles == 0 or n_rope_tiles == n_tiles:
        emit(n_rope_tiles > 0)
    else:
        pl.when(pl.program_id(1) < n_rope_tiles)(lambda: emit(True))
        pl.when(pl.program_id(1) >= n_rope_tiles)(lambda: emit(False))


def _pick_tile(n, candidates):
    for c in candidates:
        if n % c == 0:
            return c
    raise ValueError(f"no tile for {n}")


def _proj(x, w, out_dtype, rope_tabs=None, n_rope_cols=0, tn=None, w_cols=None):
    batch, dil, sub_len, k = x.shape
    layer, col0, n = w_cols if w_cols is not None else (None, 0, w.shape[-1])
    seq_len = dil * sub_len
    tm = _pick_tile(seq_len, (2048, 1024, 512, 256, 128))
    tn = tn or _pick_tile(np.gcd(n, n_rope_cols), (1024, 512, 384, 256, 128))
    sub = tm // dil
    assert n % tn == 0 and n_rope_cols % tn == 0 and tm % PROJ_ROWS == 0
    assert sub % 16 == 0 and (sub % PROJ_ROWS == 0 or PROJ_ROWS % sub == 0)
    n_seq_tiles = seq_len // tm
    assert col0 % tn == 0
    w_spec = (pl.BlockSpec((k, tn), lambda i, j: (0, j)) if layer is None else
              pl.BlockSpec((None, k, tn), lambda i, j: (layer, 0, j + col0 // tn)))
    in_specs = [pl.BlockSpec((None, dil, sub, k), lambda i, j: (i // n_seq_tiles, 0, i % n_seq_tiles, 0)), w_spec]
    args = [x, w]
    if n_rope_cols:
        tab_spec = pl.BlockSpec((dil, sub, LANES), lambda i, j: (0, i % n_seq_tiles, 0))
        in_specs += [tab_spec] * 3
        args += list(rope_tabs)
    return pl.pallas_call(
        functools.partial(_proj_kernel, n_rope_tiles=n_rope_cols // tn, n_tiles=n // tn),
        grid=(batch * n_seq_tiles, n // tn),
        in_specs=in_specs,
        out_specs=pl.BlockSpec((None, dil, sub, tn), lambda i, j: (i // n_seq_tiles, 0, i % n_seq_tiles, j)),
        out_shape=jax.ShapeDtypeStruct((batch, dil, sub_len, n), out_dtype),
        compiler_params=_params(2),
        name=f"proj_d{dil}_r{n_rope_cols}",
    )(*args)


def _band_kernel(*refs, tq, npv, max_dist, kv_pair_of, has_sinks, want_lse):
    it = iter(refs)
    q_ref, kc_ref, kp_ref, vc_ref, vp_ref = (next(it) for _ in range(5))
    sink_ref = next(it) if has_sinks else None
    o_ref = next(it)
    lse_ref = next(it) if want_lse else None
    qb = pl.program_id(2)
    w = (npv + 1) * QBLK
    pv_rows = npv * QBLK
    lane = lax.broadcasted_iota(jnp.int32, (QBLK, LANES), 1)
    first_half = lane < HEAD_DIM
    qi = lax.broadcasted_iota(jnp.int32, (QBLK, w), 0)
    kj = lax.broadcasted_iota(jnp.int32, (QBLK, w), 1)
    dist = qi + npv * QBLK - kj
    band = (dist >= 0) & (dist <= max_dist)
    for sub in range(tq // QBLK):
        r0 = sub * QBLK
        kstart = qb * tq + r0 - npv * QBLK
        mask = band & (kj + kstart >= 0)
        lse_tile = jnp.zeros((QBLK, LANES), F32)
        for g0 in range(0, N_PAIRS, BAND_GROUP):
            staged = []
            for pi in range(g0, g0 + BAND_GROUP):
                cl = slice(kv_pair_of(pi) * LANES, (kv_pair_of(pi) + 1) * LANES)
                k_parts, v_parts = [], []
                if r0 < pv_rows:
                    k_parts.append(kp_ref[r0:pv_rows, cl])
                    v_parts.append(vp_ref[r0:pv_rows, cl])
                cs = max(r0 - pv_rows, 0)
                k_parts.append(kc_ref[cs:r0 + QBLK, cl])
                v_parts.append(vc_ref[cs:r0 + QBLK, cl])
                kwin = k_parts[0] if len(k_parts) == 1 else jnp.concatenate(k_parts, axis=0)
                vwin = v_parts[0] if len(v_parts) == 1 else jnp.concatenate(v_parts, axis=0)
                qp = q_ref[r0:r0 + QBLK, pi * LANES:(pi + 1) * LANES]
                for e in range(2):
                    qe = jnp.where(first_half if e == 0 else jnp.logical_not(first_half), qp, 0)
                    s = lax.dot_general(qe, kwin, _NT, preferred_element_type=F32)
                    staged.append((pi, e, jnp.where(mask, s, NEG_INF), vwin))
            outs = {}
            for pi, e, s, vwin in staged:
                m = jnp.max(s, axis=-1, keepdims=True)
                if has_sinks:
                    sk = sink_ref[2 * pi + e]
                    m = jnp.maximum(m, sk)
                p = jnp.exp2(s - m)
                l = jnp.sum(p, axis=-1, keepdims=True)
                if has_sinks:
                    l = l + jnp.exp2(sk - m)
                pv = jnp.dot(p.astype(BF16), vwin, preferred_element_type=F32)
                outs[(pi, e)] = pv / l
                if want_lse:
                    lse_tile = jnp.where(lane == 2 * pi + e, m + jnp.log2(l), lse_tile)
            for pi in range(g0, g0 + BAND_GROUP):
                o_ref[r0:r0 + QBLK, pi * LANES:(pi + 1) * LANES] = jnp.where(
                    first_half, outs[(pi, 0)], outs[(pi, 1)])
        if want_lse:
            lse_ref[r0:r0 + QBLK, :] = lse_tile


def _banded_attention(q_arr, k_arr, v_arr, *, tq, npv, max_dist, q_off, k_off, v_off,
                      kv_pairs, kv_pair_of, sinks=None, want_lse=False):
    b, dil, l, _ = q_arr.shape
    qw, kw = N_PAIRS * LANES, kv_pairs * LANES
    pv = npv * QBLK
    assert tq % pv == 0
    in_specs = [
        pl.BlockSpec((None, None, tq, qw), lambda bi, r, i: (bi, r, i, q_off)),
        pl.BlockSpec((None, None, tq, kw), lambda bi, r, i: (bi, r, i, k_off)),
        pl.BlockSpec((None, None, pv, kw), lambda bi, r, i: (bi, r, jnp.maximum(i * (tq // pv) - 1, 0), k_off)),
        pl.BlockSpec((None, None, tq, kw), lambda bi, r, i: (bi, r, i, v_off)),
        pl.BlockSpec((None, None, pv, kw), lambda bi, r, i: (bi, r, jnp.maximum(i * (tq // pv) - 1, 0), v_off)),
    ]
    args = [q_arr, k_arr, k_arr, v_arr, v_arr]
    if sinks is not None:
        in_specs.append(pl.BlockSpec(memory_space=pltpu.SMEM))
        args.append(sinks)
    out_specs = [pl.BlockSpec((None, None, tq, qw), lambda bi, r, i: (bi, r, i, 0))]
    out_shape = [jax.ShapeDtypeStruct((b, dil, l, qw), F32)]
    if want_lse:
        out_specs.append(pl.BlockSpec((None, None, tq, LANES), lambda bi, r, i: (bi, r, i, 0)))
        out_shape.append(jax.ShapeDtypeStruct((b, dil, l, LANES), F32))
    res = pl.pallas_call(
        functools.partial(_band_kernel, tq=tq, npv=npv, max_dist=max_dist, kv_pair_of=kv_pair_of,
                          has_sinks=sinks is not None, want_lse=want_lse),
        grid=(b, dil, l // tq),
        in_specs=in_specs,
        out_specs=out_specs,
        out_shape=out_shape,
        compiler_params=_params(3),
        name=f"band_d{dil}_w{max_dist}",
    )(*args)
    return res if want_lse else res[0]


def _compress_kernel(c_ref, pos_ref, w_ref, o_ref):
    c = c_ref[...]
    top = jnp.dot((c + pos_ref[0:1, :]).astype(BF16), w_ref[0], preferred_element_type=F32)
    bot = jnp.dot((c + pos_ref[1:2, :]).astype(BF16), w_ref[1], preferred_element_type=F32)
    nc = c.shape[0]
    o_ref[...] = (top + pltpu.roll(bot, nc - 1, 0)).astype(o_ref.dtype)


def _compress(chunks, pos, w):
    b, hk, nc, cw = chunks.shape
    return pl.pallas_call(
        _compress_kernel,
        grid=(b, hk),
        in_specs=[pl.BlockSpec((None, None, nc, cw), lambda bi, h: (bi, h, 0, 0)),
                  pl.BlockSpec((2, cw), lambda bi, h: (0, 0)),
                  pl.BlockSpec((2, cw, HEAD_DIM), lambda bi, h: (0, 0, 0))],
        out_specs=pl.BlockSpec((None, None, nc, HEAD_DIM), lambda bi, h: (bi, h, 0, 0)),
        out_shape=jax.ShapeDtypeStruct((b, hk, nc, HEAD_DIM), BF16),
        compiler_params=_params(2),
        name="nsa_compress",
    )(chunks, pos, w)


def _stack_group_queries(q_ref, mp, half):
    pairs = [C_GROUP * mp + i for i in range(C_GROUP)]
    return jnp.concatenate(
        [jnp.where(half, q_ref[:, pr * LANES:(pr + 1) * LANES], 0) for pr in pairs], axis=0)


def _store_group_heads(o_ref, val, mp, e, first_half):
    for i in range(C_GROUP):
        ol = slice((C_GROUP * mp + i) * LANES, (C_GROUP * mp + i + 1) * LANES)
        rows = slice(i * QBLK, (i + 1) * QBLK)
        if e == 0:
            o_ref[:, ol] = val[rows]
        else:
            o_ref[:, ol] = jnp.where(first_half, o_ref[:, ol], val[rows])


def _nsa_cmp_kernel(q_ref, kc_ref, vc_ref, ocmp_ref, sel_ref, impt_ref, *, n_sel):
    nc = kc_ref.shape[0]
    t0 = pl.program_id(1) * QBLK
    lane = lax.broadcasted_iota(jnp.int32, (QBLK, LANES), 1)
    first_half = lane < HEAD_DIM
    pad = SUBLANES

    n_free = max(n_sel - 3, 0)

    def body(ncols):
        nv = ncols // SEL_PER_CMP
        assert nv >= n_sel
        jj = lax.broadcasted_iota(jnp.int32, (nv, QBLK), 0)
        cur = (t0 + lax.broadcasted_iota(jnp.int32, (nv, QBLK), 1)) // C_SEL_LEN
        forced = (jj == 0) | (jj == cur) | (jj == cur - 1)
        bvalid = jj <= cur
        qi_c = lax.broadcasted_iota(jnp.int32, (QBLK, ncols), 0)
        nn_c = lax.broadcasted_iota(jnp.int32, (QBLK, ncols), 1)
        cvalid = nn_c * C_CMP_STRIDE + (C_CMP_LEN - 1) <= t0 + qi_c
        cvalid = jnp.concatenate([cvalid] * C_GROUP, axis=0)
        impt_ref[0:pad, :] = jnp.zeros((pad, QBLK), F32)
        if ncols < nc:
            impt_ref[pad + ncols:pad + nc, :] = jnp.zeros((nc - ncols, QBLK), F32)

        scores = []
        for kh in range(C_KV_HEADS):
            mp, e = divmod(kh, 2)
            cl = slice(mp * LANES, (mp + 1) * LANES)
            half = first_half if e == 0 else jnp.logical_not(first_half)
            qst = _stack_group_queries(q_ref, mp, half)
            sc = lax.dot_general(qst, kc_ref[0:ncols, cl], _NT, preferred_element_type=F32)
            scores.append(jnp.where(cvalid, sc, NEG_INF))

        imps = []
        for kh in range(C_KV_HEADS):
            mp, e = divmod(kh, 2)
            cl = slice(mp * LANES, (mp + 1) * LANES)
            sc = scores[kh]
            mx = jnp.max(sc, axis=-1, keepdims=True)
            mx = jnp.where(mx > NEG_INF, mx, 0.0)
            ee = jnp.exp2(sc - mx)
            pc = ee / jnp.maximum(jnp.sum(ee, axis=-1, keepdims=True), 1e-30)
            ocmp = jnp.dot(pc.astype(BF16), vc_ref[0:ncols, cl], preferred_element_type=F32)
            _store_group_heads(ocmp_ref, ocmp, mp, e, first_half)
            imp = pc[0:QBLK]
            for g in range(1, C_GROUP):
                imp = imp + pc[g * QBLK:(g + 1) * QBLK]
            imps.append(imp)

        for kh in range(C_KV_HEADS):
            for c in range(ncols // QBLK):
                impt_ref[pad + c * QBLK:pad + (c + 1) * QBLK, :] = imps[kh][:, c * QBLK:(c + 1) * QBLK].T
            imp_s = C_SEL_OVERLAP[0] * impt_ref[pl.ds(pad - 1, nv, stride=SEL_PER_CMP), :]
            for o_off in range(1, len(C_SEL_OVERLAP)):
                imp_s = imp_s + C_SEL_OVERLAP[o_off] * impt_ref[pl.ds(pad - 1 + o_off, nv, stride=SEL_PER_CMP), :]
            score = jnp.where(forced, NEG_INF, jnp.where(bvalid, imp_s, -1.0))
            selt = jnp.where(forced, 1.0, 0.0)
            for _ in range(n_free):
                best = jnp.max(score, axis=0, keepdims=True)
                first = jnp.min(jnp.where(score == best, jj, nv), axis=0, keepdims=True)
                hit = jj == first
                selt = jnp.where(hit, 1.0, selt)
                score = jnp.where(hit, NEG_INF, score)
            sel_ref[kh, 0:nv, :] = selt
            if nv < LANES:
                sel_ref[kh, nv:LANES, :] = jnp.zeros((LANES - nv, QBLK), F32)

    col_step = min(CMP_COL_STEP, nc)
    n_variants = nc // col_step
    tokens_per_step = col_step * C_CMP_STRIDE
    for k in range(n_variants):
        pl.when(t0 // tokens_per_step == k)(functools.partial(body, (k + 1) * col_step))


def _nsa_slc_kernel(q_ref, sel_ref, ks_ref, vst_ref, oslc_ref, acc_ref):
    gq = C_GROUP * QBLK
    blocks_per_tile = SEL_TILE // C_SEL_LEN
    bias_rows = BF16_SUBLANES
    assert blocks_per_tile <= bias_rows
    t0 = pl.program_id(1) * QBLK
    lane = lax.broadcasted_iota(jnp.int32, (QBLK, LANES), 1)
    first_half = lane < HEAD_DIM
    row = lax.broadcasted_iota(jnp.int32, (LANES, QBLK), 0)
    top_rows = row < HEAD_DIM
    n_full = t0 // SEL_TILE
    key_in_tile = lax.broadcasted_iota(jnp.int32, (SEL_TILE, gq), 0)
    query_pos = t0 + lax.broadcasted_iota(jnp.int32, (SEL_TILE, gq), 1) % QBLK
    block_of_key = lax.broadcasted_iota(jnp.int32, (SEL_TILE, LANES), 0) // C_SEL_LEN
    block_onehot = jnp.where(
        block_of_key == lax.broadcasted_iota(jnp.int32, (SEL_TILE, LANES), 1), 1.0, 0.0).astype(BF16)
    bias_pad = jnp.zeros((LANES - bias_rows, gq), BF16)

    q_t = [q_ref[:, pr * LANES:(pr + 1) * LANES].astype(F32).T for pr in range(N_PAIRS)]
    qts = []
    for kh in range(C_KV_HEADS):
        mp, e = divmod(kh, 2)
        keep_rows = top_rows if e == 0 else jnp.logical_not(top_rows)
        qts.append(jnp.concatenate(
            [jnp.where(keep_rows, q_t[C_GROUP * mp + i], 0).astype(BF16) for i in range(C_GROUP)],
            axis=1))
    acc_ref[...] = jnp.zeros((C_KV_HEADS, LANES, gq), F32)

    def tile_step(kt, carry, diagonal):
        k0 = pl.multiple_of(kt * SEL_TILE, SEL_TILE)
        b0 = pl.multiple_of(kt * blocks_per_tile, blocks_per_tile)
        sts = []
        for kh in range(C_KV_HEADS):
            mp = kh // 2
            cl = slice(mp * LANES, (mp + 1) * LANES)
            keys = jnp.concatenate([ks_ref[pl.ds(k0, SEL_TILE), cl], block_onehot], axis=1)
            picked = sel_ref[kh, pl.ds(b0, blocks_per_tile), :]
            bias = jnp.concatenate([(picked - 1.0) * -MASKED] * C_GROUP, axis=1)
            bias = jnp.concatenate([bias, jnp.zeros((bias_rows - blocks_per_tile, gq), F32)], axis=0)
            queries = jnp.concatenate([qts[kh], bias.astype(BF16), bias_pad], axis=0)
            st = jnp.dot(keys, queries, preferred_element_type=F32)
            if diagonal:
                st = jnp.where(k0 + key_in_tile <= query_pos, st, MASKED)
            sts.append(st)
        new_carry = []
        for kh in range(C_KV_HEADS):
            mp = kh // 2
            cl = slice(mp * LANES, (mp + 1) * LANES)
            st = sts[kh]
            m_old = carry[kh]
            m_new = jnp.maximum(m_old, jnp.max(st, axis=0, keepdims=True))
            alpha = jnp.exp2(m_old - m_new)
            p = jnp.exp2(st - m_new)
            acc_ref[kh] = alpha * acc_ref[kh] + jnp.dot(
                vst_ref[kh % 2, kt, cl, :], p.astype(BF16), preferred_element_type=F32)
            new_carry += [m_new]
        return tuple(new_carry)

    init = (jnp.full((1, gq), NEG_INF, F32),) * C_KV_HEADS
    carry = lax.fori_loop(0, n_full, functools.partial(tile_step, diagonal=False), init)
    carry = lax.fori_loop(n_full, n_full + 1, functools.partial(tile_step, diagonal=True), carry)
    for mp in range(C_KV_HEADS // 2):
        ots = [acc_ref[2 * mp + e] / acc_ref[2 * mp + e, HEAD_DIM * (1 - e):HEAD_DIM * (1 - e) + 1, :]
               for e in range(2)]
        for i in range(C_GROUP):
            ol = slice((C_GROUP * mp + i) * LANES, (C_GROUP * mp + i + 1) * LANES)
            qs = slice(i * QBLK, (i + 1) * QBLK)
            oslc_ref[:, ol] = jnp.where(first_half, ots[0][:, qs].T, ots[1][:, qs].T)


def _nsa_select(q_arr, kcmp, vcmp, ks_arr, vst_arr, *, ks_off):
    b, s, _ = q_arr.shape
    nc = kcmp.shape[1]
    ns = nc // SEL_PER_CMP
    assert ns <= LANES and s % SEL_TILE == 0
    qw = N_PAIRS * LANES
    gq = C_GROUP * QBLK
    q_spec = pl.BlockSpec((None, QBLK, qw), lambda bi, i: (bi, i, 0))
    o_spec = pl.BlockSpec((None, QBLK, qw), lambda bi, i: (bi, i, 0))
    sel_spec = pl.BlockSpec((None, C_KV_HEADS, LANES, QBLK), lambda bi, i: (bi, 0, 0, i))
    o_shape = jax.ShapeDtypeStruct((b, s, qw), F32)
    o_cmp, sel = pl.pallas_call(
        functools.partial(_nsa_cmp_kernel, n_sel=min(C_N_SEL, ns)),
        grid=(b, s // QBLK),
        in_specs=[q_spec,
                  pl.BlockSpec((None, nc, C_KV), lambda bi, i: (bi, 0, 0)),
                  pl.BlockSpec((None, nc, C_KV), lambda bi, i: (bi, 0, 0))],
        out_specs=[o_spec, sel_spec],
        out_shape=[o_shape, jax.ShapeDtypeStruct((b, C_KV_HEADS, LANES, s), F32)],
        scratch_shapes=[pltpu.VMEM((SUBLANES + nc, QBLK), F32)],
        compiler_params=_params(2),
        name="nsa_compressed",
    )(q_arr, kcmp, vcmp)
    o_slc = pl.pallas_call(
        _nsa_slc_kernel,
        grid=(b, s // QBLK),
        in_specs=[q_spec,
                  sel_spec,
                  pl.BlockSpec((None, s, C_KV), lambda bi, i: (bi, 0, ks_off)),
                  pl.BlockSpec((None, 2, s // SEL_TILE, C_KV, SEL_TILE), lambda bi, i: (bi, 0, 0, 0, 0))],
        out_specs=o_spec,
        out_shape=o_shape,
        scratch_shapes=[pltpu.VMEM((C_KV_HEADS, LANES, gq), F32)],
        compiler_params=_params(2),
        name="nsa_selected",
    )(q_arr, sel, ks_arr, vst_arr)
    return o_cmp, o_slc


def _head_cols(tile, col, first_half):
    tm = tile.shape[0]
    a = jnp.broadcast_to(tile[:, col:col + 1], (tm, LANES))
    b = jnp.broadcast_to(tile[:, col + 1:col + 2], (tm, LANES))
    return jnp.where(first_half, a, b)


def _post_kernel(*refs, kind, n_staged, n_dil_outs):
    it = iter(refs)
    n_branch = {"A": 3, "B": 1, "C": 3}[kind]
    o_refs = [next(it) for _ in range(n_branch)]
    aux_refs = [next(it) for _ in range(3)] if kind == "A" else []
    x_ref, xb_ref, p_ref, wz_ref = (next(it) for _ in range(4))
    wgl_ref = next(it) if kind == "C" else None
    wo_ref, g_ref, b_ref, wg_ref, wp_ref, out_ref, outb_ref = (next(it) for _ in range(7))
    dil_out_refs = [next(it) for _ in range(n_dil_outs)]
    u_ref, z_ref = next(it), next(it)
    stage_refs = [next(it) for _ in range(n_staged + (1 if n_dil_outs else 0))]
    tm = x_ref.shape[0]

    def token_order(ref):
        if len(ref.shape) == 2:
            return lambda j: ref[:, j * LANES:(j + 1) * LANES]
        dil = ref.shape[0]
        stage = stage_refs.pop(0)
        for j in range(ref.shape[2] // LANES):
            for r in range(dil):
                stage[j, pl.ds(r, tm // dil, stride=dil), :] = ref[r, :, j * LANES:(j + 1) * LANES]
        return lambda j: stage[j]

    o_cols = [token_order(r) for r in o_refs]
    aux_cols = [token_order(r) for r in aux_refs]

    n_chunks = tm // POST_ROWS
    rows = [slice(c * POST_ROWS, (c + 1) * POST_ROWS) for c in range(n_chunks)]
    first_half = lax.broadcasted_iota(jnp.int32, (POST_ROWS, LANES), 1) < HEAD_DIM
    gates = [None] * n_chunks
    for c, rs in enumerate(rows):
        xb = xb_ref[rs, :]
        z_ref[rs, :] = jnp.dot(xb, wz_ref[...], preferred_element_type=F32)
        if kind == "C":
            gates[c] = jax.nn.sigmoid(jnp.dot(xb, wgl_ref[...], preferred_element_type=F32))

    hs = []
    for c, rs in enumerate(rows):
        if kind == "A":
            lses = [col(0)[rs] for col in aux_cols]
            mx = jnp.maximum(jnp.maximum(lses[0], lses[1]), lses[2])
            ws = [jnp.exp2(v - mx) for v in lses]
            den = ws[0] + ws[1] + ws[2]
            ws = [v / den for v in ws]
        for pi in range(N_PAIRS):
            cl = slice(pi * LANES, (pi + 1) * LANES)
            if kind == "A":
                o = _head_cols(ws[0], 2 * pi, first_half) * o_cols[0](pi)[rs]
                for g in range(1, 3):
                    o = o + _head_cols(ws[g], 2 * pi, first_half) * o_cols[g](pi)[rs]
            elif kind == "B":
                o = o_cols[0](pi)[rs]
            else:
                o = _head_cols(gates[c], 2 * pi, first_half) * o_cols[0](pi)[rs]
                for br in range(1, 3):
                    o = o + _head_cols(gates[c], br * N_HEADS + 2 * pi, first_half) * o_cols[br](pi)[rs]
            z = z_ref[rs, cl]
            u_ref[rs, cl] = (o * (z * jax.nn.sigmoid(z))).astype(BF16)
        hs.append(jnp.dot(u_ref[rs, :], wo_ref[...], preferred_element_type=F32))

    pre = []
    for c, rs in enumerate(rows):
        y = DEEPNORM_ALPHA * x_ref[rs, :] + hs[c]
        mu = jnp.mean(y, axis=-1, keepdims=True)
        yc = y - mu
        var = jnp.mean(yc * yc, axis=-1, keepdims=True)
        yn = yc * lax.rsqrt(var + LN_EPS) * g_ref[...] + b_ref[...]
        gate_logits = jnp.dot(yn.astype(BF16), wg_ref[...], preferred_element_type=F32)
        pp = jnp.dot(p_ref[rs, :].astype(BF16), wp_ref[...], preferred_element_type=F32)
        pre.append((yn, gate_logits, pp))

    for c, rs in enumerate(rows):
        yn, gate_logits, pp = pre[c]
        x_new = yn + jax.nn.sigmoid(gate_logits) * pp
        out_ref[rs, :] = x_new
        outb_ref[rs, :] = x_new.astype(BF16)
        if dil_out_refs:
            for j in range(x_new.shape[1] // LANES):
                stage_refs[-1][j, rs, :] = x_new[:, j * LANES:(j + 1) * LANES]
    if dil_out_refs:
        stage = stage_refs[-1]
        for o_ref in dil_out_refs:
            dil = o_ref.shape[0]
            for j in range(x_new.shape[1] // LANES):
                for r in range(dil):
                    o_ref[r, :, j * LANES:(j + 1) * LANES] = stage[j, pl.ds(r, tm // dil, stride=dil), :].astype(BF16)


def _post(kind, o_list, aux_list, x, xb, p, w_z, w_gl, w_out, ln_g, ln_b, w_gate, w_proj, batch, out_dils=()):
    t, d = x.shape
    tm = POST_TILE
    row = lambda w: pl.BlockSpec((tm, w), lambda i: (i, 0))
    full = lambda a: pl.BlockSpec(a.shape, lambda i: (0,) * a.ndim)

    branch_specs, stages = [], []
    for a in list(o_list) + list(aux_list):
        if a.ndim == 2:
            branch_specs.append(row(a.shape[1]))
        else:
            _, dil, sub_len, w = a.shape
            n_seq_tiles = sub_len * dil // tm
            branch_specs.append(pl.BlockSpec(
                (None, dil, tm // dil, w), lambda i, n=n_seq_tiles: (i // n, 0, i % n, 0)))
            stages.append(pltpu.VMEM((w // LANES, tm, LANES), F32))
    n_staged = len(stages)
    if out_dils:
        stages.append(pltpu.VMEM((d // LANES, tm, LANES), F32))
    seq_tiles = t // batch // tm
    weights = [w_z] + ([w_gl] if kind == "C" else []) + [w_out, ln_g, ln_b, w_gate, w_proj]
    args = list(o_list) + list(aux_list) + [x, xb, p[0]] + weights
    p_arr, p_layer = p
    p_spec = pl.BlockSpec((None, tm, p_arr.shape[2]), lambda i: (p_layer, i, 0))
    in_specs = branch_specs + [row(d), row(d), p_spec] + [full(w) for w in weights]
    return pl.pallas_call(
        functools.partial(_post_kernel, kind=kind, n_staged=n_staged, n_dil_outs=len(out_dils)),
        grid=(t // tm,),
        in_specs=in_specs,
        out_specs=[row(d), row(d)] + [
            pl.BlockSpec((None, dil, tm // dil, d), lambda i, n=seq_tiles: (i // n, 0, i % n, 0)) for dil in out_dils],
        out_shape=[jax.ShapeDtypeStruct((t, d), F32), jax.ShapeDtypeStruct((t, d), BF16)] + [
            jax.ShapeDtypeStruct((batch, dil, t // batch // dil, d), BF16) for dil in out_dils],
        scratch_shapes=[pltpu.VMEM((tm, d), BF16), pltpu.VMEM((tm, d), F32)] + stages,
        compiler_params=_params(1),
        name=f"post_{kind}",
    )(*args)


def _cast_kernel(x_ref, xb_ref, *rest):
    dil_out_refs, stage = rest[:-1], rest[-1]
    tm, d = x_ref.shape
    x = x_ref[...]
    xb_ref[...] = x.astype(BF16)
    if dil_out_refs:
        for j in range(d // LANES):
            stage[j] = x[:, j * LANES:(j + 1) * LANES]
        for o_ref in dil_out_refs:
            dil = o_ref.shape[0]
            for j in range(d // LANES):
                for r in range(dil):
                    o_ref[r, :, j * LANES:(j + 1) * LANES] = stage[j, pl.ds(r, tm // dil, stride=dil), :].astype(BF16)


def _cast_stream(x, batch, out_dils):
    t, d = x.shape
    tm = POST_TILE
    seq_tiles = t // batch // tm
    row = pl.BlockSpec((tm, d), lambda i: (i, 0))
    return pl.pallas_call(
        _cast_kernel,
        grid=(t // tm,),
        in_specs=[row],
        out_specs=[row] + [
            pl.BlockSpec((None, dil, tm // dil, d), lambda i, n=seq_tiles: (i // n, 0, i % n, 0)) for dil in out_dils],
        out_shape=[jax.ShapeDtypeStruct((t, d), BF16)] + [
            jax.ShapeDtypeStruct((batch, dil, t // batch // dil, d), BF16) for dil in out_dils],
        scratch_shapes=[pltpu.VMEM((d // LANES, tm, LANES), F32)],
        compiler_params=_params(1),
        name="cast_stream",
    )(x)


def _rope_tables(seq_len):
    inv = 1.0 / (ROPE_THETA ** (jnp.arange(0, HEAD_DIM, 2, dtype=F32) / HEAD_DIM))
    ang = jnp.arange(seq_len, dtype=F32)[:, None] * inv[None, :]
    cos, sin = lax.optimization_barrier((jnp.cos(ang), jnp.sin(ang)))
    zero = jnp.zeros_like(sin)
    cos_t = jnp.concatenate([cos] * 4, axis=1)
    sa_t = jnp.concatenate([-sin, zero, -sin, zero], axis=1)
    sb_t = jnp.concatenate([zero, sin, zero, sin], axis=1)
    return cos_t, sa_t, sb_t


def _residue_major(a, dil):
    *lead, s, w = a.shape
    return jnp.swapaxes(a.reshape(*lead, s // dil, dil, w), -3, -2)


def _head_cols_index(head_order):
    return np.concatenate([np.arange(h * HEAD_DIM, (h + 1) * HEAD_DIM) for h in head_order])


_B_HEAD_ORDER = [e * (N_HEADS // B_KV_HEADS) + i for i in range(N_PAIRS) for e in range(2)]
_C_HEAD_ORDER = [C_GROUP * (2 * m + e) + i for m in range(C_KV_HEADS // 2) for i in range(C_GROUP) for e in range(2)]


def _prep_a_weights(a_w_in):
    wd = N_HEADS * HEAD_DIM
    col = np.arange(a_w_in.shape[-1])
    is_q = (col < 3 * len(A_GROUPS) * wd) & (col // wd % 3 == 0)
    scale = np.where(is_q, HEAD_DIM ** -0.5 * LOG2E, 1.0).astype(np.float32)
    return (a_w_in * scale).astype(BF16)


def _mixer_a(xbs, b, s, a_w, j, tabs):
    wd = N_HEADS * HEAD_DIM
    outs, lses = [], []
    for gi, (window, dil) in enumerate(A_GROUPS):
        qkv = _proj(xbs[dil], a_w, BF16, [_residue_major(t, dil) for t in tabs],
                    n_rope_cols=2 * wd, w_cols=(j, 3 * gi * wd, 3 * wd))
        o, lse = _banded_attention(
            qkv, qkv, qkv, tq=min(512, s // dil), npv=1, max_dist=window // dil,
            q_off=0, k_off=1, v_off=2, kv_pairs=N_PAIRS, kv_pair_of=lambda pi: pi, want_lse=True)
        if dil == 1:
            o, lse = o.reshape(b * s, wd), lse.reshape(b * s, LANES)
        outs.append(o)
        lses.append(lse)
    return outs, lses, a_w[j, :, -wd:]


def _mixer_b(xbs, b, s, w_in, sinks, tabs):
    wd = N_HEADS * HEAD_DIM
    kvw = B_KV_HEADS * HEAD_DIM
    perm = _head_cols_index(_B_HEAD_ORDER)
    wq = w_in[:, :wd][:, perm] * (HEAD_DIM ** -0.5 * LOG2E)
    wk = w_in[:, wd:wd + kvw]
    wv = w_in[:, wd + kvw:wd + 2 * kvw]
    wz = w_in[:, wd + 2 * kvw:][:, perm]
    tabs1 = [t[None] for t in tabs]
    qk = _proj(xbs[1], jnp.concatenate([wq, wk], axis=1).astype(BF16), BF16, tabs1, n_rope_cols=wd + kvw)
    v = _proj(xbs[1], wv.astype(BF16), BF16)
    o = _banded_attention(
        qk, qk, v, tq=min(256, s), npv=1, max_dist=B_WINDOW - 1,
        q_off=0, k_off=wd // kvw, v_off=0, kv_pairs=1, kv_pair_of=lambda pi: 0,
        sinks=sinks[np.asarray(_B_HEAD_ORDER)].astype(F32) * LOG2E)
    return [o.reshape(b * s, wd)], [], wz, perm


def _mixer_c(xbs, b, s, w_in, w_ck, w_cv, pos, tabs):
    wd = N_HEADS * HEAD_DIM
    perm = _head_cols_index(_C_HEAD_ORDER)
    cols = np.cumsum([0, wd] + [C_KV] * 6 + [3 * N_HEADS, wd])
    part = lambda i: w_in[:, cols[i]:cols[i + 1]]
    wq = part(0)[:, perm] * (HEAD_DIM ** -0.5 * LOG2E)
    w_gl = part(7).reshape(-1, 3, N_HEADS)[:, :, np.asarray(_C_HEAD_ORDER)].reshape(-1, 3 * N_HEADS)
    w_gl = jnp.pad(w_gl, ((0, 0), (0, LANES - 3 * N_HEADS)))
    w_att = jnp.concatenate([wq, part(3), part(5), part(4), part(6)], axis=1)
    tabs1 = [t[None] for t in tabs]
    att = _proj(xbs[1], w_att.astype(BF16), BF16, tabs1, n_rope_cols=wd + 2 * C_KV)
    cmp_in = _proj(xbs[1], jnp.concatenate([part(1), part(2)], axis=1).astype(BF16), F32, tabs1,
                   n_rope_cols=C_KV, tn=C_KV).reshape(b, s, 2 * C_KV)

    nc = s // C_CMP_STRIDE
    cw = C_CMP_STRIDE * HEAD_DIM

    def chunks(t):
        return jnp.transpose(t.reshape(b, s, C_KV_HEADS, HEAD_DIM), (0, 2, 1, 3)).reshape(b, C_KV_HEADS, nc, cw)

    pos2 = pos.reshape(2, cw)

    def compressed(t, w):
        c = _compress(chunks(t), pos2, w.reshape(2, cw, HEAD_DIM).astype(BF16))
        return jnp.transpose(c, (0, 2, 1, 3)).reshape(b, nc, C_KV)

    kcmp, vcmp = compressed(cmp_in[:, :, :C_KV], w_ck), compressed(cmp_in[:, :, C_KV:], w_cv)
    att3 = att.reshape(b, s, -1)
    vs = att3[:, :, wd + 2 * C_KV:wd + 3 * C_KV]
    vst = jnp.transpose(vs.reshape(b, s // SEL_TILE, SEL_TILE, C_KV), (0, 1, 3, 2))
    in_half0 = (jnp.arange(C_KV) % PAIR < HEAD_DIM)[None, None, :, None]
    vst = jnp.stack([jnp.where(in_half0, vst, 1), jnp.where(in_half0, 1, vst)], axis=1)
    o_cmp, o_slc = _nsa_select(att3, kcmp, vcmp, att3, vst, ks_off=wd // C_KV)
    o_win = _banded_attention(
        att, att, att, tq=C_WINDOW, npv=C_WINDOW // QBLK, max_dist=C_WINDOW - 1,
        q_off=0, k_off=wd // C_KV + 1, v_off=wd // C_KV + 3, kv_pairs=C_KV_HEADS // 2,
        kv_pair_of=lambda pi: pi // C_GROUP)
    outs = [o.reshape(b * s, wd) for o in (o_cmp, o_slc, o_win)]
    return outs, [], (part(8)[:, perm], w_gl), perm


def _layer(i, xt, xbs, b, s, tabs, p, a_w_in, a_w_out, b_w_in, b_sinks, b_w_out, c_w_in, c_w_ck, c_w_cv, c_pos,
           c_w_out, ln_g, ln_b, ple_w_proj, ple_w_gate):
    d = xt.shape[1]
    j, kind = divmod(i, N_MIXERS)
    w_gl = None
    if kind == 0:
        outs, aux, w_z = _mixer_a(xbs, b, s, a_w_in, j, tabs)
        w_out, name = a_w_out[j], "A"
    elif kind == 1:
        outs, aux, w_z, perm = _mixer_b(xbs, b, s, b_w_in[j], b_sinks[j], tabs)
        w_out, name = b_w_out[j][perm, :], "B"
    else:
        outs, aux, (w_z, w_gl), perm = _mixer_c(xbs, b, s, c_w_in[j], c_w_ck[j], c_w_cv[j], c_pos[j], tabs)
        w_out, name = c_w_out[j][perm, :], "C"
        w_gl = w_gl.astype(BF16)
    next_dils = _layer_dils(i + 1)
    res = _post(name, outs, aux, xt, xbs[1].reshape(b * s, d), (p.reshape(p.shape[0], b * s, -1), i), w_z.astype(BF16), w_gl,
                w_out.astype(BF16), ln_g[i].reshape(1, d), ln_b[i].reshape(1, d),
                ple_w_gate[i].astype(BF16), ple_w_proj[i].astype(BF16), b, out_dils=next_dils)
    new_xbs = {1: res[1].reshape(b, 1, s, d)}
    new_xbs.update(zip(next_dils, res[2:]))
    return res[0], new_xbs


def _layer_dils(i):
    if i < DEPTH and i % N_MIXERS == 0:
        return tuple(dil for _, dil in A_GROUPS if dil > 1)
    return ()


def kernel(x, p, a_w_in, a_w_out, b_w_in, b_sinks, b_w_out, c_w_in, c_w_ck, c_w_cv, c_pos, c_w_out,
           ln_g, ln_b, ple_w_proj, ple_w_gate):
    b, s, d = x.shape
    assert d == N_HEADS * HEAD_DIM and s % (QBLK * A_GROUPS[-1][1]) == 0 and s % C_WINDOW == 0
    tabs = _rope_tables(s)
    xt = x.reshape(b * s, d)
    dils0 = _layer_dils(0)
    copies = _cast_stream(xt, b, dils0)
    xbs = {1: copies[0].reshape(b, 1, s, d)}
    xbs.update(zip(dils0, copies[1:]))
    a_w_in = _prep_a_weights(a_w_in)
    for i in range(DEPTH):
        xt, xbs = _layer(i, xt, xbs, b, s, tabs, p, a_w_in, a_w_out, b_w_in, b_sinks, b_w_out, c_w_in, c_w_ck,
                         c_w_cv, c_pos, c_w_out, ln_g, ln_b, ple_w_proj, ple_w_gate)
    return xt.reshape(b, s, d)
```

```python
import functools

import numpy as np
import jax
import jax.numpy as jnp
from jax import lax
from jax.experimental import pallas as pl
from jax.experimental.pallas import tpu as pltpu

F32 = jnp.float32
BF16 = jnp.bfloat16

LANES = 128
SUBLANES = 8
BF16_SUBLANES = 16
VMEM_LIMIT_BYTES = 56 * 1024 * 1024

HEAD_DIM = 64
HALF = HEAD_DIM // 2
PAIR = 2 * HEAD_DIM
assert PAIR == LANES
N_HEADS = 16
N_PAIRS = N_HEADS // 2
ROPE_THETA = 10000.0
QBLK = 128
BAND_GROUP = 4
PROJ_ROWS = 128
POST_TILE = 512
POST_ROWS = 256
LN_EPS = 1e-5
DEPTH = 4
N_MIXERS = 3
DEEPNORM_ALPHA = (2 * DEPTH) ** 0.25
A_GROUPS = ((128, 1), (512, 4), (2048, 16))
B_KV_HEADS = 2
B_WINDOW = 128
C_KV_HEADS = 4
C_GROUP = N_HEADS // C_KV_HEADS
C_KV = C_KV_HEADS * HEAD_DIM
C_CMP_STRIDE = 16
C_CMP_LEN = 32
C_SEL_LEN = 64
C_N_SEL = 16
C_WINDOW = 512
C_SEL_OVERLAP = (1.0, 2.0, 2.0, 2.0, 1.0)
SEL_PER_CMP = C_SEL_LEN // C_CMP_STRIDE
SEL_TILE = 512
CMP_COL_STEP = 128
LOG2E = 1.4426950408889634
NEG_INF = float("-inf")
MASKED = -1e30

_NT = (((1,), (1,)), ((), ()))


def _params(n_grid):
    return pltpu.CompilerParams(
        dimension_semantics=("arbitrary",) * n_grid, vmem_limit_bytes=VMEM_LIMIT_BYTES)


def _proj_kernel(*refs, n_rope_tiles, n_tiles):
    it = iter(refs)
    x_ref, w_ref = next(it), next(it)
    if n_rope_tiles:
        tab_refs = [next(it), next(it), next(it)]
    o_ref = next(it)
    dil, sub, _ = x_ref.shape
    tn = w_ref.shape[1]
    n_chunks = dil * sub // PROJ_ROWS

    def chunk(ref, m, cols=slice(None)):
        if sub >= PROJ_ROWS:
            r, l0 = divmod(m * PROJ_ROWS, sub)
            return ref.at[r, l0:l0 + PROJ_ROWS, cols]
        k = PROJ_ROWS // sub
        return ref.at[m * k:(m + 1) * k, :, cols]

    def emit(rope):
        def matmul(m):
            xm = chunk(x_ref, m)[...].reshape(PROJ_ROWS, x_ref.shape[2])
            return jnp.dot(xm, w_ref[...], preferred_element_type=F32)

        def finish(m, acc):
            if rope:
                c, sa, sb = (chunk(t, m)[...].reshape(PROJ_ROWS, LANES) for t in tab_refs)
            for j in range(tn // LANES):
                cl = slice(j * LANES, (j + 1) * LANES)
                t = acc[:, cl]
                if rope:
                    t = t * c + pltpu.roll(t, LANES - HALF, 1) * sa + pltpu.roll(t, HALF, 1) * sb
                dst = chunk(o_ref, m, cl)
                dst[...] = t.astype(o_ref.dtype).reshape(dst.shape)

        acc = matmul(0)
        for m in range(n_chunks):
            nxt = matmul(m + 1) if m + 1 < n_chunks else None
            finish(m, acc)
            acc = nxt

    if n_rope_tiles == 0 or n_rope_tiles == n_tiles:
        emit(n_rope_tiles > 0)
    else:
        pl.when(pl.program_id(1) < n_rope_tiles)(lambda: emit(True))
        pl.when(pl.program_id(1) >= n_rope_tiles)(lambda: emit(False))


def _pick_tile(n, candidates):
    for c in candidates:
        if n % c == 0:
            return c
    raise ValueError(f"no tile for {n}")


def _proj(x, w, out_dtype, rope_tabs=None, n_rope_cols=0, tn=None, w_cols=None):
    batch, dil, sub_len, k = x.shape
    layer, col0, n = w_cols if w_cols is not None else (None, 0, w.shape[-1])
    seq_len = dil * sub_len
    tm = _pick_tile(seq_len, (2048, 1024, 512, 256, 128))
    tn = tn or _pick_tile(np.gcd(n, n_rope_cols), (1024, 512, 384, 256, 128))
    sub = tm // dil
    assert n % tn == 0 and n_rope_cols % tn == 0 and tm % PROJ_ROWS == 0
    assert sub % 16 == 0 and (sub % PROJ_ROWS == 0 or PROJ_ROWS % sub == 0)
    n_seq_tiles = seq_len // tm
    assert col0 % tn == 0
    w_spec = (pl.BlockSpec((k, tn), lambda i, j: (0, j)) if layer is None else
              pl.BlockSpec((None, k, tn), lambda i, j: (layer, 0, j + col0 // tn)))
    in_specs = [pl.BlockSpec((None, dil, sub, k), lambda i, j: (i // n_seq_tiles, 0, i % n_seq_tiles, 0)), w_spec]
    args = [x, w]
    if n_rope_cols:
        tab_spec = pl.BlockSpec((dil, sub, LANES), lambda i, j: (0, i % n_seq_tiles, 0))
        in_specs += [tab_spec] * 3
        args += list(rope_tabs)
    return pl.pallas_call(
        functools.partial(_proj_kernel, n_rope_tiles=n_rope_cols // tn, n_tiles=n // tn),
        grid=(batch * n_seq_tiles, n // tn),
        in_specs=in_specs,
        out_specs=pl.BlockSpec((None, dil, sub, tn), lambda i, j: (i // n_seq_tiles, 0, i % n_seq_tiles, j)),
        out_shape=jax.ShapeDtypeStruct((batch, dil, sub_len, n), out_dtype),
        compiler_params=_params(2),
        name=f"proj_d{dil}_r{n_rope_cols}",
    )(*args)


def _band_kernel(*refs, tq, npv, max_dist, kv_pair_of, has_sinks, want_lse):
    it = iter(refs)
    q_ref, kc_ref, kp_ref, vc_ref, vp_ref = (next(it) for _ in range(5))
    sink_ref = next(it) if has_sinks else None
    o_ref = next(it)
    lse_ref = next(it) if want_lse else None
    qb = pl.program_id(2)
    w = (npv + 1) * QBLK
    pv_rows = npv * QBLK
    lane = lax.broadcasted_iota(jnp.int32, (QBLK, LANES), 1)
    first_half = lane < HEAD_DIM
    qi = lax.broadcasted_iota(jnp.int32, (QBLK, w), 0)
    kj = lax.broadcasted_iota(jnp.int32, (QBLK, w), 1)
    dist = qi + npv * QBLK - kj
    band = (dist >= 0) & (dist <= max_dist)
    for sub in range(tq // QBLK):
        r0 = sub * QBLK
        kstart = qb * tq + r0 - npv * QBLK
        mask = band & (kj + kstart >= 0)
        lse_tile = jnp.zeros((QBLK, LANES), F32)
        for g0 in range(0, N_PAIRS, BAND_GROUP):
            staged = []
            for pi in range(g0, g0 + BAND_GROUP):
                cl = slice(kv_pair_of(pi) * LANES, (kv_pair_of(pi) + 1) * LANES)
                k_parts, v_parts = [], []
                if r0 < pv_rows:
                    k_parts.append(kp_ref[r0:pv_rows, cl])
                    v_parts.append(vp_ref[r0:pv_rows, cl])
                cs = max(r0 - pv_rows, 0)
                k_parts.append(kc_ref[cs:r0 + QBLK, cl])
                v_parts.append(vc_ref[cs:r0 + QBLK, cl])
                kwin = k_parts[0] if len(k_parts) == 1 else jnp.concatenate(k_parts, axis=0)
                vwin = v_parts[0] if len(v_parts) == 1 else jnp.concatenate(v_parts, axis=0)
                qp = q_ref[r0:r0 + QBLK, pi * LANES:(pi + 1) * LANES]
                for e in range(2):
                    qe = jnp.where(first_half if e == 0 else jnp.logical_not(first_half), qp, 0)
                    s = lax.dot_general(qe, kwin, _NT, preferred_element_type=F32)
                    staged.append((pi, e, jnp.where(mask, s, NEG_INF), vwin))
            outs = {}
            for pi, e, s, vwin in staged:
                m = jnp.max(s, axis=-1, keepdims=True)
                if has_sinks:
                    sk = sink_ref[2 * pi + e]
                    m = jnp.maximum(m, sk)
                p = jnp.exp2(s - m)
                l = jnp.sum(p, axis=-1, keepdims=True)
                if has_sinks:
                    l = l + jnp.exp2(sk - m)
                pv = jnp.dot(p.astype(BF16), vwin, preferred_element_type=F32)
                outs[(pi, e)] = pv / l
                if want_lse:
                    lse_tile = jnp.where(lane == 2 * pi + e, m + jnp.log2(l), lse_tile)
            for pi in range(g0, g0 + BAND_GROUP):
                o_ref[r0:r0 + QBLK, pi * LANES:(pi + 1) * LANES] = jnp.where(
                    first_half, outs[(pi, 0)], outs[(pi, 1)])
        if want_lse:
            lse_ref[r0:r0 + QBLK, :] = lse_tile


def _banded_attention(q_arr, k_arr, v_arr, *, tq, npv, max_dist, q_off, k_off, v_off,
                      kv_pairs, kv_pair_of, sinks=None, want_lse=False):
    b, dil, l, _ = q_arr.shape
    qw, kw = N_PAIRS * LANES, kv_pairs * LANES
    pv = npv * QBLK
    assert tq % pv == 0
    in_specs = [
        pl.BlockSpec((None, None, tq, qw), lambda bi, r, i: (bi, r, i, q_off)),
        pl.BlockSpec((None, None, tq, kw), lambda bi, r, i: (bi, r, i, k_off)),
        pl.BlockSpec((None, None, pv, kw), lambda bi, r, i: (bi, r, jnp.maximum(i * (tq // pv) - 1, 0), k_off)),
        pl.BlockSpec((None, None, tq, kw), lambda bi, r, i: (bi, r, i, v_off)),
        pl.BlockSpec((None, None, pv, kw), lambda bi, r, i: (bi, r, jnp.maximum(i * (tq // pv) - 1, 0), v_off)),
    ]
    args = [q_arr, k_arr, k_arr, v_arr, v_arr]
    if sinks is not None:
        in_specs.append(pl.BlockSpec(memory_space=pltpu.SMEM))
        args.append(sinks)
    out_specs = [pl.BlockSpec((None, None, tq, qw), lambda bi, r, i: (bi, r, i, 0))]
    out_shape = [jax.ShapeDtypeStruct((b, dil, l, qw), F32)]
    if want_lse:
        out_specs.append(pl.BlockSpec((None, None, tq, LANES), lambda bi, r, i: (bi, r, i, 0)))
        out_shape.append(jax.ShapeDtypeStruct((b, dil, l, LANES), F32))
    res = pl.pallas_call(
        functools.partial(_band_kernel, tq=tq, npv=npv, max_dist=max_dist, kv_pair_of=kv_pair_of,
                          has_sinks=sinks is not None, want_lse=want_lse),
        grid=(b, dil, l // tq),
        in_specs=in_specs,
        out_specs=out_specs,
        out_shape=out_shape,
        compiler_params=_params(3),
        name=f"band_d{dil}_w{max_dist}",
    )(*args)
    return res if want_lse else res[0]


def _compress_kernel(c_ref, pos_ref, w_ref, o_ref):
    c = c_ref[...]
    top = jnp.dot((c + pos_ref[0:1, :]).astype(BF16), w_ref[0], preferred_element_type=F32)
    bot = jnp.dot((c + pos_ref[1:2, :]).astype(BF16), w_ref[1], preferred_element_type=F32)
    nc = c.shape[0]
    o_ref[...] = (top + pltpu.roll(bot, nc - 1, 0)).astype(o_ref.dtype)


def _compress(chunks, pos, w):
    b, hk, nc, cw = chunks.shape
    return pl.pallas_call(
        _compress_kernel,
        grid=(b, hk),
        in_specs=[pl.BlockSpec((None, None, nc, cw), lambda bi, h: (bi, h, 0, 0)),
                  pl.BlockSpec((2, cw), lambda bi, h: (0, 0)),
                  pl.BlockSpec((2, cw, HEAD_DIM), lambda bi, h: (0, 0, 0))],
        out_specs=pl.BlockSpec((None, None, nc, HEAD_DIM), lambda bi, h: (bi, h, 0, 0)),
        out_shape=jax.ShapeDtypeStruct((b, hk, nc, HEAD_DIM), BF16),
        compiler_params=_params(2),
        name="nsa_compress",
    )(chunks, pos, w)


def _stack_group_queries(q_ref, mp, half):
    pairs = [C_GROUP * mp + i for i in range(C_GROUP)]
    return jnp.concatenate(
        [jnp.where(half, q_ref[:, pr * LANES:(pr + 1) * LANES], 0) for pr in pairs], axis=0)


def _store_group_heads(o_ref, val, mp, e, first_half):
    for i in range(C_GROUP):
        ol = slice((C_GROUP * mp + i) * LANES, (C_GROUP * mp + i + 1) * LANES)
        rows = slice(i * QBLK, (i + 1) * QBLK)
        if e == 0:
            o_ref[:, ol] = val[rows]
        else:
            o_ref[:, ol] = jnp.where(first_half, o_ref[:, ol], val[rows])


def _nsa_cmp_kernel(q_ref, kc_ref, vc_ref, ocmp_ref, sel_ref, impt_ref, *, n_sel):
    nc = kc_ref.shape[0]
    t0 = pl.program_id(1) * QBLK
    lane = lax.broadcasted_iota(jnp.int32, (QBLK, LANES), 1)
    first_half = lane < HEAD_DIM
    pad = SUBLANES

    n_free = max(n_sel - 3, 0)

    def body(ncols):
        nv = ncols // SEL_PER_CMP
        assert nv >= n_sel
        jj = lax.broadcasted_iota(jnp.int32, (nv, QBLK), 0)
        cur = (t0 + lax.broadcasted_iota(jnp.int32, (nv, QBLK), 1)) // C_SEL_LEN
        forced = (jj == 0) | (jj == cur) | (jj == cur - 1)
        bvalid = jj <= cur
        qi_c = lax.broadcasted_iota(jnp.int32, (QBLK, ncols), 0)
        nn_c = lax.broadcasted_iota(jnp.int32, (QBLK, ncols), 1)
        cvalid = nn_c * C_CMP_STRIDE + (C_CMP_LEN - 1) <= t0 + qi_c
        cvalid = jnp.concatenate([cvalid] * C_GROUP, axis=0)
        impt_ref[0:pad, :] = jnp.zeros((pad, QBLK), F32)
        if ncols < nc:
            impt_ref[pad + ncols:pad + nc, :] = jnp.zeros((nc - ncols, QBLK), F32)

        scores = []
        for kh in range(C_KV_HEADS):
            mp, e = divmod(kh, 2)
            cl = slice(mp * LANES, (mp + 1) * LANES)
            half = first_half if e == 0 else jnp.logical_not(first_half)
            qst = _stack_group_queries(q_ref, mp, half)
            sc = lax.dot_general(qst, kc_ref[0:ncols, cl], _NT, preferred_element_type=F32)
            scores.append(jnp.where(cvalid, sc, NEG_INF))

        imps = []
        for kh in range(C_KV_HEADS):
            mp, e = divmod(kh, 2)
            cl = slice(mp * LANES, (mp + 1) * LANES)
            sc = scores[kh]
            mx = jnp.max(sc, axis=-1, keepdims=True)
            mx = jnp.where(mx > NEG_INF, mx, 0.0)
            ee = jnp.exp2(sc - mx)
            pc = ee / jnp.maximum(jnp.sum(ee, axis=-1, keepdims=True), 1e-30)
            ocmp = jnp.dot(pc.astype(BF16), vc_ref[0:ncols, cl], preferred_element_type=F32)
            _store_group_heads(ocmp_ref, ocmp, mp, e, first_half)
            imp = pc[0:QBLK]
            for g in range(1, C_GROUP):
                imp = imp + pc[g * QBLK:(g + 1) * QBLK]
            imps.append(imp)

        for kh in range(C_KV_HEADS):
            for c in range(ncols // QBLK):
                impt_ref[pad + c * QBLK:pad + (c + 1) * QBLK, :] = imps[kh][:, c * QBLK:(c + 1) * QBLK].T
            imp_s = C_SEL_OVERLAP[0] * impt_ref[pl.ds(pad - 1, nv, stride=SEL_PER_CMP), :]
            for o_off in range(1, len(C_SEL_OVERLAP)):
                imp_s = imp_s + C_SEL_OVERLAP[o_off] * impt_ref[pl.ds(pad - 1 + o_off, nv, stride=SEL_PER_CMP), :]
            score = jnp.where(forced, NEG_INF, jnp.where(bvalid, imp_s, -1.0))
            selt = jnp.where(forced, 1.0, 0.0)
            for _ in range(n_free):
                best = jnp.max(score, axis=0, keepdims=True)
                first = jnp.min(jnp.where(score == best, jj, nv), axis=0, keepdims=True)
                hit = jj == first
                selt = jnp.where(hit, 1.0, selt)
                score = jnp.where(hit, NEG_INF, score)
            sel_ref[kh, 0:nv, :] = selt
            if nv < LANES:
                sel_ref[kh, nv:LANES, :] = jnp.zeros((LANES - nv, QBLK), F32)

    col_step = min(CMP_COL_STEP, nc)
    n_variants = nc // col_step
    tokens_per_step = col_step * C_CMP_STRIDE
    for k in range(n_variants):
        pl.when(t0 // tokens_per_step == k)(functools.partial(body, (k + 1) * col_step))


def _nsa_slc_kernel(q_ref, sel_ref, ks_ref, vst_ref, oslc_ref, acc_ref):
    gq = C_GROUP * QBLK
    blocks_per_tile = SEL_TILE // C_SEL_LEN
    bias_rows = BF16_SUBLANES
    assert blocks_per_tile <= bias_rows
    t0 = pl.program_id(1) * QBLK
    lane = lax.broadcasted_iota(jnp.int32, (QBLK, LANES), 1)
    first_half = lane < HEAD_DIM
    row = lax.broadcasted_iota(jnp.int32, (LANES, QBLK), 0)
    top_rows = row < HEAD_DIM
    n_full = t0 // SEL_TILE
    key_in_tile = lax.broadcasted_iota(jnp.int32, (SEL_TILE, gq), 0)
    query_pos = t0 + lax.broadcasted_iota(jnp.int32, (SEL_TILE, gq), 1) % QBLK
    block_of_key = lax.broadcasted_iota(jnp.int32, (SEL_TILE, LANES), 0) // C_SEL_LEN
    block_onehot = jnp.where(
        block_of_key == lax.broadcasted_iota(jnp.int32, (SEL_TILE, LANES), 1), 1.0, 0.0).astype(BF16)
    bias_pad = jnp.zeros((LANES - bias_rows, gq), BF16)

    q_t = [q_ref[:, pr * LANES:(pr + 1) * LANES].astype(F32).T for pr in range(N_PAIRS)]
    qts = []
    for kh in range(C_KV_HEADS):
        mp, e = divmod(kh, 2)
        keep_rows = top_rows if e == 0 else jnp.logical_not(top_rows)
        qts.append(jnp.concatenate(
            [jnp.where(keep_rows, q_t[C_GROUP * mp + i], 0).astype(BF16) for i in range(C_GROUP)],
            axis=1))
    acc_ref[...] = jnp.zeros((C_KV_HEADS, LANES, gq), F32)

    def tile_step(kt, carry, diagonal):
        k0 = pl.multiple_of(kt * SEL_TILE, SEL_TILE)
        b0 = pl.multiple_of(kt * blocks_per_tile, blocks_per_tile)
        sts = []
        for kh in range(C_KV_HEADS):
            mp = kh // 2
            cl = slice(mp * LANES, (mp + 1) * LANES)
            keys = jnp.concatenate([ks_ref[pl.ds(k0, SEL_TILE), cl], block_onehot], axis=1)
            picked = sel_ref[kh, pl.ds(b0, blocks_per_tile), :]
            bias = jnp.concatenate([(picked - 1.0) * -MASKED] * C_GROUP, axis=1)
            bias = jnp.concatenate([bias, jnp.zeros((bias_rows - blocks_per_tile, gq), F32)], axis=0)
            queries = jnp.concatenate([qts[kh], bias.astype(BF16), bias_pad], axis=0)
            st = jnp.dot(keys, queries, preferred_element_type=F32)
            if diagonal:
                st = jnp.where(k0 + key_in_tile <= query_pos, st, MASKED)
            sts.append(st)
        new_carry = []
        for kh in range(C_KV_HEADS):
            mp = kh // 2
            cl = slice(mp * LANES, (mp + 1) * LANES)
            st = sts[kh]
            m_old = carry[kh]
            m_new = jnp.maximum(m_old, jnp.max(st, axis=0, keepdims=True))
            alpha = jnp.exp2(m_old - m_new)
            p = jnp.exp2(st - m_new)
            acc_ref[kh] = alpha * acc_ref[kh] + jnp.dot(
                vst_ref[kh % 2, kt, cl, :], p.astype(BF16), preferred_element_type=F32)
            new_carry += [m_new]
        return tuple(new_carry)

    init = (jnp.full((1, gq), NEG_INF, F32),) * C_KV_HEADS
    carry = lax.fori_loop(0, n_full, functools.partial(tile_step, diagonal=False), init)
    carry = lax.fori_loop(n_full, n_full + 1, functools.partial(tile_step, diagonal=True), carry)
    for mp in range(C_KV_HEADS // 2):
        ots = [acc_ref[2 * mp + e] / acc_ref[2 * mp + e, HEAD_DIM * (1 - e):HEAD_DIM * (1 - e) + 1, :]
               for e in range(2)]
        for i in range(C_GROUP):
            ol = slice((C_GROUP * mp + i) * LANES, (C_GROUP * mp + i + 1) * LANES)
            qs = slice(i * QBLK, (i + 1) * QBLK)
            oslc_ref[:, ol] = jnp.where(first_half, ots[0][:, qs].T, ots[1][:, qs].T)


def _nsa_select(q_arr, kcmp, vcmp, ks_arr, vst_arr, *, ks_off):
    b, s, _ = q_arr.shape
    nc = kcmp.shape[1]
    ns = nc // SEL_PER_CMP
    assert ns <= LANES and s % SEL_TILE == 0
    qw = N_PAIRS * LANES
    gq = C_GROUP * QBLK
    q_spec = pl.BlockSpec((None, QBLK, qw), lambda bi, i: (bi, i, 0))
    o_spec = pl.BlockSpec((None, QBLK, qw), lambda bi, i: (bi, i, 0))
    sel_spec = pl.BlockSpec((None, C_KV_HEADS, LANES, QBLK), lambda bi, i: (bi, 0, 0, i))
    o_shape = jax.ShapeDtypeStruct((b, s, qw), F32)
    o_cmp, sel = pl.pallas_call(
        functools.partial(_nsa_cmp_kernel, n_sel=min(C_N_SEL, ns)),
        grid=(b, s // QBLK),
        in_specs=[q_spec,
                  pl.BlockSpec((None, nc, C_KV), lambda bi, i: (bi, 0, 0)),
                  pl.BlockSpec((None, nc, C_KV), lambda bi, i: (bi, 0, 0))],
        out_specs=[o_spec, sel_spec],
        out_shape=[o_shape, jax.ShapeDtypeStruct((b, C_KV_HEADS, LANES, s), F32)],
        scratch_shapes=[pltpu.VMEM((SUBLANES + nc, QBLK), F32)],
        compiler_params=_params(2),
        name="nsa_compressed",
    )(q_arr, kcmp, vcmp)
    o_slc = pl.pallas_call(
        _nsa_slc_kernel,
        grid=(b, s // QBLK),
        in_specs=[q_spec,
                  sel_spec,
                  pl.BlockSpec((None, s, C_KV), lambda bi, i: (bi, 0, ks_off)),
                  pl.BlockSpec((None, 2, s // SEL_TILE, C_KV, SEL_TILE), lambda bi, i: (bi, 0, 0, 0, 0))],
        out_specs=o_spec,
        out_shape=o_shape,
        scratch_shapes=[pltpu.VMEM((C_KV_HEADS, LANES, gq), F32)],
        compiler_params=_params(2),
        name="nsa_selected",
    )(q_arr, sel, ks_arr, vst_arr)
    return o_cmp, o_slc


def _head_cols(tile, col, first_half):
    tm = tile.shape[0]
    a = jnp.broadcast_to(tile[:, col:col + 1], (tm, LANES))
    b = jnp.broadcast_to(tile[:, col + 1:col + 2], (tm, LANES))
    return jnp.where(first_half, a, b)


def _post_kernel(*refs, kind, n_staged, n_dil_outs):
    it = iter(refs)
    n_branch = {"A": 3, "B": 1, "C": 3}[kind]
    o_refs = [next(it) for _ in range(n_branch)]
    aux_refs = [next(it) for _ in range(3)] if kind == "A" else []
    x_ref, xb_ref, p_ref, wz_ref = (next(it) for _ in range(4))
    wgl_ref = next(it) if kind == "C" else None
    wo_ref, g_ref, b_ref, wg_ref, wp_ref, out_ref, outb_ref = (next(it) for _ in range(7))
    dil_out_refs = [next(it) for _ in range(n_dil_outs)]
    u_ref, z_ref = next(it), next(it)
    stage_refs = [next(it) for _ in range(n_staged + (1 if n_dil_outs else 0))]
    tm = x_ref.shape[0]

    def token_order(ref):
        if len(ref.shape) == 2:
            return lambda j: ref[:, j * LANES:(j + 1) * LANES]
        dil = ref.shape[0]
        stage = stage_refs.pop(0)
        for j in range(ref.shape[2] // LANES):
            for r in range(dil):
                stage[j, pl.ds(r, tm // dil, stride=dil), :] = ref[r, :, j * LANES:(j + 1) * LANES]
        return lambda j: stage[j]

    o_cols = [token_order(r) for r in o_refs]
    aux_cols = [token_order(r) for r in aux_refs]

    n_chunks = tm // POST_ROWS
    rows = [slice(c * POST_ROWS, (c + 1) * POST_ROWS) for c in range(n_chunks)]
    first_half = lax.broadcasted_iota(jnp.int32, (POST_ROWS, LANES), 1) < HEAD_DIM
    gates = [None] * n_chunks
    for c, rs in enumerate(rows):
        xb = xb_ref[rs, :]
        z_ref[rs, :] = jnp.dot(xb, wz_ref[...], preferred_element_type=F32)
        if kind == "C":
            gates[c] = jax.nn.sigmoid(jnp.dot(xb, wgl_ref[...], preferred_element_type=F32))

    hs = []
    for c, rs in enumerate(rows):
        if kind == "A":
            lses = [col(0)[rs] for col in aux_cols]
            mx = jnp.maximum(jnp.maximum(lses[0], lses[1]), lses[2])
            ws = [jnp.exp2(v - mx) for v in lses]
            den = ws[0] + ws[1] + ws[2]
            ws = [v / den for v in ws]
        for pi in range(N_PAIRS):
            cl = slice(pi * LANES, (pi + 1) * LANES)
            if kind == "A":
                o_last = o_cols[2](pi)[rs]
                o = o_last
                for g in range(2):
                    o = o + _head_cols(ws[g], 2 * pi, first_half) * (o_cols[g](pi)[rs] - o_last)
            elif kind == "B":
                o = o_cols[0](pi)[rs]
            else:
                o = _head_cols(gates[c], 2 * pi, first_half) * o_cols[0](pi)[rs]
                for br in range(1, 3):
                    o = o + _head_cols(gates[c], br * N_HEADS + 2 * pi, first_half) * o_cols[br](pi)[rs]
            z = z_ref[rs, cl]
            u_ref[rs, cl] = (o * (z * jax.nn.sigmoid(z))).astype(BF16)
        hs.append(jnp.dot(u_ref[rs, :], wo_ref[...], preferred_element_type=F32))

    pre = []
    for c, rs in enumerate(rows):
        y = DEEPNORM_ALPHA * x_ref[rs, :] + hs[c]
        mu = jnp.mean(y, axis=-1, keepdims=True)
        yc = y - mu
        var = jnp.mean(yc * yc, axis=-1, keepdims=True)
        yn = yc * lax.rsqrt(var + LN_EPS) * g_ref[...] + b_ref[...]
        gate_logits = jnp.dot(yn.astype(BF16), wg_ref[...], preferred_element_type=F32)
        pp = jnp.dot(p_ref[rs, :].astype(BF16), wp_ref[...], preferred_element_type=F32)
        pre.append((yn, gate_logits, pp))

    for c, rs in enumerate(rows):
        yn, gate_logits, pp = pre[c]
        x_new = yn + jax.nn.sigmoid(gate_logits) * pp
        out_ref[rs, :] = x_new
        outb_ref[rs, :] = x_new.astype(BF16)
        if dil_out_refs:
            for j in range(x_new.shape[1] // LANES):
                stage_refs[-1][j, rs, :] = x_new[:, j * LANES:(j + 1) * LANES]
    if dil_out_refs:
        stage = stage_refs[-1]
        for o_ref in dil_out_refs:
            dil = o_ref.shape[0]
            for j in range(x_new.shape[1] // LANES):
                for r in range(dil):
                    o_ref[r, :, j * LANES:(j + 1) * LANES] = stage[j, pl.ds(r, tm // dil, stride=dil), :].astype(BF16)


def _post(kind, o_list, aux_list, x, xb, p, w_z, w_gl, w_out, ln_g, ln_b, w_gate, w_proj, batch, out_dils=()):
    t, d = x.shape
    tm = POST_TILE
    row = lambda w: pl.BlockSpec((tm, w), lambda i: (i, 0))
    full = lambda a: pl.BlockSpec(a.shape, lambda i: (0,) * a.ndim)

    branch_specs, stages = [], []
    for a in list(o_list) + list(aux_list):
        if a.ndim == 2:
            branch_specs.append(row(a.shape[1]))
        else:
            _, dil, sub_len, w = a.shape
            n_seq_tiles = sub_len * dil // tm
            branch_specs.append(pl.BlockSpec(
                (None, dil, tm // dil, w), lambda i, n=n_seq_tiles: (i // n, 0, i % n, 0)))
            stages.append(pltpu.VMEM((w // LANES, tm, LANES), F32))
    n_staged = len(stages)
    if out_dils:
        stages.append(pltpu.VMEM((d // LANES, tm, LANES), F32))
    seq_tiles = t // batch // tm
    weights = [w_z] + ([w_gl] if kind == "C" else []) + [w_out, ln_g, ln_b, w_gate, w_proj]
    args = list(o_list) + list(aux_list) + [x, xb, p[0]] + weights
    p_arr, p_layer = p
    p_spec = pl.BlockSpec((None, tm, p_arr.shape[2]), lambda i: (p_layer, i, 0))
    in_specs = branch_specs + [row(d), row(d), p_spec] + [full(w) for w in weights]
    return pl.pallas_call(
        functools.partial(_post_kernel, kind=kind, n_staged=n_staged, n_dil_outs=len(out_dils)),
        grid=(t // tm,),
        in_specs=in_specs,
        out_specs=[row(d), row(d)] + [
            pl.BlockSpec((None, dil, tm // dil, d), lambda i, n=seq_tiles: (i // n, 0, i % n, 0)) for dil in out_dils],
        out_shape=[jax.ShapeDtypeStruct((t, d), F32), jax.ShapeDtypeStruct((t, d), BF16)] + [
            jax.ShapeDtypeStruct((batch, dil, t // batch // dil, d), BF16) for dil in out_dils],
        scratch_shapes=[pltpu.VMEM((tm, d), BF16), pltpu.VMEM((tm, d), F32)] + stages,
        compiler_params=_params(1),
        name=f"post_{kind}",
    )(*args)


def _cast_kernel(x_ref, xb_ref, *rest):
    dil_out_refs, stage = rest[:-1], rest[-1]
    tm, d = x_ref.shape
    x = x_ref[...]
    xb_ref[...] = x.astype(BF16)
    if dil_out_refs:
        for j in range(d // LANES):
            stage[j] = x[:, j * LANES:(j + 1) * LANES]
        for o_ref in dil_out_refs:
            dil = o_ref.shape[0]
            for j in range(d // LANES):
                for r in range(dil):
                    o_ref[r, :, j * LANES:(j + 1) * LANES] = stage[j, pl.ds(r, tm // dil, stride=dil), :].astype(BF16)


def _cast_stream(x, batch, out_dils):
    t, d = x.shape
    tm = POST_TILE
    seq_tiles = t // batch // tm
    row = pl.BlockSpec((tm, d), lambda i: (i, 0))
    return pl.pallas_call(
        _cast_kernel,
        grid=(t // tm,),
        in_specs=[row],
        out_specs=[row] + [
            pl.BlockSpec((None, dil, tm // dil, d), lambda i, n=seq_tiles: (i // n, 0, i % n, 0)) for dil in out_dils],
        out_shape=[jax.ShapeDtypeStruct((t, d), BF16)] + [
            jax.ShapeDtypeStruct((batch, dil, t // batch // dil, d), BF16) for dil in out_dils],
        scratch_shapes=[pltpu.VMEM((d // LANES, tm, LANES), F32)],
        compiler_params=_params(1),
        name="cast_stream",
    )(x)


def _rope_tables(seq_len):
    inv = 1.0 / (ROPE_THETA ** (jnp.arange(0, HEAD_DIM, 2, dtype=F32) / HEAD_DIM))
    ang = jnp.arange(seq_len, dtype=F32)[:, None] * inv[None, :]
    cos, sin = lax.optimization_barrier((jnp.cos(ang), jnp.sin(ang)))
    zero = jnp.zeros_like(sin)
    cos_t = jnp.concatenate([cos] * 4, axis=1)
    sa_t = jnp.concatenate([-sin, zero, -sin, zero], axis=1)
    sb_t = jnp.concatenate([zero, sin, zero, sin], axis=1)
    return cos_t, sa_t, sb_t


def _residue_major(a, dil):
    *lead, s, w = a.shape
    return jnp.swapaxes(a.reshape(*lead, s // dil, dil, w), -3, -2)


def _head_cols_index(head_order):
    return np.concatenate([np.arange(h * HEAD_DIM, (h + 1) * HEAD_DIM) for h in head_order])


_B_HEAD_ORDER = [e * (N_HEADS // B_KV_HEADS) + i for i in range(N_PAIRS) for e in range(2)]
_C_HEAD_ORDER = [C_GROUP * (2 * m + e) + i for m in range(C_KV_HEADS // 2) for i in range(C_GROUP) for e in range(2)]


def _prep_a_weights(a_w_in):
    wd = N_HEADS * HEAD_DIM
    col = np.arange(a_w_in.shape[-1])
    is_q = (col < 3 * len(A_GROUPS) * wd) & (col // wd % 3 == 0)
    scale = np.where(is_q, HEAD_DIM ** -0.5 * LOG2E, 1.0).astype(np.float32)
    return (a_w_in * scale).astype(BF16)


def _mixer_a(xbs, b, s, a_w, j, tabs):
    wd = N_HEADS * HEAD_DIM
    outs, lses = [], []
    for gi, (window, dil) in enumerate(A_GROUPS):
        qkv = _proj(xbs[dil], a_w, BF16, [_residue_major(t, dil) for t in tabs],
                    n_rope_cols=2 * wd, w_cols=(j, 3 * gi * wd, 3 * wd))
        o, lse = _banded_attention(
            qkv, qkv, qkv, tq=min(512, s // dil), npv=1, max_dist=window // dil,
            q_off=0, k_off=1, v_off=2, kv_pairs=N_PAIRS, kv_pair_of=lambda pi: pi, want_lse=True)
        if dil == 1:
            o, lse = o.reshape(b * s, wd), lse.reshape(b * s, LANES)
        outs.append(o)
        lses.append(lse)
    return outs, lses, a_w[j, :, -wd:]


def _mixer_b(xbs, b, s, w_in, sinks, tabs):
    wd = N_HEADS * HEAD_DIM
    kvw = B_KV_HEADS * HEAD_DIM
    perm = _head_cols_index(_B_HEAD_ORDER)
    wq = w_in[:, :wd][:, perm] * (HEAD_DIM ** -0.5 * LOG2E)
    wk = w_in[:, wd:wd + kvw]
    wv = w_in[:, wd + kvw:wd + 2 * kvw]
    wz = w_in[:, wd + 2 * kvw:][:, perm]
    tabs1 = [t[None] for t in tabs]
    qk = _proj(xbs[1], jnp.concatenate([wq, wk], axis=1).astype(BF16), BF16, tabs1, n_rope_cols=wd + kvw)
    v = _proj(xbs[1], wv.astype(BF16), BF16)
    o = _banded_attention(
        qk, qk, v, tq=min(256, s), npv=1, max_dist=B_WINDOW - 1,
        q_off=0, k_off=wd // kvw, v_off=0, kv_pairs=1, kv_pair_of=lambda pi: 0,
        sinks=sinks[np.asarray(_B_HEAD_ORDER)].astype(F32) * LOG2E)
    return [o.reshape(b * s, wd)], [], wz, perm


def _mixer_c(xbs, b, s, w_in, w_ck, w_cv, pos, tabs):
    wd = N_HEADS * HEAD_DIM
    perm = _head_cols_index(_C_HEAD_ORDER)
    cols = np.cumsum([0, wd] + [C_KV] * 6 + [3 * N_HEADS, wd])
    part = lambda i: w_in[:, cols[i]:cols[i + 1]]
    wq = part(0)[:, perm] * (HEAD_DIM ** -0.5 * LOG2E)
    w_gl = part(7).reshape(-1, 3, N_HEADS)[:, :, np.asarray(_C_HEAD_ORDER)].reshape(-1, 3 * N_HEADS)
    w_gl = jnp.pad(w_gl, ((0, 0), (0, LANES - 3 * N_HEADS)))
    w_att = jnp.concatenate([wq, part(3), part(5), part(4), part(6)], axis=1)
    tabs1 = [t[None] for t in tabs]
    att = _proj(xbs[1], w_att.astype(BF16), BF16, tabs1, n_rope_cols=wd + 2 * C_KV)
    cmp_in = _proj(xbs[1], jnp.concatenate([part(1), part(2)], axis=1).astype(BF16), F32, tabs1,
                   n_rope_cols=C_KV, tn=C_KV).reshape(b, s, 2 * C_KV)

    nc = s // C_CMP_STRIDE
    cw = C_CMP_STRIDE * HEAD_DIM

    def chunks(t):
        return jnp.transpose(t.reshape(b, s, C_KV_HEADS, HEAD_DIM), (0, 2, 1, 3)).reshape(b, C_KV_HEADS, nc, cw)

    pos2 = pos.reshape(2, cw)

    def compressed(t, w):
        c = _compress(chunks(t), pos2, w.reshape(2, cw, HEAD_DIM).astype(BF16))
        return jnp.transpose(c, (0, 2, 1, 3)).reshape(b, nc, C_KV)

    kcmp, vcmp = compressed(cmp_in[:, :, :C_KV], w_ck), compressed(cmp_in[:, :, C_KV:], w_cv)
    att3 = att.reshape(b, s, -1)
    vs = att3[:, :, wd + 2 * C_KV:wd + 3 * C_KV]
    vst = jnp.transpose(vs.reshape(b, s // SEL_TILE, SEL_TILE, C_KV), (0, 1, 3, 2))
    in_half0 = (jnp.arange(C_KV) % PAIR < HEAD_DIM)[None, None, :, None]
    vst = jnp.stack([jnp.where(in_half0, vst, 1), jnp.where(in_half0, 1, vst)], axis=1)
    o_cmp, o_slc = _nsa_select(att3, kcmp, vcmp, att3, vst, ks_off=wd // C_KV)
    o_win = _banded_attention(
        att, att, att, tq=C_WINDOW, npv=C_WINDOW // QBLK, max_dist=C_WINDOW - 1,
        q_off=0, k_off=wd // C_KV + 1, v_off=wd // C_KV + 3, kv_pairs=C_KV_HEADS // 2,
        kv_pair_of=lambda pi: pi // C_GROUP)
    outs = [o.reshape(b * s, wd) for o in (o_cmp, o_slc, o_win)]
    return outs, [], (part(8)[:, perm], w_gl), perm


def _layer(i, xt, xbs, b, s, tabs, p, a_w_in, a_w_out, b_w_in, b_sinks, b_w_out, c_w_in, c_w_ck, c_w_cv, c_pos,
           c_w_out, ln_g, ln_b, ple_w_proj, ple_w_gate):
    d = xt.shape[1]
    j, kind = divmod(i, N_MIXERS)
    w_gl = None
    if kind == 0:
        outs, aux, w_z = _mixer_a(xbs, b, s, a_w_in, j, tabs)
        w_out, name = a_w_out[j], "A"
    elif kind == 1:
        outs, aux, w_z, perm = _mixer_b(xbs, b, s, b_w_in[j], b_sinks[j], tabs)
        w_out, name = b_w_out[j][perm, :], "B"
    else:
        outs, aux, (w_z, w_gl), perm = _mixer_c(xbs, b, s, c_w_in[j], c_w_ck[j], c_w_cv[j], c_pos[j], tabs)
        w_out, name = c_w_out[j][perm, :], "C"
        w_gl = w_gl.astype(BF16)
    next_dils = _layer_dils(i + 1)
    res = _post(name, outs, aux, xt, xbs[1].reshape(b * s, d), (p.reshape(p.shape[0], b * s, -1), i), w_z.astype(BF16), w_gl,
                w_out.astype(BF16), ln_g[i].reshape(1, d), ln_b[i].reshape(1, d),
                ple_w_gate[i].astype(BF16), ple_w_proj[i].astype(BF16), b, out_dils=next_dils)
    new_xbs = {1: res[1].reshape(b, 1, s, d)}
    new_xbs.update(zip(next_dils, res[2:]))
    return res[0], new_xbs


def _layer_dils(i):
    if i < DEPTH and i % N_MIXERS == 0:
        return tuple(dil for _, dil in A_GROUPS if dil > 1)
    return ()


def kernel(x, p, a_w_in, a_w_out, b_w_in, b_sinks, b_w_out, c_w_in, c_w_ck, c_w_cv, c_pos, c_w_out,
           ln_g, ln_b, ple_w_proj, ple_w_gate):
    b, s, d = x.shape
    assert d == N_HEADS * HEAD_DIM and s % (QBLK * A_GROUPS[-1][1]) == 0 and s % C_WINDOW == 0
    tabs = _rope_tables(s)
    xt = x.reshape(b * s, d)
    dils0 = _layer_dils(0)
    copies = _cast_stream(xt, b, dils0)
    xbs = {1: copies[0].reshape(b, 1, s, d)}
    xbs.update(zip(dils0, copies[1:]))
    a_w_in = _prep_a_weights(a_w_in)
    for i in range(DEPTH):
        xt, xbs = _layer(i, xt, xbs, b, s, tabs, p, a_w_in, a_w_out, b_w_in, b_sinks, b_w_out, c_w_in, c_w_ck,
                         c_w_cv, c_pos, c_w_out, ln_g, ln_b, ple_w_proj, ple_w_gate)
    return xt.reshape(b, s, d)
```
